```python
import jax, jax.numpy as jnp
from jax import lax
import numpy as np

D_MODEL = 1024
BATCH = 2
SEQ = 16384
DEPTH = 1

CTX_LEN = 256
GRID_W = 64
EPS = 1e-6
D_RNN = 512
RNN_BLOCKS = 8
RNN_BLOCK_W = D_RNN // RNN_BLOCKS
RNN_CONV_W = 4
RNN_CONV_LEFT = 2
LRU_C = 8.0
HEAD_DIM = 64
N_Q_HEADS = 8
N_KV_HEADS = 2
Q_PER_KV = N_Q_HEADS // N_KV_HEADS
D_ATTN = N_Q_HEADS * HEAD_DIM
D_KV = N_KV_HEADS * HEAD_DIM
D_MIX = D_RNN + D_ATTN
D_IN = 2 * D_RNN + D_ATTN + 2 * D_KV
IN_SPLITS = (D_RNN, 2 * D_RNN, 2 * D_RNN + D_ATTN, 2 * D_RNN + D_ATTN + D_KV)
WINDOW = 128
BLOCK_Q = 128
ROPE_THETA = 10000.0
NEG_INF = -1e30
D_FF = 2816
FFN_CONV_W = 3
FFN_CONV_LEFT = 1

kernel_name = "hybrid_rglru_swa_convffn_dit_layer"


def rms_norm(x, g):
    xf = x.astype(jnp.float32)
    y = xf * lax.rsqrt(jnp.mean(xf * xf, axis=-1, keepdims=True) + EPS)
    return (y * g.astype(jnp.float32)).astype(x.dtype)


def adaln(cond, w_mod, b_mod):
    m = (jax.nn.silu(cond) @ w_mod + b_mod)[..., None, :]
    return jnp.split(m, 6, axis=-1)


def modulate(h, shift, scale):
    return h * (1 + scale) + shift


def dwconv(x, w, b, left):
    k = w.shape[0]
    s = x.shape[1]
    xp = jnp.pad(x, ((0, 0), (left, k - 1 - left), (0, 0)))
    out = b
    for j in range(k):
        out = out + xp[:, j:j + s] * w[j]
    return out


def rope_1d(x, pos):
    half = x.shape[-1] // 2
    inv = ROPE_THETA ** (-jnp.arange(half, dtype=jnp.float32) / half)
    ang = pos.astype(jnp.float32)[:, None] * inv[None, :]
    cos = jnp.cos(ang)[:, None, :].astype(x.dtype)
    sin = jnp.sin(ang)[:, None, :].astype(x.dtype)
    x1, x2 = x[..., :half], x[..., half:]
    return jnp.concatenate([x1 * cos - x2 * sin, x2 * cos + x1 * sin], axis=-1)


def rope_2d(x, row, col):
    d = HEAD_DIM // 2
    return jnp.concatenate([rope_1d(x[..., :d], row), rope_1d(x[..., d:], col)], axis=-1)


def linear_scan(a, u, h0):
    def combine(l, r):
        return l[0] * r[0], r[0] * l[1] + r[1]
    a_cum, b_cum = lax.associative_scan(combine, (a, u), axis=1)
    return a_cum * h0[:, None, :] + b_cum


def rglru_coeffs(xr, w_a, b_a, w_i, b_i, lam):
    xf = xr.astype(jnp.float32)
    xb = xf.reshape(xf.shape[:-1] + (RNN_BLOCKS, RNN_BLOCK_W))
    r = jax.nn.sigmoid(jnp.einsum('bsnc,ncd->bsnd', xb, w_a.astype(jnp.float32)).reshape(xf.shape) + b_a.astype(jnp.float32))
    i = jax.nn.sigmoid(jnp.einsum('bsnc,ncd->bsnd', xb, w_i.astype(jnp.float32)).reshape(xf.shape) + b_i.astype(jnp.float32))
    log_a = -LRU_C * r * jax.nn.softplus(-lam.astype(jnp.float32))
    a = jnp.exp(log_a)
    u = jnp.sqrt(-jnp.expm1(2.0 * log_a)) * (i * xf)
    return a, u


def rglru_bidir(x_lat, x_ctx, w_a, b_a, w_i, b_i, lam, with_ctx_out):
    y_lat = jnp.zeros(x_lat.shape, jnp.float32)
    y_ctx = jnp.zeros(x_ctx.shape, jnp.float32)
    h0 = jnp.zeros((x_ctx.shape[0], D_RNN), jnp.float32)
    for d in range(2):
        a_c, u_c = rglru_coeffs(x_ctx, w_a[d], b_a[d], w_i[d], b_i[d], lam[d])
        a_l, u_l = rglru_coeffs(x_lat, w_a[d], b_a[d], w_i[d], b_i[d], lam[d])
        if d == 1:
            a_c, u_c, a_l, u_l = (jnp.flip(t, axis=1) for t in (a_c, u_c, a_l, u_l))
        h_c = linear_scan(a_c, u_c, h0)
        h_l = linear_scan(a_l, u_l, h_c[:, -1])
        if d == 1:
            h_c, h_l = jnp.flip(h_c, axis=1), jnp.flip(h_l, axis=1)
        y_lat = y_lat + h_l
        if with_ctx_out:
            y_ctx = y_ctx + h_c
    return y_lat.astype(x_lat.dtype), (y_ctx.astype(x_ctx.dtype) if with_ctx_out else None)


def windowed_attention(q, k, v, qc, kc, vc, sink, with_ctx_out):
    B, S = q.shape[:2]
    L = kc.shape[1]
    nb = S // BLOCK_Q
    scale = HEAD_DIM ** -0.5
    qb = q.reshape(B, nb, BLOCK_Q, N_KV_HEADS, Q_PER_KV, HEAD_DIM)

    def band(t):
        tp = jnp.pad(t, ((0, 0), (BLOCK_Q, BLOCK_Q), (0, 0), (0, 0)))
        tp = tp.reshape(B, nb + 2, BLOCK_Q, N_KV_HEADS, HEAD_DIM)
        return jnp.concatenate([tp[:, :-2], tp[:, 1:-1], tp[:, 2:]], axis=2)

    kb, vb = band(k), band(v)
    qpos = jnp.arange(nb)[:, None] * BLOCK_Q + jnp.arange(BLOCK_Q)[None, :]
    kpos = (jnp.arange(nb)[:, None] - 1) * BLOCK_Q + jnp.arange(3 * BLOCK_Q)[None, :]
    mask = ((jnp.abs(qpos[:, :, None] - kpos[:, None, :]) <= WINDOW)
            & (kpos[:, None, :] >= 0) & (kpos[:, None, :] < S))
    s_loc = jnp.einsum('bnqhgd,bnkhd->bnhgqk', qb, kb).astype(jnp.float32) * scale
    s_loc = jnp.where(mask[None, :, None, None], s_loc, NEG_INF)
    s_ctx = jnp.einsum('bnqhgd,blhd->bnhgql', qb, kc).astype(jnp.float32) * scale
    sink_hg = sink.astype(jnp.float32).reshape(N_KV_HEADS, Q_PER_KV)
    s_sink = jnp.broadcast_to(sink_hg[None, None, :, :, None, None], s_loc.shape[:-1] + (1,))
    p = jax.nn.softmax(jnp.concatenate([s_loc, s_ctx, s_sink], axis=-1), axis=-1)
    p_loc = p[..., :3 * BLOCK_Q].astype(v.dtype)
    p_ctx = p[..., 3 * BLOCK_Q:3 * BLOCK_Q + L].astype(v.dtype)
    o = jnp.einsum('bnhgqk,bnkhd->bnqhgd', p_loc, vb) + jnp.einsum('bnhgql,blhd->bnqhgd', p_ctx, vc)
    o_lat = o.reshape(B, S, D_ATTN)
    o_ctx = None
    if with_ctx_out:
        qcg = qc.reshape(B, L, N_KV_HEADS, Q_PER_KV, HEAD_DIM)
        sc = jnp.einsum('blhgd,bmhd->bhglm', qcg, kc).astype(jnp.float32) * scale
        sc_sink = jnp.broadcast_to(sink_hg[None, :, :, None, None], sc.shape[:-1] + (1,))
        pc = jax.nn.softmax(jnp.concatenate([sc, sc_sink], axis=-1), axis=-1)[..., :L].astype(vc.dtype)
        o_ctx = jnp.einsum('bhglm,bmhd->blhgd', pc, vc).reshape(B, L, D_ATTN)
    return o_lat, o_ctx


def token_mixer(h, hc, row, col, w_in, rnn_conv_w, rnn_conv_b, lru_w_a, lru_b_a, lru_w_i, lru_b_i,
                lru_lam, attn_sink, gn_rnn, gn_attn, w_out, with_ctx_out):
    B, S = h.shape[:2]
    L = hc.shape[1]
    xr, xg, q, k, v = jnp.split(h @ w_in, IN_SPLITS, axis=-1)
    xrc, xgc, qc, kc, vc = jnp.split(hc @ w_in, IN_SPLITS, axis=-1)
    xr = dwconv(xr, rnn_conv_w, rnn_conv_b, RNN_CONV_LEFT)
    xrc = dwconv(xrc, rnn_conv_w, rnn_conv_b, RNN_CONV_LEFT)
    y_l, y_c = rglru_bidir(xr, xrc, lru_w_a, lru_b_a, lru_w_i, lru_b_i, lru_lam, with_ctx_out)
    rnn_l = jax.nn.gelu(xg) * y_l
    q = rope_2d(q.reshape(B, S, N_Q_HEADS, HEAD_DIM), row, col)
    k = rope_2d(k.reshape(B, S, N_KV_HEADS, HEAD_DIM), row, col)
    v = v.reshape(B, S, N_KV_HEADS, HEAD_DIM)
    qc = qc.reshape(B, L, N_Q_HEADS, HEAD_DIM)
    kc = kc.reshape(B, L, N_KV_HEADS, HEAD_DIM)
    vc = vc.reshape(B, L, N_KV_HEADS, HEAD_DIM)
    a_l, a_c = windowed_attention(q, k, v, qc, kc, vc, attn_sink, with_ctx_out)
    out_l = jnp.concatenate([rms_norm(rnn_l, gn_rnn), rms_norm(a_l, gn_attn)], axis=-1) @ w_out
    out_c = None
    if with_ctx_out:
        rnn_c = jax.nn.gelu(xgc) * y_c
        out_c = jnp.concatenate([rms_norm(rnn_c, gn_rnn), rms_norm(a_c, gn_attn)], axis=-1) @ w_out
    return out_l, out_c


def conv_ffn(h, w_up, conv_w, conv_b, w_down):
    u = dwconv(h @ w_up, conv_w, conv_b, FFN_CONV_LEFT)
    gate, val = jnp.split(u, 2, axis=-1)
    return (jax.nn.silu(gate) * val) @ w_down


def setup_inputs(seed: int = 0) -> dict:
    key = jax.random.key(seed)
    ks = jax.random.split(key, 25)
    nrm = jax.random.normal
    f32 = jnp.float32
    a0 = jax.random.uniform(ks[14], (DEPTH, 2, D_RNN), f32, 0.9, 0.999)
    return {
        "x": nrm(ks[0], (BATCH, SEQ, D_MODEL), f32),
        "c": nrm(ks[1], (BATCH, D_MODEL), f32),
        "ctx": nrm(ks[2], (BATCH, CTX_LEN, D_MODEL), f32),
        "c_ctx": nrm(ks[3], (D_MODEL,), f32),
        "w_mod": nrm(ks[4], (DEPTH, D_MODEL, 6 * D_MODEL), f32) * (0.5 * D_MODEL ** -0.5),
        "b_mod": nrm(ks[5], (DEPTH, 6 * D_MODEL), f32) * 0.02,
        "norm1_g": 1.0 + 0.02 * nrm(ks[6], (DEPTH, D_MODEL), f32),
        "w_in": nrm(ks[7], (DEPTH, D_MODEL, D_IN), f32) * D_MODEL ** -0.5,
        "rnn_conv_w": nrm(ks[8], (DEPTH, RNN_CONV_W, D_RNN), f32) * RNN_CONV_W ** -0.5,
        "rnn_conv_b": nrm(ks[9], (DEPTH, D_RNN), f32) * 0.02,
        "lru_w_a": nrm(ks[10], (DEPTH, 2, RNN_BLOCKS, RNN_BLOCK_W, RNN_BLOCK_W), f32) * RNN_BLOCK_W ** -0.5,
        "lru_b_a": nrm(ks[11], (DEPTH, 2, D_RNN), f32) * 0.02,
        "lru_w_i": nrm(ks[12], (DEPTH, 2, RNN_BLOCKS, RNN_BLOCK_W, RNN_BLOCK_W), f32) * RNN_BLOCK_W ** -0.5,
        "lru_b_i": nrm(ks[13], (DEPTH, 2, D_RNN), f32) * 0.02,
        "lru_lam": jnp.log(a0) - jnp.log1p(-a0),
        "attn_sink": nrm(ks[15], (DEPTH, N_Q_HEADS), f32) * 0.5,
        "gn_rnn": 1.0 + 0.02 * nrm(ks[16], (DEPTH, D_RNN), f32),
        "gn_attn": 1.0 + 0.02 * nrm(ks[17], (DEPTH, D_ATTN), f32),
        "w_out": nrm(ks[18], (DEPTH, D_MIX, D_MODEL), f32) * D_MIX ** -0.5,
        "norm2_g": 1.0 + 0.02 * nrm(ks[19], (DEPTH, D_MODEL), f32),
        "w_up": nrm(ks[20], (DEPTH, D_MODEL, 2 * D_FF), f32) * D_MODEL ** -0.5,
        "ffn_conv_w": nrm(ks[21], (DEPTH, FFN_CONV_W, 2 * D_FF), f32) * FFN_CONV_W ** -0.5,
        "ffn_conv_b": nrm(ks[22], (DEPTH, 2 * D_FF), f32) * 0.02,
        "w_down": nrm(ks[23], (DEPTH, D_FF, D_MODEL), f32) * D_FF ** -0.5,
        "final_g": 1.0 + 0.02 * nrm(ks[24], (D_MODEL,), f32),
    }


def reference(x, c, ctx, c_ctx, w_mod, b_mod, norm1_g, w_in, rnn_conv_w, rnn_conv_b, lru_w_a, lru_b_a,
              lru_w_i, lru_b_i, lru_lam, attn_sink, gn_rnn, gn_attn, w_out, norm2_g, w_up, ffn_conv_w,
              ffn_conv_b, w_down, final_g):
    S = x.shape[1]
    rows = S // GRID_W
    row = jnp.repeat(jnp.arange(rows, dtype=jnp.int32), GRID_W)
    col = jnp.tile(jnp.arange(GRID_W, dtype=jnp.int32), rows)
    for l in range(DEPTH):
        update_ctx = l < DEPTH - 1
        sh1, sc1, g1, sh2, sc2, g2 = adaln(c, w_mod[l], b_mod[l])
        csh1, csc1, cg1, csh2, csc2, cg2 = adaln(c_ctx, w_mod[l], b_mod[l])
        h = modulate(rms_norm(x, norm1_g[l]), sh1, sc1)
        hc = modulate(rms_norm(ctx, norm1_g[l]), csh1, csc1)
        out_l, out_c = token_mixer(h, hc, row, col, w_in[l], rnn_conv_w[l], rnn_conv_b[l], lru_w_a[l],
                                   lru_b_a[l], lru_w_i[l], lru_b_i[l], lru_lam[l], attn_sink[l],
                                   gn_rnn[l], gn_attn[l], w_out[l], update_ctx)
        x = x + g1 * out_l
        x = x + g2 * conv_ffn(modulate(rms_norm(x, norm2_g[l]), sh2, sc2), w_up[l], ffn_conv_w[l],
                              ffn_conv_b[l], w_down[l])
        if update_ctx:
            ctx = ctx + cg1 * out_c
            ctx = ctx + cg2 * conv_ffn(modulate(rms_norm(ctx, norm2_g[l]), csh2, csc2), w_up[l],
                                       ffn_conv_w[l], ffn_conv_b[l], w_down[l])
    return rms_norm(x, final_g)
```

```python
import functools

import jax
import jax.numpy as jnp
import numpy as np
from jax import lax
from jax.experimental import pallas as pl
from jax.experimental.pallas import tpu as pltpu

F32 = jnp.float32
BF16 = jnp.bfloat16

EPS = 1e-6
GRID_W = 64
D_RNN = 512
RNN_BLOCKS = 8
RNN_BLOCK_W = D_RNN // RNN_BLOCKS
RNN_CONV_W = 4
RNN_CONV_LEFT = 2
LRU_C = 8.0
HEAD_DIM = 64
N_Q_HEADS = 8
N_KV_HEADS = 2
Q_PER_KV = N_Q_HEADS // N_KV_HEADS
D_ATTN = N_Q_HEADS * HEAD_DIM
D_KV = N_KV_HEADS * HEAD_DIM
WINDOW = 128
BLOCK_Q = 128
ROPE_THETA = 10000.0
NEG_INF = -1e30
FFN_CONV_W = 3
FFN_CONV_LEFT = 1

LANES = 128
SUBLANES_F32 = 8
SUBLANES_BF16 = 16
MXU_DIM = 256
VMEM_LIMIT = 56 * 1024 * 1024

TM_PROJ = 512
TM_SCAN = 1024
TM_FFN = 512
FFN_CHUNK = 256

_HEAD_ORDER = (0, 4, 1, 5, 2, 6, 3, 7)
_HEAD_PERM = np.concatenate([np.arange(HEAD_DIM) + HEAD_DIM * h for h in _HEAD_ORDER])


def _params(*sem):
    return pltpu.CompilerParams(dimension_semantics=sem, vmem_limit_bytes=VMEM_LIMIT)


def _const_spec(shape):
    zeros = (0,) * len(shape)
    return pl.BlockSpec(shape, lambda *_: zeros)


def _mod_kernel(cond_ref, w_ref, b_ref, o_ref):
    s = cond_ref[...]
    s = s * jax.nn.sigmoid(s)
    o_ref[...] = jnp.dot(s, w_ref[...], preferred_element_type=F32,
                         precision=lax.Precision.HIGHEST) + b_ref[...]


def _modulation(cond, w_mod, b_mod):
    rows, d = cond.shape
    n = w_mod.shape[1]
    tn = 768
    return pl.pallas_call(
        _mod_kernel,
        grid=(n // tn,),
        in_specs=[pl.BlockSpec((rows, d), lambda j: (0, 0)),
                  pl.BlockSpec((d, tn), lambda j: (0, j)),
                  pl.BlockSpec((1, tn), lambda j: (0, j))],
        out_specs=pl.BlockSpec((rows, tn), lambda j: (0, j)),
        out_shape=jax.ShapeDtypeStruct((rows, n), F32),
        compiler_params=_params("arbitrary"),
        name="modulation",
    )(cond, w_mod, b_mod.reshape(1, n))


def _rope_partner(t):
    lane = lax.broadcasted_iota(jnp.int32, t.shape, 1)
    first = (lane % 32) < 16
    return jnp.where(first, pltpu.roll(t, LANES - 16, 1), pltpu.roll(t, 16, 1))


def _inproj_kernel(x_ref, gsc_ref, sh_ref, w_ref, cos_ref, sin_ref,
                   xr_ref, gx_ref, q_ref, k_ref, v_ref, *, rope):
    x = x_ref[...]
    ms = jnp.mean(x * x, axis=-1, keepdims=True)
    h = x * lax.rsqrt(ms + EPS) * gsc_ref[...] + sh_ref[...]
    p = jnp.dot(h.astype(BF16), w_ref[...], preferred_element_type=F32)
    xr_ref[...] = p[:, :D_RNN]
    gx_ref[...] = jax.nn.gelu(p[:, D_RNN:2 * D_RNN])
    q0 = 2 * D_RNN
    k0 = q0 + D_ATTN
    v0 = k0 + D_KV
    cols = [p[:, q0 + LANES * c:q0 + LANES * (c + 1)] for c in range(D_ATTN // LANES)]
    k = p[:, k0:v0]
    if rope:
        cos = cos_ref[...]
        sin = sin_ref[...]
        cols = [t * cos + _rope_partner(t) * sin for t in cols]
        k = k * cos + _rope_partner(k) * sin
    scale = HEAD_DIM ** -0.5
    for c, t in enumerate(cols):
        q_ref[:, LANES * c:LANES * (c + 1)] = (t * scale).astype(BF16)
    k_ref[...] = k.astype(BF16)
    v_ref[...] = p[:, v0:v0 + D_KV].astype(BF16)


def _in_proj(x, gsc, sh, w_in, cos, sin, *, rope, tm):
    b, s, d = x.shape
    n = w_in.shape[1]
    nt = s // tm
    row = lambda bb, i: (bb, i, 0)
    vec = lambda bb, i: (bb, 0, 0)
    tab = lambda bb, i: (i, 0)
    outs = [(D_RNN, F32), (D_RNN, F32), (D_ATTN, BF16), (D_KV, BF16), (D_KV, BF16)]
    return pl.pallas_call(
        functools.partial(_inproj_kernel, rope=rope),
        grid=(b, nt),
        in_specs=[pl.BlockSpec((None, tm, d), row),
                  pl.BlockSpec((None, 1, d), vec),
                  pl.BlockSpec((None, 1, d), vec),
                  _const_spec((d, n)),
                  pl.BlockSpec((tm, LANES), tab),
                  pl.BlockSpec((tm, LANES), tab)],
        out_specs=[pl.BlockSpec((None, tm, w), row) for w, _ in outs],
        out_shape=[jax.ShapeDtypeStruct((b, s, w), dt) for w, dt in outs],
        compiler_params=_params("arbitrary", "arbitrary"),
        name="in_proj_rope" if rope else "in_proj_ctx",
    )(x, gsc, sh, w_in, cos, sin)


def _rglru_kernel(*refs, reverse, final, nt, tm):
    if final:
        (xr_ref, xp_ref, xn_ref, cw_ref, cb_ref, wg_ref, ba_ref, bi_ref, sp_ref, h0_ref,
         yo_ref, gx_ref, gn_ref, y_ref, hfin_ref, a_scr, u_scr, y_scr, h_scr) = refs
    else:
        (xr_ref, xp_ref, xn_ref, cw_ref, cb_ref, wg_ref, ba_ref, bi_ref, sp_ref, h0_ref,
         y_ref, hfin_ref, a_scr, u_scr, y_scr, h_scr) = refs
    i = pl.program_id(1)
    ti = (nt - 1 - i) if reverse else i

    @pl.when(i == 0)
    def _():
        h_scr[...] = h0_ref[...]

    xp = jnp.where(ti > 0, xp_ref[...], 0.0)
    xn = jnp.where(ti < nt - 1, xn_ref[...], 0.0)
    xe = jnp.concatenate([xp, xr_ref[...], xn], axis=0)
    cw = cw_ref[...]
    xc = cb_ref[...]
    base = SUBLANES_F32 - RNN_CONV_LEFT
    for j in range(RNN_CONV_W):
        xc = xc + xe[base + j:base + j + tm] * cw[j:j + 1]
    xb = xc.astype(BF16)

    for g in range(D_RNN // MXU_DIM):
        sl = slice(MXU_DIM * g, MXU_DIM * (g + 1))
        z = jnp.dot(xb[:, sl], wg_ref[g], preferred_element_type=F32)
        r = jax.nn.sigmoid(z[:, :MXU_DIM] + ba_ref[:, sl])
        gi = jax.nn.sigmoid(z[:, MXU_DIM:] + bi_ref[:, sl])
        log_a = -LRU_C * r * sp_ref[:, sl]
        a = jnp.exp(log_a)
        a_scr[:, sl] = a
        u_scr[:, sl] = jnp.sqrt(-jnp.tanh(log_a) * (1.0 + a * a)) * (gi * xc[:, sl])

    def step(t, h):
        tt = (tm - 1 - t) if reverse else t
        h = a_scr[pl.ds(tt, 1), :] * h + u_scr[pl.ds(tt, 1), :]
        y_scr[pl.ds(tt, 1), :] = h
        return h

    h = lax.fori_loop(0, tm, step, h_scr[...], unroll=8)
    h_scr[...] = h
    hfin_ref[...] = h

    if final:
        z = gx_ref[...] * (y_scr[...] + yo_ref[...])
        ms = jnp.mean(z * z, axis=-1, keepdims=True)
        y_ref[...] = (z * lax.rsqrt(ms + EPS) * gn_ref[...]).astype(BF16)
    else:
        y_ref[...] = y_scr[...]


def _rglru(xr, cw, cb, wg, ba, bi, sp, h0, *, reverse, tm, final_args=None):
    b, s, d = xr.shape
    nt = s // tm
    final = final_args is not None
    nblk = s // SUBLANES_F32
    per = tm // SUBLANES_F32
    tile = (lambda i: nt - 1 - i) if reverse else (lambda i: i)
    row = lambda bb, i: (bb, tile(i), 0)
    prev = lambda bb, i: (bb, jnp.maximum(tile(i) * per - 1, 0), 0)
    nxt = lambda bb, i: (bb, jnp.minimum((tile(i) + 1) * per, nblk - 1), 0)
    vec = lambda bb, i: (bb, 0, 0)
    in_specs = [pl.BlockSpec((None, tm, d), row),
                pl.BlockSpec((None, SUBLANES_F32, d), prev),
                pl.BlockSpec((None, SUBLANES_F32, d), nxt),
                _const_spec(cw.shape), _const_spec(cb.shape), _const_spec(wg.shape),
                _const_spec(ba.shape), _const_spec(bi.shape), _const_spec(sp.shape),
                pl.BlockSpec((None, 1, d), vec)]
    args = [xr, xr, xr, cw, cb, wg, ba, bi, sp, h0]
    if final:
        y_other, gx, gn = final_args
        in_specs += [pl.BlockSpec((None, tm, d), row), pl.BlockSpec((None, tm, d), row),
                     _const_spec(gn.shape)]
        args += [y_other, gx, gn]
    return pl.pallas_call(
        functools.partial(_rglru_kernel, reverse=reverse, final=final, nt=nt, tm=tm),
        grid=(b, nt),
        in_specs=in_specs,
        out_specs=[pl.BlockSpec((None, tm, d), row), pl.BlockSpec((None, 1, d), vec)],
        out_shape=[jax.ShapeDtypeStruct((b, s, d), BF16 if final else F32),
                   jax.ShapeDtypeStruct((b, 1, d), F32)],
        scratch_shapes=[pltpu.VMEM((tm, d), F32), pltpu.VMEM((tm, d), F32),
                        pltpu.VMEM((tm, d), F32), pltpu.VMEM((1, d), F32)],
        compiler_params=_params("arbitrary", "arbitrary"),
        name="rglru_" + ("bwd" if reverse else "fwd") + ("_final" if final else ""),
    )(*args)


def _attn_kernel(q_ref, kp_ref, kc_ref, kn_ref, vp_ref, vc_ref, vn_ref, kx_ref, vx_ref,
                 bias_ref, sink_ref, gn_ref, o_ref):
    ncol = D_ATTN // LANES
    lane = lax.broadcasted_iota(jnp.int32, (BLOCK_Q, LANES), 1)
    low = lane < HEAD_DIM
    zero = jnp.zeros((BLOCK_Q, LANES), BF16)
    cols = [q_ref[:, LANES * c:LANES * (c + 1)] for c in range(ncol)]
    qs = jnp.concatenate([jnp.where(low, t, zero) for t in cols]
                         + [jnp.where(low, zero, t) for t in cols], axis=0)
    keys = jnp.concatenate([kp_ref[...], kc_ref[...], kn_ref[...], kx_ref[...]], axis=0)
    vals = jnp.concatenate([vp_ref[...], vc_ref[...], vn_ref[...], vx_ref[...]], axis=0)
    s = lax.dot_general(qs, keys, (((1,), (1,)), ((), ())), preferred_element_type=F32)
    nk = s.shape[1]
    s = (s.reshape(N_Q_HEADS, BLOCK_Q, nk) + bias_ref[...][None]).reshape(N_Q_HEADS * BLOCK_Q, nk)
    sink = sink_ref[...]
    m = jnp.maximum(jnp.max(s, axis=-1, keepdims=True), sink)
    e = jnp.exp(s - m)
    den = jnp.sum(e, axis=-1, keepdims=True) + jnp.exp(sink - m)
    o = jnp.dot(e.astype(BF16), vals, preferred_element_type=F32) / den
    outs = [jnp.where(low, o[BLOCK_Q * c:BLOCK_Q * (c + 1)],
                      o[BLOCK_Q * (ncol + c):BLOCK_Q * (ncol + c + 1)]) for c in range(ncol)]
    ms = sum(jnp.sum(t * t, axis=-1, keepdims=True) for t in outs) * (1.0 / D_ATTN)
    inv = lax.rsqrt(ms + EPS)
    for c, t in enumerate(outs):
        sl = slice(LANES * c, LANES * (c + 1))
        o_ref[:, sl] = (t * inv * gn_ref[:, sl]).astype(BF16)


def _attention(q, k, v, kx, vx, bias, sink_rows, gn):
    b, s, _ = q.shape
    nb = s // BLOCK_Q
    lx = kx.shape[1]
    cur = lambda bb, n: (bb, n, 0)
    prev = lambda bb, n: (bb, jnp.maximum(n - 1, 0), 0)
    nxt = lambda bb, n: (bb, jnp.minimum(n + 1, nb - 1), 0)
    ctx = lambda bb, n: (bb, 0, 0)
    edge = lambda bb, n: (jnp.where(n == 0, 0, jnp.where(n == nb - 1, 2, 1)), 0, 0)
    kv = lambda im: pl.BlockSpec((None, BLOCK_Q, D_KV), im)
    return pl.pallas_call(
        _attn_kernel,
        grid=(b, nb),
        in_specs=[pl.BlockSpec((None, BLOCK_Q, D_ATTN), cur),
                  kv(prev), kv(cur), kv(nxt), kv(prev), kv(cur), kv(nxt),
                  pl.BlockSpec((None, lx, D_KV), ctx), pl.BlockSpec((None, lx, D_KV), ctx),
                  pl.BlockSpec((None, BLOCK_Q, bias.shape[2]), edge),
                  _const_spec(sink_rows.shape), _const_spec(gn.shape)],
        out_specs=pl.BlockSpec((None, BLOCK_Q, D_ATTN), cur),
        out_shape=jax.ShapeDtypeStruct((b, s, D_ATTN), BF16),
        compiler_params=_params("arbitrary", "arbitrary"),
        name="attention",
    )(q, k, k, k, v, v, v, kx, vx, bias, sink_rows, gn)


def _attn_bias(lx):
    i = np.arange(BLOCK_Q)[:, None]
    j = np.arange(3 * BLOCK_Q)[None, :]
    band = np.abs(i + BLOCK_Q - j) <= WINDOW
    variants = [band & (j >= BLOCK_Q), band, band & (j < 2 * BLOCK_Q)]
    out = np.zeros((3, BLOCK_Q, 3 * BLOCK_Q + lx), np.float32)
    for n, ok in enumerate(variants):
        out[n, :, :3 * BLOCK_Q] = np.where(ok, 0.0, NEG_INF)
    return jnp.asarray(out)


def _outproj_kernel(rn_ref, an_ref, x_ref, wr_ref, wa_ref, g1_ref, gsc_ref, sh_ref, x1_ref, h2_ref):
    o = (jnp.dot(rn_ref[...], wr_ref[...], preferred_element_type=F32)
         + jnp.dot(an_ref[...], wa_ref[...], preferred_element_type=F32))
    x1 = x_ref[...] + g1_ref[...] * o
    x1_ref[...] = x1
    ms = jnp.mean(x1 * x1, axis=-1, keepdims=True)
    h2_ref[...] = (x1 * lax.rsqrt(ms + EPS) * gsc_ref[...] + sh_ref[...]).astype(BF16)


def _out_proj(rn, an, x, wr, wa, g1, gsc, sh, *, tm):
    b, s, d = x.shape
    row = lambda bb, i: (bb, i, 0)
    vec = lambda bb, i: (bb, 0, 0)
    return pl.pallas_call(
        _outproj_kernel,
        grid=(b, s // tm),
        in_specs=[pl.BlockSpec((None, tm, D_RNN), row), pl.BlockSpec((None, tm, D_ATTN), row),
                  pl.BlockSpec((None, tm, d), row), _const_spec(wr.shape), _const_spec(wa.shape),
                  pl.BlockSpec((None, 1, d), vec), pl.BlockSpec((None, 1, d), vec),
                  pl.BlockSpec((None, 1, d), vec)],
        out_specs=[pl.BlockSpec((None, tm, d), row), pl.BlockSpec((None, tm, d), row)],
        out_shape=[jax.ShapeDtypeStruct((b, s, d), F32), jax.ShapeDtypeStruct((b, s, d), BF16)],
        compiler_params=_params("arbitrary", "arbitrary"),
        name="out_proj",
    )(rn, an, x, wr, wa, g1, gsc, sh)


def _ffn_kernel(h_ref, hp_ref, hn_ref, x1_ref, wg_ref, wv_ref, cg_ref, cv_ref, bg_ref, bv_ref,
                wd_ref, g2_ref, fg_ref, o_ref, hext, acc, *, nt, tm, nc):
    i = pl.program_id(1)
    halo = SUBLANES_BF16
    zeros = jnp.zeros((halo, hext.shape[1]), BF16)
    hext[0:halo] = jnp.where(i > 0, hp_ref[...], zeros)
    hext[halo:halo + tm] = h_ref[...]
    hext[halo + tm:] = jnp.where(i < nt - 1, hn_ref[...], zeros)
    acc[...] = jnp.zeros_like(acc)
    base = halo - FFN_CONV_LEFT

    def conv(u, cw, cb):
        out = cb
        for j in range(FFN_CONV_W):
            out = out + u[base + j:base + j + tm] * cw[j:j + 1]
        return out

    def chunk(j, carry):
        he = hext[...]
        gate = conv(jnp.dot(he, wg_ref[j], preferred_element_type=F32), cg_ref[j], bg_ref[j])
        val = conv(jnp.dot(he, wv_ref[j], preferred_element_type=F32), cv_ref[j], bv_ref[j])
        act = (gate * jax.nn.sigmoid(gate) * val).astype(BF16)
        acc[...] += jnp.dot(act, wd_ref[j], preferred_element_type=F32)
        return carry

    lax.fori_loop(0, nc, chunk, 0)
    y = x1_ref[...] + g2_ref[...] * acc[...]
    ms = jnp.mean(y * y, axis=-1, keepdims=True)
    o_ref[...] = y * lax.rsqrt(ms + EPS) * fg_ref[...]


def _conv_ffn(h2, x1, wg, wv, cg, cv, bg, bv, wd, g2, fg, *, tm):
    b, s, d = x1.shape
    nt = s // tm
    nc = wg.shape[0]
    halo = SUBLANES_BF16
    per = tm // halo
    nblk = s // halo
    row = lambda bb, i: (bb, i, 0)
    prev = lambda bb, i: (bb, jnp.maximum(i * per - 1, 0), 0)
    nxt = lambda bb, i: (bb, jnp.minimum((i + 1) * per, nblk - 1), 0)
    vec = lambda bb, i: (bb, 0, 0)
    return pl.pallas_call(
        functools.partial(_ffn_kernel, nt=nt, tm=tm, nc=nc),
        grid=(b, nt),
        in_specs=[pl.BlockSpec((None, tm, d), row), pl.BlockSpec((None, halo, d), prev),
                  pl.BlockSpec((None, halo, d), nxt), pl.BlockSpec((None, tm, d), row),
                  _const_spec(wg.shape), _const_spec(wv.shape), _const_spec(cg.shape),
                  _const_spec(cv.shape), _const_spec(bg.shape), _const_spec(bv.shape),
                  _const_spec(wd.shape), pl.BlockSpec((None, 1, d), vec), _const_spec(fg.shape)],
        out_specs=pl.BlockSpec((None, tm, d), row),
        out_shape=jax.ShapeDtypeStruct((b, s, d), F32),
        scratch_shapes=[pltpu.VMEM((tm + 2 * halo, d), BF16), pltpu.VMEM((tm, d), F32)],
        compiler_params=_params("arbitrary", "arbitrary"),
        name="conv_ffn",
    )(h2, h2, h2, x1, wg, wv, cg, cv, bg, bv, wd, g2, fg)


def _rope_tables(s):
    half = HEAD_DIM // 4
    inv = ROPE_THETA ** (-jnp.arange(half, dtype=F32) / half)
    t = jnp.arange(s, dtype=jnp.int32)
    ang_r = (t // GRID_W).astype(F32)[:, None] * inv[None, :]
    ang_c = (t % GRID_W).astype(F32)[:, None] * inv[None, :]
    cos = jnp.concatenate([jnp.cos(ang_r)] * 2 + [jnp.cos(ang_c)] * 2, axis=1)
    sin = jnp.concatenate([-jnp.sin(ang_r), jnp.sin(ang_r), -jnp.sin(ang_c), jnp.sin(ang_c)], axis=1)
    reps = LANES // HEAD_DIM
    return jnp.tile(cos, (1, reps)), jnp.tile(sin, (1, reps))


def _gate_weights(w_a, w_i):
    per = MXU_DIM // RNN_BLOCK_W

    def dense(w):
        groups = []
        for g in range(RNN_BLOCKS // per):
            m = jnp.zeros((MXU_DIM, MXU_DIM), F32)
            for n in range(per):
                o = n * RNN_BLOCK_W
                m = m.at[o:o + RNN_BLOCK_W, o:o + RNN_BLOCK_W].set(w[g * per + n])
            groups.append(m)
        return jnp.stack(groups)

    return jnp.concatenate([dense(w_a), dense(w_i)], axis=2).astype(BF16)


def _chunks(w, nc):
    rows = w.shape[0]
    return w.reshape(rows, nc, w.shape[1] // nc).transpose(1, 0, 2)


def kernel(x, c, ctx, c_ctx, w_mod, b_mod, norm1_g, w_in, rnn_conv_w, rnn_conv_b, lru_w_a, lru_b_a,
           lru_w_i, lru_b_i, lru_lam, attn_sink, gn_rnn, gn_attn, w_out, norm2_g, w_up, ffn_conv_w,
           ffn_conv_b, w_down, final_g):
    assert w_mod.shape[0] == 1, "one layer: the last layer's context outputs are never consumed"
    b, s, d = x.shape
    lx = ctx.shape[1]
    d_ff = w_down.shape[1]
    nc = d_ff // FFN_CHUNK

    cond = jnp.concatenate([c, c_ctx[None], jnp.zeros((SUBLANES_F32 - b - 1, d), F32)], axis=0)
    mod = _modulation(cond, w_mod[0], b_mod[0])
    sh1, sc1, g1, sh2, sc2, g2 = [mod[:b, None, d * n:d * (n + 1)] for n in range(6)]
    csh1 = jnp.broadcast_to(mod[b:b + 1, None, :d], (b, 1, d))
    csc1 = jnp.broadcast_to(mod[b:b + 1, None, d:2 * d], (b, 1, d))
    gsc1 = norm1_g[0] * (1.0 + sc1)
    cgsc1 = norm1_g[0] * (1.0 + csc1)
    gsc2 = norm2_g[0] * (1.0 + sc2)

    wi = w_in[0]
    q0 = 2 * D_RNN
    wi = jnp.concatenate([wi[:, :q0], wi[:, q0:q0 + D_ATTN][:, _HEAD_PERM], wi[:, q0 + D_ATTN:]],
                         axis=1).astype(BF16)
    wo = w_out[0]
    wo_r = wo[:D_RNN].astype(BF16)
    wo_a = wo[D_RNN:][_HEAD_PERM].astype(BF16)
    gn_a = gn_attn[0][_HEAD_PERM][None]
    gn_r = gn_rnn[0][None]
    cw = rnn_conv_w[0]
    cb = rnn_conv_b[0][None]
    sp = jax.nn.softplus(-lru_lam[0])

    cos, sin = _rope_tables(s)
    xr, gx, q, k, v = _in_proj(x, gsc1, sh1, wi, cos, sin, rope=True, tm=TM_PROJ)
    xrc, _, _, kx, vx = _in_proj(ctx, cgsc1, csh1, wi, cos[:lx], sin[:lx], rope=False, tm=lx)

    h0 = jnp.zeros((b, 1, D_RNN), F32)
    y_dir = None
    for dr in range(2):
        wg = _gate_weights(lru_w_a[0, dr], lru_w_i[0, dr])
        gate_args = (cw, cb, wg, lru_b_a[0, dr][None], lru_b_i[0, dr][None], sp[dr][None])
        rev = dr == 1
        _, h_ctx = _rglru(xrc, *gate_args, h0, reverse=rev, tm=lx)
        fin = (y_dir, gx, gn_r) if rev else None
        y_dir, _ = _rglru(xr, *gate_args, h_ctx, reverse=rev, tm=TM_SCAN, final_args=fin)
    rn = y_dir

    sink_rows = jnp.repeat(attn_sink[0], BLOCK_Q)[:, None]
    an = _attention(q, k, v, kx, vx, _attn_bias(lx), sink_rows, gn_a)

    x1, h2 = _out_proj(rn, an, x, wo_r, wo_a, g1, gsc2, sh2, tm=TM_PROJ)

    wu = w_up[0].astype(BF16)
    wg_f = _chunks(wu[:, :d_ff], nc)
    wv_f = _chunks(wu[:, d_ff:], nc)
    fcw = ffn_conv_w[0]
    fcb = ffn_conv_b[0][None]
    cg_f, cv_f = _chunks(fcw[:, :d_ff], nc), _chunks(fcw[:, d_ff:], nc)
    bg_f, bv_f = _chunks(fcb[:, :d_ff], nc), _chunks(fcb[:, d_ff:], nc)
    wd_f = w_down[0].astype(BF16).reshape(nc, FFN_CHUNK, d)
    return _conv_ffn(h2, x1, wg_f, wv_f, cg_f, cv_f, bg_f, bv_f, wd_f, g2, final_g[None], tm=TM_FFN)
```

```python
import functools

import jax
import jax.numpy as jnp
import numpy as np
from jax import lax
from jax.experimental import pallas as pl
from jax.experimental.pallas import tpu as pltpu

F32 = jnp.float32
BF16 = jnp.bfloat16

EPS = 1e-6
GRID_W = 64
D_RNN = 512
RNN_BLOCKS = 8
RNN_BLOCK_W = D_RNN // RNN_BLOCKS
RNN_CONV_W = 4
RNN_CONV_LEFT = 2
LRU_C = 8.0
HEAD_DIM = 64
N_Q_HEADS = 8
N_KV_HEADS = 2
Q_PER_KV = N_Q_HEADS // N_KV_HEADS
D_ATTN = N_Q_HEADS * HEAD_DIM
D_KV = N_KV_HEADS * HEAD_DIM
WINDOW = 128
BLOCK_Q = 128
ROPE_THETA = 10000.0
NEG_INF = -1e30
LOG2E = 1.4426950408889634
FFN_CONV_W = 3
FFN_CONV_LEFT = 1

LANES = 128
SUBLANES_F32 = 8
SUBLANES_BF16 = 16
MXU_DIM = 256
VMEM_LIMIT = 56 * 1024 * 1024

TM_PROJ = 512
TM_SCAN = 1024
TM_FFN = 512
FFN_CHUNK = 256

_HEAD_ORDER = (0, 4, 1, 5, 2, 6, 3, 7)
_HEAD_PERM = np.concatenate([np.arange(HEAD_DIM) + HEAD_DIM * h for h in _HEAD_ORDER])


def _params(*sem):
    return pltpu.CompilerParams(dimension_semantics=sem, vmem_limit_bytes=VMEM_LIMIT)


def _sigmoid(x):
    return 0.5 * jnp.tanh(0.5 * x) + 0.5


def _const_spec(shape):
    zeros = (0,) * len(shape)
    return pl.BlockSpec(shape, lambda *_: zeros, pipeline_mode=pl.Buffered(1))


def _mod_kernel(cond_ref, w_ref, b_ref, o_ref):
    s = cond_ref[...]
    s = s * jax.nn.sigmoid(s)
    o_ref[...] = jnp.dot(s, w_ref[...], preferred_element_type=F32,
                         precision=lax.Precision.HIGHEST) + b_ref[...]


def _modulation(cond, w_mod, b_mod):
    rows, d = cond.shape
    n = w_mod.shape[1]
    tn = 768
    return pl.pallas_call(
        _mod_kernel,
        grid=(n // tn,),
        in_specs=[pl.BlockSpec((rows, d), lambda j: (0, 0)),
                  pl.BlockSpec((d, tn), lambda j: (0, j)),
                  pl.BlockSpec((1, tn), lambda j: (0, j))],
        out_specs=pl.BlockSpec((rows, tn), lambda j: (0, j)),
        out_shape=jax.ShapeDtypeStruct((rows, n), F32),
        compiler_params=_params("arbitrary"),
        name="modulation",
    )(cond, w_mod, b_mod.reshape(1, n))


def _rope_partner(t):
    lane = lax.broadcasted_iota(jnp.int32, t.shape, 1)
    first = (lane % 32) < 16
    return jnp.where(first, pltpu.roll(t, LANES - 16, 1), pltpu.roll(t, 16, 1))


def _inproj_kernel(x_ref, gsc_ref, sh_ref, w_ref, cos_ref, sin_ref,
                   xr_ref, gx_ref, q_ref, k_ref, v_ref, *, rope):
    x = x_ref[...]
    ms = jnp.mean(x * x, axis=-1, keepdims=True)
    h = x * lax.rsqrt(ms + EPS) * gsc_ref[...] + sh_ref[...]
    p = jnp.dot(h.astype(BF16), w_ref[...], preferred_element_type=F32)
    xr_ref[...] = p[:, :D_RNN]
    gx_ref[...] = jax.nn.gelu(p[:, D_RNN:2 * D_RNN])
    q0 = 2 * D_RNN
    k0 = q0 + D_ATTN
    v0 = k0 + D_KV
    cols = [p[:, q0 + LANES * c:q0 + LANES * (c + 1)] for c in range(D_ATTN // LANES)]
    k = p[:, k0:v0]
    if rope:
        cos = cos_ref[...]
        sin = sin_ref[...]
        cols = [t * cos + _rope_partner(t) * sin for t in cols]
        k = k * cos + _rope_partner(k) * sin
    scale = HEAD_DIM ** -0.5 * LOG2E
    for c, t in enumerate(cols):
        q_ref[:, LANES * c:LANES * (c + 1)] = (t * scale).astype(BF16)
    k_ref[...] = k.astype(BF16)
    v_ref[...] = p[:, v0:v0 + D_KV].astype(BF16)


def _in_proj(x, gsc, sh, w_in, cos, sin, *, rope, tm):
    b, s, d = x.shape
    n = w_in.shape[1]
    nt = s // tm
    row = lambda bb, i: (bb, i, 0)
    vec = lambda bb, i: (bb, 0, 0)
    tab = lambda bb, i: (i, 0)
    outs = [(D_RNN, F32), (D_RNN, F32), (D_ATTN, BF16), (D_KV, BF16), (D_KV, BF16)]
    return pl.pallas_call(
        functools.partial(_inproj_kernel, rope=rope),
        grid=(b, nt),
        in_specs=[pl.BlockSpec((None, tm, d), row),
                  pl.BlockSpec((None, 1, d), vec),
                  pl.BlockSpec((None, 1, d), vec),
                  _const_spec((d, n)),
                  pl.BlockSpec((tm, LANES), tab),
                  pl.BlockSpec((tm, LANES), tab)],
        out_specs=[pl.BlockSpec((None, tm, w), row) for w, _ in outs],
        out_shape=[jax.ShapeDtypeStruct((b, s, w), dt) for w, dt in outs],
        compiler_params=_params("arbitrary", "arbitrary"),
        name="in_proj_rope" if rope else "in_proj_ctx",
    )(x, gsc, sh, w_in, cos, sin)


def _rglru_kernel(*refs, reverse, final, nt, tm):
    if final:
        (xr_ref, xp_ref, xn_ref, cw_ref, cb_ref, wg_ref, ba_ref, bi_ref, sp_ref, h0_ref,
         yo_ref, gx_ref, gn_ref, y_ref, hfin_ref, xe_scr, a_scr, u_scr, y_scr, h_scr) = refs
    else:
        (xr_ref, xp_ref, xn_ref, cw_ref, cb_ref, wg_ref, ba_ref, bi_ref, sp_ref, h0_ref,
         y_ref, hfin_ref, xe_scr, a_scr, u_scr, y_scr, h_scr) = refs
    i = pl.program_id(1)
    ti = (nt - 1 - i) if reverse else i
    halo = SUBLANES_F32

    @pl.when(i == 0)
    def _():
        h_scr[...] = h0_ref[...]

    xe_scr[0:halo] = jnp.where(ti > 0, xp_ref[...], 0.0)
    xe_scr[halo:halo + tm] = xr_ref[...]
    xe_scr[halo + tm:] = jnp.where(ti < nt - 1, xn_ref[...], 0.0)
    cw = cw_ref[...]
    xc = cb_ref[...]
    base = halo - RNN_CONV_LEFT
    for j in range(RNN_CONV_W):
        xc = xc + xe_scr[base + j:base + j + tm, :] * cw[j:j + 1]
    xb = xc.astype(BF16)

    for g in range(D_RNN // MXU_DIM):
        sl = slice(MXU_DIM * g, MXU_DIM * (g + 1))
        z = jnp.dot(xb[:, sl], wg_ref[g], preferred_element_type=F32)
        r = _sigmoid(z[:, :MXU_DIM] + ba_ref[:, sl])
        gi = _sigmoid(z[:, MXU_DIM:] + bi_ref[:, sl])
        log_a = -LRU_C * r * sp_ref[:, sl]
        a = jnp.exp(log_a)
        a_scr[:, sl] = a
        u_scr[:, sl] = jnp.sqrt(-jnp.tanh(log_a) * (1.0 + a * a)) * (gi * xc[:, sl])

    def step(t, h):
        tt = (tm - 1 - t) if reverse else t
        h = a_scr[pl.ds(tt, 1), :] * h + u_scr[pl.ds(tt, 1), :]
        y_scr[pl.ds(tt, 1), :] = h
        return h

    h = lax.fori_loop(0, tm, step, h_scr[...], unroll=8)
    h_scr[...] = h
    hfin_ref[...] = h

    if final:
        z = gx_ref[...] * (y_scr[...] + yo_ref[...])
        ms = jnp.mean(z * z, axis=-1, keepdims=True)
        y_ref[...] = (z * lax.rsqrt(ms + EPS) * gn_ref[...]).astype(BF16)
    else:
        y_ref[...] = y_scr[...]


def _rglru(xr, cw, cb, wg, ba, bi, sp, h0, *, reverse, tm, final_args=None):
    b, s, d = xr.shape
    nt = s // tm
    final = final_args is not None
    nblk = s // SUBLANES_F32
    per = tm // SUBLANES_F32
    tile = (lambda i: nt - 1 - i) if reverse else (lambda i: i)
    row = lambda bb, i: (bb, tile(i), 0)
    prev = lambda bb, i: (bb, jnp.maximum(tile(i) * per - 1, 0), 0)
    nxt = lambda bb, i: (bb, jnp.minimum((tile(i) + 1) * per, nblk - 1), 0)
    vec = lambda bb, i: (bb, 0, 0)
    in_specs = [pl.BlockSpec((None, tm, d), row),
                pl.BlockSpec((None, SUBLANES_F32, d), prev),
                pl.BlockSpec((None, SUBLANES_F32, d), nxt),
                _const_spec(cw.shape), _const_spec(cb.shape), _const_spec(wg.shape),
                _const_spec(ba.shape), _const_spec(bi.shape), _const_spec(sp.shape),
                pl.BlockSpec((None, 1, d), vec)]
    args = [xr, xr, xr, cw, cb, wg, ba, bi, sp, h0]
    if final:
        y_other, gx, gn = final_args
        in_specs += [pl.BlockSpec((None, tm, d), row), pl.BlockSpec((None, tm, d), row),
                     _const_spec(gn.shape)]
        args += [y_other, gx, gn]
    return pl.pallas_call(
        functools.partial(_rglru_kernel, reverse=reverse, final=final, nt=nt, tm=tm),
        grid=(b, nt),
        in_specs=in_specs,
        out_specs=[pl.BlockSpec((None, tm, d), row), pl.BlockSpec((None, 1, d), vec)],
        out_shape=[jax.ShapeDtypeStruct((b, s, d), BF16 if final else F32),
                   jax.ShapeDtypeStruct((b, 1, d), F32)],
        scratch_shapes=[pltpu.VMEM((tm + 2 * SUBLANES_F32, d), F32),
                        pltpu.VMEM((tm, d), F32), pltpu.VMEM((tm, d), F32),
                        pltpu.VMEM((tm, d), F32), pltpu.VMEM((1, d), F32)],
        compiler_params=_params("arbitrary", "arbitrary"),
        name="rglru_" + ("bwd" if reverse else "fwd") + ("_final" if final else ""),
    )(*args)


def _attn_kernel(q_ref, kp_ref, kc_ref, kn_ref, vp_ref, vc_ref, vn_ref, kx_ref, vx_ref,
                 bias_ref, sink_ref, gn_ref, o_ref, keys, vals):
    ncol = D_ATTN // LANES
    nloc = 3 * BLOCK_Q
    nk = keys.shape[0]
    for n, (k_ref, v_ref) in enumerate(((kp_ref, vp_ref), (kc_ref, vc_ref), (kn_ref, vn_ref))):
        keys[BLOCK_Q * n:BLOCK_Q * (n + 1), :] = k_ref[...]
        vals[BLOCK_Q * n:BLOCK_Q * (n + 1), :D_KV] = v_ref[...]
    keys[nloc:, :] = kx_ref[...]
    vals[nloc:, :D_KV] = vx_ref[...]
    vals[:, D_KV:] = jnp.ones((nk, LANES), BF16)

    low = lax.broadcasted_iota(jnp.int32, (BLOCK_Q, LANES), 1) < HEAD_DIM
    first = lax.broadcasted_iota(jnp.int32, (2 * BLOCK_Q, 1), 0) < BLOCK_Q
    zero = jnp.zeros((BLOCK_Q, LANES), BF16)
    bias = bias_ref[...]
    outs = []
    for c in range(ncol):
        t = q_ref[:, LANES * c:LANES * (c + 1)]
        qs = jnp.concatenate([jnp.where(low, t, zero), jnp.where(low, zero, t)], axis=0)
        s = lax.dot_general(qs, keys[...], (((1,), (1,)), ((), ())), preferred_element_type=F32)
        s = (s.reshape(2, BLOCK_Q, nk) + bias[None]).reshape(2 * BLOCK_Q, nk)
        sink = jnp.where(first, sink_ref[c], sink_ref[ncol + c])
        m = jnp.maximum(jnp.max(s, axis=-1, keepdims=True), sink)
        e = jnp.exp2(s - m)
        pv = jnp.dot(e.astype(BF16), vals[...], preferred_element_type=F32)
        o = pv[:, :D_KV] / (pv[:, D_KV:] + jnp.exp2(sink - m))
        outs.append(jnp.where(low, o[:BLOCK_Q], o[BLOCK_Q:]))
    ms = sum(jnp.sum(t * t, axis=-1, keepdims=True) for t in outs) * (1.0 / D_ATTN)
    inv = lax.rsqrt(ms + EPS)
    for c, t in enumerate(outs):
        sl = slice(LANES * c, LANES * (c + 1))
        o_ref[:, sl] = (t * inv * gn_ref[:, sl]).astype(BF16)


def _attention(q, k, v, kx, vx, bias, sink, gn):
    b, s, _ = q.shape
    nb = s // BLOCK_Q
    lx = kx.shape[1]
    cur = lambda bb, n: (bb, n, 0)
    prev = lambda bb, n: (bb, jnp.maximum(n - 1, 0), 0)
    nxt = lambda bb, n: (bb, jnp.minimum(n + 1, nb - 1), 0)
    ctx = lambda bb, n: (bb, 0, 0)
    edge = lambda bb, n: (jnp.where(n == 0, 0, jnp.where(n == nb - 1, 2, 1)), 0, 0)
    kv = lambda im: pl.BlockSpec((None, BLOCK_Q, D_KV), im)
    return pl.pallas_call(
        _attn_kernel,
        grid=(b, nb),
        in_specs=[pl.BlockSpec((None, BLOCK_Q, D_ATTN), cur),
                  kv(prev), kv(cur), kv(nxt), kv(prev), kv(cur), kv(nxt),
                  pl.BlockSpec((None, lx, D_KV), ctx), pl.BlockSpec((None, lx, D_KV), ctx),
                  pl.BlockSpec((None, BLOCK_Q, bias.shape[2]), edge),
                  pl.BlockSpec(memory_space=pltpu.SMEM), _const_spec(gn.shape)],
        out_specs=pl.BlockSpec((None, BLOCK_Q, D_ATTN), cur),
        out_shape=jax.ShapeDtypeStruct((b, s, D_ATTN), BF16),
        scratch_shapes=[pltpu.VMEM((3 * BLOCK_Q + lx, D_KV), BF16),
                        pltpu.VMEM((3 * BLOCK_Q + lx, D_KV + LANES), BF16)],
        compiler_params=_params("arbitrary", "arbitrary"),
        name="attention",
    )(q, k, k, k, v, v, v, kx, vx, bias, sink, gn)


def _attn_bias(lx):
    i = np.arange(BLOCK_Q)[:, None]
    j = np.arange(3 * BLOCK_Q)[None, :]
    band = np.abs(i + BLOCK_Q - j) <= WINDOW
    variants = [band & (j >= BLOCK_Q), band, band & (j < 2 * BLOCK_Q)]
    out = np.zeros((3, BLOCK_Q, 3 * BLOCK_Q + lx), np.float32)
    for n, ok in enumerate(variants):
        out[n, :, :3 * BLOCK_Q] = np.where(ok, 0.0, NEG_INF)
    return jnp.asarray(out)


def _outproj_kernel(rn_ref, an_ref, x_ref, wr_ref, wa_ref, g1_ref, gsc_ref, sh_ref, x1_ref, h2_ref):
    o = (jnp.dot(rn_ref[...], wr_ref[...], preferred_element_type=F32)
         + jnp.dot(an_ref[...], wa_ref[...], preferred_element_type=F32))
    x1 = x_ref[...] + g1_ref[...] * o
    x1_ref[...] = x1
    ms = jnp.mean(x1 * x1, axis=-1, keepdims=True)
    h2_ref[...] = (x1 * lax.rsqrt(ms + EPS) * gsc_ref[...] + sh_ref[...]).astype(BF16)


def _out_proj(rn, an, x, wr, wa, g1, gsc, sh, *, tm):
    b, s, d = x.shape
    row = lambda bb, i: (bb, i, 0)
    vec = lambda bb, i: (bb, 0, 0)
    return pl.pallas_call(
        _outproj_kernel,
        grid=(b, s // tm),
        in_specs=[pl.BlockSpec((None, tm, D_RNN), row), pl.BlockSpec((None, tm, D_ATTN), row),
                  pl.BlockSpec((None, tm, d), row), _const_spec(wr.shape), _const_spec(wa.shape),
                  pl.BlockSpec((None, 1, d), vec), pl.BlockSpec((None, 1, d), vec),
                  pl.BlockSpec((None, 1, d), vec)],
        out_specs=[pl.BlockSpec((None, tm, d), row), pl.BlockSpec((None, tm, d), row)],
        out_shape=[jax.ShapeDtypeStruct((b, s, d), F32), jax.ShapeDtypeStruct((b, s, d), BF16)],
        compiler_params=_params("arbitrary", "arbitrary"),
        name="out_proj",
    )(rn, an, x, wr, wa, g1, gsc, sh)


def _ffn_kernel(h_ref, hp_ref, hn_ref, x1_ref, wg_ref, wv_ref, cg_ref, cv_ref, bg_ref, bv_ref,
                wd_ref, g2_ref, fg_ref, o_ref, hext, ug, uv, act, acc, *, nt, tm, nc):
    i = pl.program_id(1)
    halo = SUBLANES_BF16
    zeros = jnp.zeros((halo, hext.shape[1]), BF16)
    hext[0:halo] = jnp.where(i > 0, hp_ref[...], zeros)
    hext[halo:halo + tm] = h_ref[...]
    hext[halo + tm:] = jnp.where(i < nt - 1, hn_ref[...], zeros)
    base = halo - FFN_CONV_LEFT

    def up(j, slot):
        he = hext[...]
        ug[slot] = jnp.dot(he, wg_ref[j], preferred_element_type=F32)
        uv[slot] = jnp.dot(he, wv_ref[j], preferred_element_type=F32)

    def conv(u, slot, cw, cb):
        out = cb
        for t in range(FFN_CONV_W):
            out = out + u[slot, base + t:base + t + tm, :] * cw[t:t + 1]
        return out

    def gate(j, slot):
        half = 0.5 * conv(ug, slot, cg_ref[j], bg_ref[j])
        silu = half * jnp.tanh(half) + half
        act[slot] = (silu * conv(uv, slot, cv_ref[j], bv_ref[j])).astype(BF16)

    def down(j, slot):
        return jnp.dot(act[slot], wd_ref[j], preferred_element_type=F32)

    def pair(p, carry):
        j = 2 * p
        up(j + 1, 1)
        gate(j, 0)
        up(j + 2, 0)
        gate(j + 1, 1)
        acc[...] += down(j, 0) + down(j + 1, 1)
        return carry

    assert nc % 2 == 1
    up(0, 0)
    acc[...] = jnp.zeros_like(acc)
    lax.fori_loop(0, nc // 2, pair, 0)
    gate(nc - 1, 0)
    y = x1_ref[...] + g2_ref[...] * (acc[...] + down(nc - 1, 0))
    ms = jnp.mean(y * y, axis=-1, keepdims=True)
    o_ref[...] = y * lax.rsqrt(ms + EPS) * fg_ref[...]


def _conv_ffn(h2, x1, wg, wv, cg, cv, bg, bv, wd, g2, fg, *, tm):
    b, s, d = x1.shape
    nt = s // tm
    nc = wg.shape[0]
    halo = SUBLANES_BF16
    per = tm // halo
    nblk = s // halo
    row = lambda bb, i: (bb, i, 0)
    prev = lambda bb, i: (bb, jnp.maximum(i * per - 1, 0), 0)
    nxt = lambda bb, i: (bb, jnp.minimum((i + 1) * per, nblk - 1), 0)
    vec = lambda bb, i: (bb, 0, 0)
    return pl.pallas_call(
        functools.partial(_ffn_kernel, nt=nt, tm=tm, nc=nc),
        grid=(b, nt),
        in_specs=[pl.BlockSpec((None, tm, d), row), pl.BlockSpec((None, halo, d), prev),
                  pl.BlockSpec((None, halo, d), nxt), pl.BlockSpec((None, tm, d), row),
                  _const_spec(wg.shape), _const_spec(wv.shape), _const_spec(cg.shape),
                  _const_spec(cv.shape), _const_spec(bg.shape), _const_spec(bv.shape),
                  _const_spec(wd.shape), pl.BlockSpec((None, 1, d), vec), _const_spec(fg.shape)],
        out_specs=pl.BlockSpec((None, tm, d), row),
        out_shape=jax.ShapeDtypeStruct((b, s, d), F32),
        scratch_shapes=[pltpu.VMEM((tm + 2 * halo, d), BF16),
                        pltpu.VMEM((2, tm + 2 * halo, FFN_CHUNK), F32),
                        pltpu.VMEM((2, tm + 2 * halo, FFN_CHUNK), F32),
                        pltpu.VMEM((2, tm, FFN_CHUNK), BF16),
                        pltpu.VMEM((tm, d), F32)],
        compiler_params=_params("arbitrary", "arbitrary"),
        name="conv_ffn",
    )(h2, h2, h2, x1, wg, wv, cg, cv, bg, bv, wd, g2, fg)


def _rope_tables(s):
    half = HEAD_DIM // 4
    inv = ROPE_THETA ** (-jnp.arange(half, dtype=F32) / half)
    t = jnp.arange(s, dtype=jnp.int32)
    ang_r = (t // GRID_W).astype(F32)[:, None] * inv[None, :]
    ang_c = (t % GRID_W).astype(F32)[:, None] * inv[None, :]
    cos = jnp.concatenate([jnp.cos(ang_r)] * 2 + [jnp.cos(ang_c)] * 2, axis=1)
    sin = jnp.concatenate([-jnp.sin(ang_r), jnp.sin(ang_r), -jnp.sin(ang_c), jnp.sin(ang_c)], axis=1)
    reps = LANES // HEAD_DIM
    return jnp.tile(cos, (1, reps)), jnp.tile(sin, (1, reps))


def _gate_weights(w_a, w_i):
    per = MXU_DIM // RNN_BLOCK_W

    def dense(w):
        groups = []
        for g in range(RNN_BLOCKS // per):
            m = jnp.zeros((MXU_DIM, MXU_DIM), F32)
            for n in range(per):
                o = n * RNN_BLOCK_W
                m = m.at[o:o + RNN_BLOCK_W, o:o + RNN_BLOCK_W].set(w[g * per + n])
            groups.append(m)
        return jnp.stack(groups)

    return jnp.concatenate([dense(w_a), dense(w_i)], axis=2).astype(BF16)


def _chunks(w, nc):
    rows = w.shape[0]
    return w.reshape(rows, nc, w.shape[1] // nc).transpose(1, 0, 2)


def kernel(x, c, ctx, c_ctx, w_mod, b_mod, norm1_g, w_in, rnn_conv_w, rnn_conv_b, lru_w_a, lru_b_a,
           lru_w_i, lru_b_i, lru_lam, attn_sink, gn_rnn, gn_attn, w_out, norm2_g, w_up, ffn_conv_w,
           ffn_conv_b, w_down, final_g):
    assert w_mod.shape[0] == 1, "one layer: the last layer's context outputs are never consumed"
    b, s, d = x.shape
    lx = ctx.shape[1]
    d_ff = w_down.shape[1]
    nc = d_ff // FFN_CHUNK

    cond = jnp.concatenate([c, c_ctx[None], jnp.zeros((SUBLANES_F32 - b - 1, d), F32)], axis=0)
    mod = _modulation(cond, w_mod[0], b_mod[0])
    sh1, sc1, g1, sh2, sc2, g2 = [mod[:b, None, d * n:d * (n + 1)] for n in range(6)]
    csh1 = jnp.broadcast_to(mod[b:b + 1, None, :d], (b, 1, d))
    csc1 = jnp.broadcast_to(mod[b:b + 1, None, d:2 * d], (b, 1, d))
    gsc1 = norm1_g[0] * (1.0 + sc1)
    cgsc1 = norm1_g[0] * (1.0 + csc1)
    gsc2 = norm2_g[0] * (1.0 + sc2)

    wi = w_in[0]
    q0 = 2 * D_RNN
    wi = jnp.concatenate([wi[:, :q0], wi[:, q0:q0 + D_ATTN][:, _HEAD_PERM], wi[:, q0 + D_ATTN:]],
                         axis=1).astype(BF16)
    wo = w_out[0]
    wo_r = wo[:D_RNN].astype(BF16)
    wo_a = wo[D_RNN:][_HEAD_PERM].astype(BF16)
    gn_a = gn_attn[0][_HEAD_PERM][None]
    gn_r = gn_rnn[0][None]
    cw = rnn_conv_w[0]
    cb = rnn_conv_b[0][None]
    sp = jax.nn.softplus(-lru_lam[0])

    cos, sin = _rope_tables(s)
    xr, gx, q, k, v = _in_proj(x, gsc1, sh1, wi, cos, sin, rope=True, tm=TM_PROJ)
    xrc, _, _, kx, vx = _in_proj(ctx, cgsc1, csh1, wi, cos[:lx], sin[:lx], rope=False, tm=lx)

    h0 = jnp.zeros((b, 1, D_RNN), F32)
    y_dir = None
    for dr in range(2):
        wg = _gate_weights(lru_w_a[0, dr], lru_w_i[0, dr])
        gate_args = (cw, cb, wg, lru_b_a[0, dr][None], lru_b_i[0, dr][None], sp[dr][None])
        rev = dr == 1
        _, h_ctx = _rglru(xrc, *gate_args, h0, reverse=rev, tm=lx)
        fin = (y_dir, gx, gn_r) if rev else None
        y_dir, _ = _rglru(xr, *gate_args, h_ctx, reverse=rev, tm=TM_SCAN, final_args=fin)
    rn = y_dir

    an = _attention(q, k, v, kx, vx, _attn_bias(lx), attn_sink[0] * LOG2E, gn_a)

    x1, h2 = _out_proj(rn, an, x, wo_r, wo_a, g1, gsc2, sh2, tm=TM_PROJ)

    wu = w_up[0].astype(BF16)
    wg_f = _chunks(wu[:, :d_ff], nc)
    wv_f = _chunks(wu[:, d_ff:], nc)
    fcw = ffn_conv_w[0]
    fcb = ffn_conv_b[0][None]
    cg_f, cv_f = _chunks(fcw[:, :d_ff], nc), _chunks(fcw[:, d_ff:], nc)
    bg_f, bv_f = _chunks(fcb[:, :d_ff], nc), _chunks(fcb[:, d_ff:], nc)
    wd_f = w_down[0].astype(BF16).reshape(nc, FFN_CHUNK, d)
    return _conv_ffn(h2, x1, wg_f, wv_f, cg_f, cv_f, bg_f, bv_f, wd_f, g2, final_g[None], tm=TM_FFN)
```

```python
import functools

import jax
import jax.numpy as jnp
import numpy as np
from jax import lax
from jax.experimental import pallas as pl
from jax.experimental.pallas import tpu as pltpu

F32 = jnp.float32
BF16 = jnp.bfloat16

EPS = 1e-6
GRID_W = 64
D_RNN = 512
RNN_BLOCKS = 8
RNN_BLOCK_W = D_RNN // RNN_BLOCKS
RNN_CONV_W = 4
RNN_CONV_LEFT = 2
LRU_C = 8.0
HEAD_DIM = 64
N_Q_HEADS = 8
N_KV_HEADS = 2
Q_PER_KV = N_Q_HEADS // N_KV_HEADS
D_ATTN = N_Q_HEADS * HEAD_DIM
D_KV = N_KV_HEADS * HEAD_DIM
WINDOW = 128
BLOCK_Q = 128
ROPE_THETA = 10000.0
NEG_INF = -1e30
LOG2E = 1.4426950408889634
F32_TINY = 2.0 ** -126
FFN_CONV_W = 3
FFN_CONV_LEFT = 1

LANES = 128
SUBLANES_F32 = 8
SUBLANES_BF16 = 16
MXU_DIM = 256
VMEM_LIMIT = 56 * 1024 * 1024

TM_PROJ = 512
TM_SCAN = 1024
TM_FFN = 512
FFN_CHUNK = 256

_HEAD_ORDER = (0, 4, 1, 5, 2, 6, 3, 7)
_HEAD_PERM = np.concatenate([np.arange(HEAD_DIM) + HEAD_DIM * h for h in _HEAD_ORDER])


def _params(*sem):
    return pltpu.CompilerParams(dimension_semantics=sem, vmem_limit_bytes=VMEM_LIMIT)


def _sigmoid(x):
    return 0.5 * jnp.tanh(0.5 * x) + 0.5


def _const_spec(shape):
    zeros = (0,) * len(shape)
    return pl.BlockSpec(shape, lambda *_: zeros, pipeline_mode=pl.Buffered(1))


def _mod_kernel(cond_ref, w_ref, b_ref, o_ref):
    s = cond_ref[...]
    s = s * jax.nn.sigmoid(s)
    o_ref[...] = jnp.dot(s, w_ref[...], preferred_element_type=F32,
                         precision=lax.Precision.HIGHEST) + b_ref[...]


def _modulation(cond, w_mod, b_mod):
    rows, d = cond.shape
    n = w_mod.shape[1]
    tn = 768
    return pl.pallas_call(
        _mod_kernel,
        grid=(n // tn,),
        in_specs=[pl.BlockSpec((rows, d), lambda j: (0, 0)),
                  pl.BlockSpec((d, tn), lambda j: (0, j)),
                  pl.BlockSpec((1, tn), lambda j: (0, j))],
        out_specs=pl.BlockSpec((rows, tn), lambda j: (0, j)),
        out_shape=jax.ShapeDtypeStruct((rows, n), F32),
        compiler_params=_params("arbitrary"),
        name="modulation",
    )(cond, w_mod, b_mod.reshape(1, n))


def _rope_partner(t):
    lane = lax.broadcasted_iota(jnp.int32, t.shape, 1)
    first = (lane % 32) < 16
    return jnp.where(first, pltpu.roll(t, LANES - 16, 1), pltpu.roll(t, 16, 1))


def _inproj_kernel(x_ref, gsc_ref, sh_ref, w_ref, cos_ref, sin_ref,
                   xr_ref, gx_ref, q_ref, k_ref, v_ref, *, rope):
    x = x_ref[...]
    ms = jnp.mean(x * x, axis=-1, keepdims=True)
    h = x * lax.rsqrt(ms + EPS) * gsc_ref[...] + sh_ref[...]
    p = jnp.dot(h.astype(BF16), w_ref[...], preferred_element_type=F32)
    xr_ref[...] = p[:, :D_RNN]
    gx_ref[...] = jax.nn.gelu(p[:, D_RNN:2 * D_RNN])
    q0 = 2 * D_RNN
    k0 = q0 + D_ATTN
    v0 = k0 + D_KV
    cols = [p[:, q0 + LANES * c:q0 + LANES * (c + 1)] for c in range(D_ATTN // LANES)]
    k = p[:, k0:v0]
    if rope:
        cos = cos_ref[...]
        sin = sin_ref[...]
        cols = [t * cos + _rope_partner(t) * sin for t in cols]
        k = k * cos + _rope_partner(k) * sin
    scale = HEAD_DIM ** -0.5 * LOG2E
    for c, t in enumerate(cols):
        q_ref[:, LANES * c:LANES * (c + 1)] = (t * scale).astype(BF16)
    k_ref[...] = k.astype(BF16)
    v_ref[...] = p[:, v0:v0 + D_KV].astype(BF16)


def _in_proj(x, gsc, sh, w_in, cos, sin, *, rope, tm):
    b, s, d = x.shape
    n = w_in.shape[1]
    nt = s // tm
    row = lambda bb, i: (bb, i, 0)
    vec = lambda bb, i: (bb, 0, 0)
    tab = lambda bb, i: (i, 0)
    outs = [(D_RNN, F32), (D_RNN, F32), (D_ATTN, BF16), (D_KV, BF16), (D_KV, BF16)]
    return pl.pallas_call(
        functools.partial(_inproj_kernel, rope=rope),
        grid=(b, nt),
        in_specs=[pl.BlockSpec((None, tm, d), row),
                  pl.BlockSpec((None, 1, d), vec),
                  pl.BlockSpec((None, 1, d), vec),
                  _const_spec((d, n)),
                  pl.BlockSpec((tm, LANES), tab),
                  pl.BlockSpec((tm, LANES), tab)],
        out_specs=[pl.BlockSpec((None, tm, w), row) for w, _ in outs],
        out_shape=[jax.ShapeDtypeStruct((b, s, w), dt) for w, dt in outs],
        compiler_params=_params("arbitrary", "arbitrary"),
        name="in_proj_rope" if rope else "in_proj_ctx",
    )(x, gsc, sh, w_in, cos, sin)


def _rglru_kernel(*refs, reverse, final, nt, tm):
    if final:
        (xr_ref, xp_ref, xn_ref, cw_ref, cb_ref, wg_ref, ba_ref, bi_ref, sp_ref, h0_ref,
         yo_ref, gx_ref, gn_ref, y_ref, hfin_ref, xe_scr, a_scr, u_scr, y_scr, h_scr) = refs
    else:
        (xr_ref, xp_ref, xn_ref, cw_ref, cb_ref, wg_ref, ba_ref, bi_ref, sp_ref, h0_ref,
         y_ref, hfin_ref, xe_scr, a_scr, u_scr, y_scr, h_scr) = refs
    i = pl.program_id(0)
    ti = (nt - 1 - i) if reverse else i
    halo = SUBLANES_F32
    nb = xr_ref.shape[0]
    nslab = D_RNN // LANES

    @pl.when(i == 0)
    def _():
        h_scr[...] = h0_ref[...]

    half = tm // 2
    base = halo - RNN_CONV_LEFT
    cw = cw_ref[...]
    cb = cb_ref[...]
    for bb in range(nb):
        cols = []
        for s in range(nslab):
            sl = slice(LANES * s, LANES * (s + 1))
            xs = xe_scr.at[bb * nslab + s]
            xs[0:halo, :] = jnp.where(ti > 0, xp_ref[bb, :, sl], 0.0)
            xs[halo:halo + tm, :] = xr_ref[bb, :, sl]
            xs[halo + tm:, :] = jnp.where(ti < nt - 1, xn_ref[bb, :, sl], 0.0)
            taps = [xs[pl.ds(base + j, half, stride=2), :] for j in range(RNN_CONV_W + 1)]
            even = cb[:, sl]
            odd = cb[:, sl]
            for j in range(RNN_CONV_W):
                even = even + taps[j] * cw[j:j + 1, sl]
                odd = odd + taps[j + 1] * cw[j:j + 1, sl]
            cols.append(jnp.concatenate([even, odd], axis=0))
        xc = jnp.concatenate(cols, axis=1)
        xb = xc.astype(BF16)

        for g in range(D_RNN // MXU_DIM):
            sl = slice(MXU_DIM * g, MXU_DIM * (g + 1))
            z = jnp.dot(xb[:, sl], wg_ref[g], preferred_element_type=F32)
            r = _sigmoid(z[:, :MXU_DIM] + ba_ref[:, sl])
            gi = _sigmoid(z[:, MXU_DIM:] + bi_ref[:, sl])
            log_a = -LRU_C * r * sp_ref[:, sl]
            a = jnp.exp(log_a)
            a_scr[bb, :, sl] = a
            w = -jnp.tanh(log_a) * (1.0 + a * a)
            u_scr[bb, :, sl] = w * lax.rsqrt(jnp.maximum(w, F32_TINY)) * (gi * xc[:, sl])

    rows = SUBLANES_F32
    ngrp = half // rows
    order = range(rows - 1, -1, -1) if reverse else range(rows)
    parity = (1, 0) if reverse else (0, 1)

    def group(g, hs):
        gg = (ngrp - 1 - g) if reverse else g
        lo = pl.multiple_of(gg * rows, rows)
        hi = pl.multiple_of(half + gg * rows, rows)
        out = pl.multiple_of(gg * 2 * rows, 2 * rows)
        hs = list(hs)
        for j in order:
            for par in parity:
                start = hi if par else lo
                for bb in range(nb):
                    av = a_scr.at[bb, pl.ds(start, rows), :]
                    uv = u_scr.at[bb, pl.ds(start, rows), :]
                    yv = y_scr.at[bb, pl.ds(out, 2 * rows), :]
                    hs[bb] = av[j:j + 1, :] * hs[bb] + uv[j:j + 1, :]
                    yv[2 * j + par:2 * j + par + 1, :] = hs[bb]
        return tuple(hs)

    hs = lax.fori_loop(0, ngrp, group, tuple(h_scr[bb] for bb in range(nb)))
    for bb in range(nb):
        h_scr[bb] = hs[bb]
        hfin_ref[bb] = hs[bb]

    if final:
        z = gx_ref[...] * (y_scr[...] + yo_ref[...])
        ms = jnp.mean(z * z, axis=-1, keepdims=True)
        y_ref[...] = (z * lax.rsqrt(ms + EPS) * gn_ref[...]).astype(BF16)
    else:
        y_ref[...] = y_scr[...]


def _rglru(xr, cw, cb, wg, ba, bi, sp, h0, *, reverse, tm, final_args=None):
    b, s, d = xr.shape
    nt = s // tm
    final = final_args is not None
    nblk = s // SUBLANES_F32
    per = tm // SUBLANES_F32
    tile = (lambda i: nt - 1 - i) if reverse else (lambda i: i)
    row = lambda i: (0, tile(i), 0)
    prev = lambda i: (0, jnp.maximum(tile(i) * per - 1, 0), 0)
    nxt = lambda i: (0, jnp.minimum((tile(i) + 1) * per, nblk - 1), 0)
    in_specs = [pl.BlockSpec((b, tm, d), row),
                pl.BlockSpec((b, SUBLANES_F32, d), prev),
                pl.BlockSpec((b, SUBLANES_F32, d), nxt),
                _const_spec(cw.shape), _const_spec(cb.shape), _const_spec(wg.shape),
                _const_spec(ba.shape), _const_spec(bi.shape), _const_spec(sp.shape),
                _const_spec(h0.shape)]
    args = [xr, xr, xr, cw, cb, wg, ba, bi, sp, h0]
    if final:
        y_other, gx, gn = final_args
        in_specs += [pl.BlockSpec((b, tm, d), row), pl.BlockSpec((b, tm, d), row),
                     _const_spec(gn.shape)]
        args += [y_other, gx, gn]
    return pl.pallas_call(
        functools.partial(_rglru_kernel, reverse=reverse, final=final, nt=nt, tm=tm),
        grid=(nt,),
        in_specs=in_specs,
        out_specs=[pl.BlockSpec((b, tm, d), row), pl.BlockSpec((b, 1, d), lambda i: (0, 0, 0))],
        out_shape=[jax.ShapeDtypeStruct((b, s, d), BF16 if final else F32),
                   jax.ShapeDtypeStruct((b, 1, d), F32)],
        scratch_shapes=[pltpu.VMEM((b * d // LANES, tm + 2 * SUBLANES_F32, LANES), F32),
                        pltpu.VMEM((b, tm, d), F32), pltpu.VMEM((b, tm, d), F32),
                        pltpu.VMEM((b, tm, d), F32), pltpu.VMEM((b, 1, d), F32)],
        compiler_params=_params("arbitrary"),
        name="rglru_" + ("bwd" if reverse else "fwd") + ("_final" if final else ""),
    )(*args)


def _attn_kernel(q_ref, kp_ref, kc_ref, kn_ref, vp_ref, vc_ref, vn_ref, kx_ref, vx_ref,
                 bias_ref, sink_ref, gn_ref, o_ref, keys, vals):
    ncol = D_ATTN // LANES
    nloc = 3 * BLOCK_Q
    nk = keys.shape[0]
    for n, (k_ref, v_ref) in enumerate(((kp_ref, vp_ref), (kc_ref, vc_ref), (kn_ref, vn_ref))):
        keys[BLOCK_Q * n:BLOCK_Q * (n + 1), :] = k_ref[...]
        vals[BLOCK_Q * n:BLOCK_Q * (n + 1), :D_KV] = v_ref[...]
    keys[nloc:, :] = kx_ref[...]
    vals[nloc:, :D_KV] = vx_ref[...]
    vals[:, D_KV:] = jnp.ones((nk, LANES), BF16)

    low = lax.broadcasted_iota(jnp.int32, (BLOCK_Q, LANES), 1) < HEAD_DIM
    first = lax.broadcasted_iota(jnp.int32, (2 * BLOCK_Q, 1), 0) < BLOCK_Q
    zero = jnp.zeros((BLOCK_Q, LANES), BF16)
    bias = bias_ref[...]
    outs = []
    for c in range(ncol):
        t = q_ref[:, LANES * c:LANES * (c + 1)]
        qs = jnp.concatenate([jnp.where(low, t, zero), jnp.where(low, zero, t)], axis=0)
        s = lax.dot_general(qs, keys[...], (((1,), (1,)), ((), ())), preferred_element_type=F32)
        s = (s.reshape(2, BLOCK_Q, nk) + bias[None]).reshape(2 * BLOCK_Q, nk)
        sink = jnp.where(first, sink_ref[c], sink_ref[ncol + c])
        m = jnp.maximum(jnp.max(s, axis=-1, keepdims=True), sink)
        e = jnp.exp2(s - m)
        pv = jnp.dot(e.astype(BF16), vals[...], preferred_element_type=F32)
        o = pv[:, :D_KV] / (pv[:, D_KV:] + jnp.exp2(sink - m))
        outs.append(jnp.where(low, o[:BLOCK_Q], o[BLOCK_Q:]))
    ms = sum(jnp.sum(t * t, axis=-1, keepdims=True) for t in outs) * (1.0 / D_ATTN)
    inv = lax.rsqrt(ms + EPS)
    for c, t in enumerate(outs):
        sl = slice(LANES * c, LANES * (c + 1))
        o_ref[:, sl] = (t * inv * gn_ref[:, sl]).astype(BF16)


def _attention(q, k, v, kx, vx, bias, sink, gn):
    b, s, _ = q.shape
    nb = s // BLOCK_Q
    lx = kx.shape[1]
    cur = lambda bb, n: (bb, n, 0)
    prev = lambda bb, n: (bb, jnp.maximum(n - 1, 0), 0)
    nxt = lambda bb, n: (bb, jnp.minimum(n + 1, nb - 1), 0)
    ctx = lambda bb, n: (bb, 0, 0)
    edge = lambda bb, n: (jnp.where(n == 0, 0, jnp.where(n == nb - 1, 2, 1)), 0, 0)
    kv = lambda im: pl.BlockSpec((None, BLOCK_Q, D_KV), im)
    return pl.pallas_call(
        _attn_kernel,
        grid=(b, nb),
        in_specs=[pl.BlockSpec((None, BLOCK_Q, D_ATTN), cur),
                  kv(prev), kv(cur), kv(nxt), kv(prev), kv(cur), kv(nxt),
                  pl.BlockSpec((None, lx, D_KV), ctx), pl.BlockSpec((None, lx, D_KV), ctx),
                  pl.BlockSpec((None, BLOCK_Q, bias.shape[2]), edge),
                  pl.BlockSpec(memory_space=pltpu.SMEM), _const_spec(gn.shape)],
        out_specs=pl.BlockSpec((None, BLOCK_Q, D_ATTN), cur),
        out_shape=jax.ShapeDtypeStruct((b, s, D_ATTN), BF16),
        scratch_shapes=[pltpu.VMEM((3 * BLOCK_Q + lx, D_KV), BF16),
                        pltpu.VMEM((3 * BLOCK_Q + lx, D_KV + LANES), BF16)],
        compiler_params=_params("arbitrary", "arbitrary"),
        name="attention",
    )(q, k, k, k, v, v, v, kx, vx, bias, sink, gn)


def _attn_bias(lx):
    i = np.arange(BLOCK_Q)[:, None]
    j = np.arange(3 * BLOCK_Q)[None, :]
    band = np.abs(i + BLOCK_Q - j) <= WINDOW
    variants = [band & (j >= BLOCK_Q), band, band & (j < 2 * BLOCK_Q)]
    out = np.zeros((3, BLOCK_Q, 3 * BLOCK_Q + lx), np.float32)
    for n, ok in enumerate(variants):
        out[n, :, :3 * BLOCK_Q] = np.where(ok, 0.0, NEG_INF)
    return jnp.asarray(out)


def _outproj_kernel(rn_ref, an_ref, x_ref, wr_ref, wa_ref, g1_ref, gsc_ref, sh_ref, x1_ref, h2_ref):
    o = (jnp.dot(rn_ref[...], wr_ref[...], preferred_element_type=F32)
         + jnp.dot(an_ref[...], wa_ref[...], preferred_element_type=F32))
    x1 = x_ref[...] + g1_ref[...] * o
    x1_ref[...] = x1
    ms = jnp.mean(x1 * x1, axis=-1, keepdims=True)
    h2_ref[...] = (x1 * lax.rsqrt(ms + EPS) * gsc_ref[...] + sh_ref[...]).astype(BF16)


def _out_proj(rn, an, x, wr, wa, g1, gsc, sh, *, tm):
    b, s, d = x.shape
    row = lambda bb, i: (bb, i, 0)
    vec = lambda bb, i: (bb, 0, 0)
    return pl.pallas_call(
        _outproj_kernel,
        grid=(b, s // tm),
        in_specs=[pl.BlockSpec((None, tm, D_RNN), row), pl.BlockSpec((None, tm, D_ATTN), row),
                  pl.BlockSpec((None, tm, d), row), _const_spec(wr.shape), _const_spec(wa.shape),
                  pl.BlockSpec((None, 1, d), vec), pl.BlockSpec((None, 1, d), vec),
                  pl.BlockSpec((None, 1, d), vec)],
        out_specs=[pl.BlockSpec((None, tm, d), row), pl.BlockSpec((None, tm, d), row)],
        out_shape=[jax.ShapeDtypeStruct((b, s, d), F32), jax.ShapeDtypeStruct((b, s, d), BF16)],
        compiler_params=_params("arbitrary", "arbitrary"),
        name="out_proj",
    )(rn, an, x, wr, wa, g1, gsc, sh)


def _ffn_kernel(h_ref, hp_ref, hn_ref, x1_ref, wg_ref, wv_ref, cg_ref, cv_ref, bg_ref, bv_ref,
                wd_ref, g2_ref, fg_ref, o_ref, hext, ug, uv, act, acc, nat, *, nt, tm, nc):
    i = pl.program_id(1)
    halo = SUBLANES_BF16
    zeros = jnp.zeros((halo, hext.shape[1]), BF16)
    hext[0:halo] = jnp.where(i > 0, hp_ref[...], zeros)
    hext[halo:halo + tm] = h_ref[...]
    hext[halo + tm:] = jnp.where(i < nt - 1, hn_ref[...], zeros)
    base = halo - FFN_CONV_LEFT

    half = tm // 2
    nslab = FFN_CHUNK // LANES

    def up(j, slot):
        he = hext[...]
        g = jnp.dot(he, wg_ref[j], preferred_element_type=F32)
        v = jnp.dot(he, wv_ref[j], preferred_element_type=F32)
        for s in range(nslab):
            ug[slot, s] = g[:, LANES * s:LANES * (s + 1)]
            uv[slot, s] = v[:, LANES * s:LANES * (s + 1)]

    def conv(u, slot, cw, cb):
        cols = []
        for s in range(nslab):
            sl = slice(LANES * s, LANES * (s + 1))
            taps = [u[slot, s, pl.ds(base + t, half, stride=2), :] for t in range(FFN_CONV_W + 1)]
            even = cb[:, sl]
            odd = cb[:, sl]
            for t in range(FFN_CONV_W):
                even = even + taps[t] * cw[t:t + 1, sl]
                odd = odd + taps[t + 1] * cw[t:t + 1, sl]
            cols.append(jnp.concatenate([even, odd], axis=0))
        return jnp.concatenate(cols, axis=1)

    def gate(j, slot):
        half = 0.5 * conv(ug, slot, cg_ref[j], bg_ref[j])
        silu = half * jnp.tanh(half) + half
        act[slot] = (silu * conv(uv, slot, cv_ref[j], bv_ref[j])).astype(BF16)

    def down(j, slot):
        return jnp.dot(act[slot], wd_ref[j], preferred_element_type=F32)

    def pair(p, carry):
        j = 2 * p
        up(j + 1, 1)
        gate(j, 0)
        up(j + 2, 0)
        gate(j + 1, 1)
        acc[...] += down(j, 0) + down(j + 1, 1)
        return carry

    assert nc % 2 == 1
    up(0, 0)
    acc[...] = jnp.zeros_like(acc)
    lax.fori_loop(0, nc // 2, pair, 0)
    gate(nc - 1, 0)
    mixed = acc[...] + down(nc - 1, 0)
    nlane = mixed.shape[1] // LANES
    for s in range(nlane):
        sl = slice(LANES * s, LANES * (s + 1))
        nat[s, pl.ds(0, half, stride=2), :] = mixed[:half, sl]
        nat[s, pl.ds(1, half, stride=2), :] = mixed[half:, sl]
    ffn = jnp.concatenate([nat[s] for s in range(nlane)], axis=1)
    y = x1_ref[...] + g2_ref[...] * ffn
    ms = jnp.mean(y * y, axis=-1, keepdims=True)
    o_ref[...] = y * lax.rsqrt(ms + EPS) * fg_ref[...]


def _conv_ffn(h2, x1, wg, wv, cg, cv, bg, bv, wd, g2, fg, *, tm):
    b, s, d = x1.shape
    nt = s // tm
    nc = wg.shape[0]
    halo = SUBLANES_BF16
    per = tm // halo
    nblk = s // halo
    row = lambda bb, i: (bb, i, 0)
    prev = lambda bb, i: (bb, jnp.maximum(i * per - 1, 0), 0)
    nxt = lambda bb, i: (bb, jnp.minimum((i + 1) * per, nblk - 1), 0)
    vec = lambda bb, i: (bb, 0, 0)
    return pl.pallas_call(
        functools.partial(_ffn_kernel, nt=nt, tm=tm, nc=nc),
        grid=(b, nt),
        in_specs=[pl.BlockSpec((None, tm, d), row), pl.BlockSpec((None, halo, d), prev),
                  pl.BlockSpec((None, halo, d), nxt), pl.BlockSpec((None, tm, d), row),
                  _const_spec(wg.shape), _const_spec(wv.shape), _const_spec(cg.shape),
                  _const_spec(cv.shape), _const_spec(bg.shape), _const_spec(bv.shape),
                  _const_spec(wd.shape), pl.BlockSpec((None, 1, d), vec), _const_spec(fg.shape)],
        out_specs=pl.BlockSpec((None, tm, d), row),
        out_shape=jax.ShapeDtypeStruct((b, s, d), F32),
        scratch_shapes=[pltpu.VMEM((tm + 2 * halo, d), BF16),
                        pltpu.VMEM((2, FFN_CHUNK // LANES, tm + 2 * halo, LANES), F32),
                        pltpu.VMEM((2, FFN_CHUNK // LANES, tm + 2 * halo, LANES), F32),
                        pltpu.VMEM((2, tm, FFN_CHUNK), BF16),
                        pltpu.VMEM((tm, d), F32),
                        pltpu.VMEM((d // LANES, tm, LANES), F32)],
        compiler_params=_params("arbitrary", "arbitrary"),
        name="conv_ffn",
    )(h2, h2, h2, x1, wg, wv, cg, cv, bg, bv, wd, g2, fg)


def _rope_tables(s):
    half = HEAD_DIM // 4
    inv = ROPE_THETA ** (-jnp.arange(half, dtype=F32) / half)
    t = jnp.arange(s, dtype=jnp.int32)
    ang_r = (t // GRID_W).astype(F32)[:, None] * inv[None, :]
    ang_c = (t % GRID_W).astype(F32)[:, None] * inv[None, :]
    cos = jnp.concatenate([jnp.cos(ang_r)] * 2 + [jnp.cos(ang_c)] * 2, axis=1)
    sin = jnp.concatenate([-jnp.sin(ang_r), jnp.sin(ang_r), -jnp.sin(ang_c), jnp.sin(ang_c)], axis=1)
    reps = LANES // HEAD_DIM
    return jnp.tile(cos, (1, reps)), jnp.tile(sin, (1, reps))


def _gate_weights(w_a, w_i):
    per = MXU_DIM // RNN_BLOCK_W

    def dense(w):
        groups = []
        for g in range(RNN_BLOCKS // per):
            m = jnp.zeros((MXU_DIM, MXU_DIM), F32)
            for n in range(per):
                o = n * RNN_BLOCK_W
                m = m.at[o:o + RNN_BLOCK_W, o:o + RNN_BLOCK_W].set(w[g * per + n])
            groups.append(m)
        return jnp.stack(groups)

    return jnp.concatenate([dense(w_a), dense(w_i)], axis=2).astype(BF16)


def _chunks(w, nc):
    rows = w.shape[0]
    return w.reshape(rows, nc, w.shape[1] // nc).transpose(1, 0, 2)


def kernel(x, c, ctx, c_ctx, w_mod, b_mod, norm1_g, w_in, rnn_conv_w, rnn_conv_b, lru_w_a, lru_b_a,
           lru_w_i, lru_b_i, lru_lam, attn_sink, gn_rnn, gn_attn, w_out, norm2_g, w_up, ffn_conv_w,
           ffn_conv_b, w_down, final_g):
    assert w_mod.shape[0] == 1, "one layer: the last layer's context outputs are never consumed"
    b, s, d = x.shape
    lx = ctx.shape[1]
    d_ff = w_down.shape[1]
    nc = d_ff // FFN_CHUNK

    cond = jnp.concatenate([c, c_ctx[None], jnp.zeros((SUBLANES_F32 - b - 1, d), F32)], axis=0)
    mod = _modulation(cond, w_mod[0], b_mod[0])
    sh1, sc1, g1, sh2, sc2, g2 = [mod[:b, None, d * n:d * (n + 1)] for n in range(6)]
    csh1 = jnp.broadcast_to(mod[b:b + 1, None, :d], (b, 1, d))
    csc1 = jnp.broadcast_to(mod[b:b + 1, None, d:2 * d], (b, 1, d))
    gsc1 = norm1_g[0] * (1.0 + sc1)
    cgsc1 = norm1_g[0] * (1.0 + csc1)
    gsc2 = norm2_g[0] * (1.0 + sc2)

    wi = w_in[0]
    q0 = 2 * D_RNN
    wi = jnp.concatenate([wi[:, :q0], wi[:, q0:q0 + D_ATTN][:, _HEAD_PERM], wi[:, q0 + D_ATTN:]],
                         axis=1).astype(BF16)
    wo = w_out[0]
    wo_r = wo[:D_RNN].astype(BF16)
    wo_a = wo[D_RNN:][_HEAD_PERM].astype(BF16)
    gn_a = gn_attn[0][_HEAD_PERM][None]
    gn_r = gn_rnn[0][None]
    cw = rnn_conv_w[0]
    cb = rnn_conv_b[0][None]
    sp = jax.nn.softplus(-lru_lam[0])

    cos, sin = _rope_tables(s)
    xr, gx, q, k, v = _in_proj(x, gsc1, sh1, wi, cos, sin, rope=True, tm=TM_PROJ)
    xrc, _, _, kx, vx = _in_proj(ctx, cgsc1, csh1, wi, cos[:lx], sin[:lx], rope=False, tm=lx)

    h0 = jnp.zeros((b, 1, D_RNN), F32)
    y_dir = None
    for dr in range(2):
        wg = _gate_weights(lru_w_a[0, dr], lru_w_i[0, dr])
        gate_args = (cw, cb, wg, lru_b_a[0, dr][None], lru_b_i[0, dr][None], sp[dr][None])
        rev = dr == 1
        _, h_ctx = _rglru(xrc, *gate_args, h0, reverse=rev, tm=lx)
        fin = (y_dir, gx, gn_r) if rev else None
        y_dir, _ = _rglru(xr, *gate_args, h_ctx, reverse=rev, tm=TM_SCAN, final_args=fin)
    rn = y_dir

    an = _attention(q, k, v, kx, vx, _attn_bias(lx), attn_sink[0] * LOG2E, gn_a)

    x1, h2 = _out_proj(rn, an, x, wo_r, wo_a, g1, gsc2, sh2, tm=TM_PROJ)

    wu = w_up[0].astype(BF16)
    wg_f = _chunks(wu[:, :d_ff], nc)
    wv_f = _chunks(wu[:, d_ff:], nc)
    fcw = ffn_conv_w[0]
    fcb = ffn_conv_b[0][None]
    cg_f, cv_f = _chunks(fcw[:, :d_ff], nc), _chunks(fcw[:, d_ff:], nc)
    bg_f, bv_f = _chunks(fcb[:, :d_ff], nc), _chunks(fcb[:, d_ff:], nc)
    wd_f = w_down[0].astype(BF16).reshape(nc, FFN_CHUNK, d)
    return _conv_ffn(h2, x1, wg_f, wv_f, cg_f, cv_f, bg_f, bv_f, wd_f, g2, final_g[None], tm=TM_FFN)
```

```python
import functools

import jax
import jax.numpy as jnp
import numpy as np
from jax import lax
from jax.experimental import pallas as pl
from jax.experimental.pallas import tpu as pltpu

F32 = jnp.float32
BF16 = jnp.bfloat16

EPS = 1e-6
GRID_W = 64
D_RNN = 512
RNN_BLOCKS = 8
RNN_BLOCK_W = D_RNN // RNN_BLOCKS
RNN_CONV_W = 4
RNN_CONV_LEFT = 2
LRU_C = 8.0
HEAD_DIM = 64
N_Q_HEADS = 8
N_KV_HEADS = 2
Q_PER_KV = N_Q_HEADS // N_KV_HEADS
D_ATTN = N_Q_HEADS * HEAD_DIM
D_KV = N_KV_HEADS * HEAD_DIM
WINDOW = 128
BLOCK_Q = 128
ROPE_THETA = 10000.0
NEG_INF = -1e30
LOG2E = 1.4426950408889634
F32_TINY = 2.0 ** -126
FFN_CONV_W = 3
FFN_CONV_LEFT = 1

LANES = 128
SUBLANES_F32 = 8
SUBLANES_BF16 = 16
MXU_DIM = 256
VMEM_LIMIT = 56 * 1024 * 1024

TM_PROJ = 512
TM_SCAN = 1024
TM_FFN = 512
FFN_CHUNK = 256
ATTN_QB = 2


def _params(*sem):
    return pltpu.CompilerParams(dimension_semantics=sem, vmem_limit_bytes=VMEM_LIMIT)


def _sigmoid(x):
    return 0.5 * jnp.tanh(0.5 * x) + 0.5


def _const_spec(shape):
    zeros = (0,) * len(shape)
    return pl.BlockSpec(shape, lambda *_: zeros, pipeline_mode=pl.Buffered(1))


def _mod_kernel(cond_ref, w_ref, b_ref, o_ref):
    s = cond_ref[...]
    s = s * jax.nn.sigmoid(s)
    o_ref[...] = jnp.dot(s, w_ref[...], preferred_element_type=F32,
                         precision=lax.Precision.HIGHEST) + b_ref[...]


def _modulation(cond, w_mod, b_mod):
    rows, d = cond.shape
    n = w_mod.shape[1]
    tn = 768
    return pl.pallas_call(
        _mod_kernel,
        grid=(n // tn,),
        in_specs=[pl.BlockSpec((rows, d), lambda j: (0, 0)),
                  pl.BlockSpec((d, tn), lambda j: (0, j)),
                  pl.BlockSpec((1, tn), lambda j: (0, j))],
        out_specs=pl.BlockSpec((rows, tn), lambda j: (0, j)),
        out_shape=jax.ShapeDtypeStruct((rows, n), F32),
        compiler_params=_params("arbitrary"),
        name="modulation",
    )(cond, w_mod, b_mod.reshape(1, n))


def _rope_partner(t):
    lane = lax.broadcasted_iota(jnp.int32, t.shape, 1)
    first = (lane % 32) < 16
    return jnp.where(first, pltpu.roll(t, LANES - 16, 1), pltpu.roll(t, 16, 1))


def _inproj_kernel(x_ref, gsc_ref, sh_ref, w_ref, cosr_ref, sinr_ref, cosc_ref, sinc_ref,
                   xr_ref, gx_ref, q_ref, k_ref, v_ref, *, rope):
    x = x_ref[...]
    ms = jnp.mean(x * x, axis=-1, keepdims=True)
    h = x * lax.rsqrt(ms + EPS) * gsc_ref[...] + sh_ref[...]
    p = jnp.dot(h.astype(BF16), w_ref[...], preferred_element_type=F32)
    xr_ref[...] = p[:, :D_RNN]
    gx_ref[...] = jax.nn.gelu(p[:, D_RNN:2 * D_RNN])
    q0 = 2 * D_RNN
    k0 = q0 + D_ATTN
    v0 = k0 + D_KV
    cols = [p[:, q0 + LANES * c:q0 + LANES * (c + 1)] for c in range(D_ATTN // LANES)]
    k = p[:, k0:v0]
    if rope:
        tm = x.shape[0]
        expand = lambda r_ref, c_ref: (r_ref[...][:, None, :] + c_ref[...][None, :, :]).reshape(tm, LANES)
        cos = expand(cosr_ref, cosc_ref)
        sin = expand(sinr_ref, sinc_ref)
        cols =[t * cos + _rope_partner(t) * sin for t in cols]
        k = k * cos + _rope_partner(k) * sin
    scale = HEAD_DIM ** -0.5 * LOG2E
    for c, t in enumerate(cols):
        q_ref[:, LANES * c:LANES * (c + 1)] = (t * scale).astype(BF16)
    k_ref[...] = k.astype(BF16)
    v_ref[...] = p[:, v0:v0 + D_KV].astype(BF16)


def _in_proj(x, gsc, sh, w_in, tables, *, rope, tm):
    b, s, d = x.shape
    n = w_in.shape[1]
    nt = s // tm
    row = lambda bb, i: (bb, i, 0)
    vec = lambda bb, i: (bb, 0, 0)
    cosr, sinr, cosc, sinc = tables
    if rope:
        assert tm % GRID_W == 0
        rtab = pl.BlockSpec((tm // GRID_W, LANES), lambda bb, i: (i, 0))
    else:
        rtab = _const_spec(cosr.shape)
    outs = [(D_RNN, F32), (D_RNN, F32), (D_ATTN, BF16), (D_KV, BF16), (D_KV, BF16)]
    return pl.pallas_call(
        functools.partial(_inproj_kernel, rope=rope),
        grid=(b, nt),
        in_specs=[pl.BlockSpec((None, tm, d), row),
                  pl.BlockSpec((None, 1, d), vec),
                  pl.BlockSpec((None, 1, d), vec),
                  _const_spec((d, n)),
                  rtab, rtab, _const_spec(cosc.shape), _const_spec(sinc.shape)],
        out_specs=[pl.BlockSpec((None, tm, w), row) for w, _ in outs],
        out_shape=[jax.ShapeDtypeStruct((b, s, w), dt) for w, dt in outs],
        compiler_params=_params("arbitrary", "arbitrary"),
        name="in_proj_rope" if rope else "in_proj_ctx",
    )(x, gsc, sh, w_in, cosr, sinr, cosc, sinc)


def _rglru_kernel(*refs, reverse, final, nt, tm):
    if final:
        (xr_ref, xp_ref, xn_ref, cw_ref, cb_ref, wg_ref, ba_ref, bi_ref, sp_ref, h0_ref,
         yo_ref, gx_ref, gn_ref, y_ref, hfin_ref, xe_scr, a_scr, u_scr, y_scr, h_scr) = refs
    else:
        (xr_ref, xp_ref, xn_ref, cw_ref, cb_ref, wg_ref, ba_ref, bi_ref, sp_ref, h0_ref,
         y_ref, hfin_ref, xe_scr, a_scr, u_scr, y_scr, h_scr) = refs
    i = pl.program_id(0)
    ti = (nt - 1 - i) if reverse else i
    halo = SUBLANES_F32
    nb = xr_ref.shape[0]
    nslab = D_RNN // LANES

    @pl.when(i == 0)
    def _():
        h_scr[...] = h0_ref[...]

    half = tm // 2
    base = halo - RNN_CONV_LEFT
    cw = cw_ref[...]
    cb = cb_ref[...]
    for bb in range(nb):
        cols = []
        for s in range(nslab):
            sl = slice(LANES * s, LANES * (s + 1))
            xs = xe_scr.at[bb * nslab + s]
            xs[0:halo, :] = jnp.where(ti > 0, xp_ref[bb, :, sl], 0.0)
            xs[halo:halo + tm, :] = xr_ref[bb, :, sl]
            xs[halo + tm:, :] = jnp.where(ti < nt - 1, xn_ref[bb, :, sl], 0.0)
            taps = [xs[pl.ds(base + j, half, stride=2), :] for j in range(RNN_CONV_W + 1)]
            even = cb[:, sl]
            odd = cb[:, sl]
            for j in range(RNN_CONV_W):
                even = even + taps[j] * cw[j:j + 1, sl]
                odd = odd + taps[j + 1] * cw[j:j + 1, sl]
            cols.append(jnp.concatenate([even, odd], axis=0))
        xc = jnp.concatenate(cols, axis=1)
        xb = xc.astype(BF16)

        for g in range(D_RNN // MXU_DIM):
            sl = slice(MXU_DIM * g, MXU_DIM * (g + 1))
            z = jnp.dot(xb[:, sl], wg_ref[g], preferred_element_type=F32)
            r = _sigmoid(z[:, :MXU_DIM] + ba_ref[:, sl])
            gi = _sigmoid(z[:, MXU_DIM:] + bi_ref[:, sl])
            log_a = -LRU_C * r * sp_ref[:, sl]
            a = jnp.exp(log_a)
            a_scr[bb, :, sl] = a
            w = -jnp.tanh(log_a) * (1.0 + a * a)
            u_scr[bb, :, sl] = w * lax.rsqrt(jnp.maximum(w, F32_TINY)) * (gi * xc[:, sl])

    rows = SUBLANES_F32
    ngrp = half // rows
    order = range(rows - 1, -1, -1) if reverse else range(rows)
    parity = (1, 0) if reverse else (0, 1)

    def group(g, hs):
        gg = (ngrp - 1 - g) if reverse else g
        lo = pl.multiple_of(gg * rows, rows)
        hi = pl.multiple_of(half + gg * rows, rows)
        out = pl.multiple_of(gg * 2 * rows, 2 * rows)
        hs = list(hs)
        for j in order:
            for par in parity:
                start = hi if par else lo
                for bb in range(nb):
                    av = a_scr.at[bb, pl.ds(start, rows), :]
                    uv = u_scr.at[bb, pl.ds(start, rows), :]
                    yv = y_scr.at[bb, pl.ds(out, 2 * rows), :]
                    hs[bb] = av[j:j + 1, :] * hs[bb] + uv[j:j + 1, :]
                    yv[2 * j + par:2 * j + par + 1, :] = hs[bb]
        return tuple(hs)

    hs = lax.fori_loop(0, ngrp, group, tuple(h_scr[bb] for bb in range(nb)))
    for bb in range(nb):
        h_scr[bb] = hs[bb]
        hfin_ref[bb] = hs[bb]

    if final:
        z = gx_ref[...] * (y_scr[...] + yo_ref[...])
        ms = jnp.mean(z * z, axis=-1, keepdims=True)
        y_ref[...] = (z * lax.rsqrt(ms + EPS) * gn_ref[...]).astype(BF16)
    else:
        y_ref[...] = y_scr[...]


def _rglru(xr, cw, cb, wg, ba, bi, sp, h0, *, reverse, tm, final_args=None):
    b, s, d = xr.shape
    nt = s // tm
    final = final_args is not None
    nblk = s // SUBLANES_F32
    per = tm // SUBLANES_F32
    tile = (lambda i: nt - 1 - i) if reverse else (lambda i: i)
    row = lambda i: (0, tile(i), 0)
    prev = lambda i: (0, jnp.maximum(tile(i) * per - 1, 0), 0)
    nxt = lambda i: (0, jnp.minimum((tile(i) + 1) * per, nblk - 1), 0)
    in_specs = [pl.BlockSpec((b, tm, d), row),
                pl.BlockSpec((b, SUBLANES_F32, d), prev),
                pl.BlockSpec((b, SUBLANES_F32, d), nxt),
                _const_spec(cw.shape), _const_spec(cb.shape), _const_spec(wg.shape),
                _const_spec(ba.shape), _const_spec(bi.shape), _const_spec(sp.shape),
                _const_spec(h0.shape)]
    args = [xr, xr, xr, cw, cb, wg, ba, bi, sp, h0]
    if final:
        y_other, gx, gn = final_args
        in_specs += [pl.BlockSpec((b, tm, d), row), pl.BlockSpec((b, tm, d), row),
                     _const_spec(gn.shape)]
        args += [y_other, gx, gn]
    return pl.pallas_call(
        functools.partial(_rglru_kernel, reverse=reverse, final=final, nt=nt, tm=tm),
        grid=(nt,),
        in_specs=in_specs,
        out_specs=[pl.BlockSpec((b, tm, d), row), pl.BlockSpec((b, 1, d), lambda i: (0, 0, 0))],
        out_shape=[jax.ShapeDtypeStruct((b, s, d), BF16 if final else F32),
                   jax.ShapeDtypeStruct((b, 1, d), F32)],
        scratch_shapes=[pltpu.VMEM((b * d // LANES, tm + 2 * SUBLANES_F32, LANES), F32),
                        pltpu.VMEM((b, tm, d), F32), pltpu.VMEM((b, tm, d), F32),
                        pltpu.VMEM((b, tm, d), F32), pltpu.VMEM((b, 1, d), F32)],
        compiler_params=_params("arbitrary"),
        name="rglru_" + ("bwd" if reverse else "fwd") + ("_final" if final else ""),
    )(*args)


def _attn_kernel(q_ref, kp_ref, kc_ref, kn_ref, vp_ref, vc_ref, vn_ref, kx_ref, vx_ref,
                 bfirst_ref, bmid_ref, blast_ref, sink_ref, gn_ref, o_ref, keys, vals, valx, *, qb):
    ncol = D_ATTN // LANES
    nwin = 3 * BLOCK_Q
    mid = slice(BLOCK_Q, BLOCK_Q * (qb + 1))
    keys[:BLOCK_Q, :] = kp_ref[...]
    keys[mid, :] = kc_ref[...]
    keys[BLOCK_Q * (qb + 1):, :] = kn_ref[...]
    vals[:BLOCK_Q, :D_KV] = vp_ref[...]
    vals[mid, :D_KV] = vc_ref[...]
    vals[BLOCK_Q * (qb + 1):, :D_KV] = vn_ref[...]
    valx[:, :D_KV] = vx_ref[...]
    vals[:, D_KV:] = jnp.ones((vals.shape[0], LANES), BF16)
    valx[:, D_KV:] = jnp.ones((valx.shape[0], LANES), BF16)

    low = lax.broadcasted_iota(jnp.int32, (BLOCK_Q, LANES), 1) < HEAD_DIM
    first = lax.broadcasted_iota(jnp.int32, (2 * BLOCK_Q, 1), 0) < BLOCK_Q
    zero = jnp.zeros((BLOCK_Q, LANES), BF16)
    nt_dims = (((1,), (1,)), ((), ()))
    for x in range(qb):
        rows = slice(BLOCK_Q * x, BLOCK_Q * (x + 1))
        win = slice(BLOCK_Q * x, BLOCK_Q * x + nwin)
        bias = (bfirst_ref if x == 0 else blast_ref if x == qb - 1 else bmid_ref)[...]
        outs = []
        for c in range(ncol):
            t = q_ref[rows, LANES * c:LANES * (c + 1)]
            qs = jnp.concatenate([jnp.where(low, t, zero), jnp.where(low, zero, t)], axis=0)
            s_loc = lax.dot_general(qs, keys[win, :], nt_dims, preferred_element_type=F32)
            s_ctx = lax.dot_general(qs, kx_ref[...], nt_dims, preferred_element_type=F32)
            s_loc = (s_loc.reshape(2, BLOCK_Q, nwin) + bias[None]).reshape(2 * BLOCK_Q, nwin)
            sink = jnp.where(first, sink_ref[c], sink_ref[ncol + c])
            m = jnp.maximum(jnp.maximum(jnp.max(s_loc, axis=-1, keepdims=True),
                                        jnp.max(s_ctx, axis=-1, keepdims=True)), sink)
            e_loc = jnp.exp2(s_loc - m).astype(BF16)
            e_ctx = jnp.exp2(s_ctx - m).astype(BF16)
            pv = (jnp.dot(e_loc, vals[win, :], preferred_element_type=F32)
                  + jnp.dot(e_ctx, valx[...], preferred_element_type=F32))
            o = pv[:, :D_KV] / (pv[:, D_KV:] + jnp.exp2(sink - m))
            outs.append(jnp.where(low, o[:BLOCK_Q], o[BLOCK_Q:]))
        ms = sum(jnp.sum(t * t, axis=-1, keepdims=True) for t in outs) * (1.0 / D_ATTN)
        inv = lax.rsqrt(ms + EPS)
        for c, t in enumerate(outs):
            sl = slice(LANES * c, LANES * (c + 1))
            o_ref[rows, sl] = (t * inv * gn_ref[:, sl]).astype(BF16)


def _attention(q, k, v, kx, vx, bias, sink, gn):
    b, s, _ = q.shape
    qb = ATTN_QB
    nblk = s // BLOCK_Q
    nstep = nblk // qb
    lx = kx.shape[1]
    cur = lambda bb, n: (bb, n, 0)
    prev = lambda bb, n: (bb, jnp.maximum(n * qb - 1, 0), 0)
    nxt = lambda bb, n: (bb, jnp.minimum((n + 1) * qb, nblk - 1), 0)
    ctx = lambda bb, n: (bb, 0, 0)
    first = lambda bb, n: (jnp.where(n == 0, 0, 1), 0, 0)
    last = lambda bb, n: (jnp.where(n == nstep - 1, 2, 1), 0, 0)
    edge = lambda im: pl.BlockSpec((None, BLOCK_Q, 3 * BLOCK_Q), im)
    halo = lambda im: pl.BlockSpec((None, BLOCK_Q, D_KV), im)
    body = pl.BlockSpec((None, qb * BLOCK_Q, D_KV), cur)
    side = pl.BlockSpec((None, lx, D_KV), ctx)
    return pl.pallas_call(
        functools.partial(_attn_kernel, qb=qb),
        grid=(b, nstep),
        in_specs=[pl.BlockSpec((None, qb * BLOCK_Q, D_ATTN), cur),
                  halo(prev), body, halo(nxt), halo(prev), body, halo(nxt), side, side,
                  edge(first), edge(lambda bb, n: (1, 0, 0)), edge(last),
                  pl.BlockSpec(memory_space=pltpu.SMEM), _const_spec(gn.shape)],
        out_specs=pl.BlockSpec((None, qb * BLOCK_Q, D_ATTN), cur),
        out_shape=jax.ShapeDtypeStruct((b, s, D_ATTN), BF16),
        scratch_shapes=[pltpu.VMEM(((qb + 2) * BLOCK_Q, D_KV), BF16),
                        pltpu.VMEM(((qb + 2) * BLOCK_Q, D_KV + LANES), BF16),
                        pltpu.VMEM((lx, D_KV + LANES), BF16)],
        compiler_params=_params("arbitrary", "arbitrary"),
        name="attention",
    )(q, k, k, k, v, v, v, kx, vx, bias, bias, bias, sink, gn)


def _attn_bias():
    i = np.arange(BLOCK_Q)[:, None]
    j = np.arange(3 * BLOCK_Q)[None, :]
    band = np.abs(i + BLOCK_Q - j) <= WINDOW
    variants = [band & (j >= BLOCK_Q), band, band & (j < 2 * BLOCK_Q)]
    return jnp.asarray(np.stack([np.where(ok, 0.0, NEG_INF) for ok in variants]).astype(np.float32))


def _outproj_kernel(rn_ref, an_ref, x_ref, wr_ref, wa_ref, g1_ref, gsc_ref, sh_ref, x1_ref, h2_ref):
    o = (jnp.dot(rn_ref[...], wr_ref[...], preferred_element_type=F32)
         + jnp.dot(an_ref[...], wa_ref[...], preferred_element_type=F32))
    x1 = x_ref[...] + g1_ref[...] * o
    x1_ref[...] = x1
    ms = jnp.mean(x1 * x1, axis=-1, keepdims=True)
    h2_ref[...] = (x1 * lax.rsqrt(ms + EPS) * gsc_ref[...] + sh_ref[...]).astype(BF16)


def _out_proj(rn, an, x, wr, wa, g1, gsc, sh, *, tm):
    b, s, d = x.shape
    row = lambda bb, i: (bb, i, 0)
    vec = lambda bb, i: (bb, 0, 0)
    return pl.pallas_call(
        _outproj_kernel,
        grid=(b, s // tm),
        in_specs=[pl.BlockSpec((None, tm, D_RNN), row), pl.BlockSpec((None, tm, D_ATTN), row),
                  pl.BlockSpec((None, tm, d), row), _const_spec(wr.shape), _const_spec(wa.shape),
                  pl.BlockSpec((None, 1, d), vec), pl.BlockSpec((None, 1, d), vec),
                  pl.BlockSpec((None, 1, d), vec)],
        out_specs=[pl.BlockSpec((None, tm, d), row), pl.BlockSpec((None, tm, d), row)],
        out_shape=[jax.ShapeDtypeStruct((b, s, d), F32), jax.ShapeDtypeStruct((b, s, d), BF16)],
        compiler_params=_params("arbitrary", "arbitrary"),
        name="out_proj",
    )(rn, an, x, wr, wa, g1, gsc, sh)


def _ffn_kernel(h_ref, hp_ref, hn_ref, x1_ref, wu_ref, cw_ref, cb_ref, wd_ref, g2_ref, fg_ref,
                o_ref, hext, ug, uv, act, acc, nat, *, nt, tm):
    d_ff = wd_ref.shape[0]
    nc = d_ff // FFN_CHUNK
    span = lambda j, branch: slice(branch * d_ff + j * FFN_CHUNK, branch * d_ff + (j + 1) * FFN_CHUNK)
    i = pl.program_id(1)
    halo = SUBLANES_BF16
    zeros = jnp.zeros((halo, hext.shape[1]), BF16)
    hext[0:halo] = jnp.where(i > 0, hp_ref[...], zeros)
    hext[halo:halo + tm] = h_ref[...]
    hext[halo + tm:] = jnp.where(i < nt - 1, hn_ref[...], zeros)
    base = halo - FFN_CONV_LEFT

    half = tm // 2
    nslab = FFN_CHUNK // LANES

    def up(j, slot):
        he = hext[...]
        g = jnp.dot(he, wu_ref[:, span(j,0)], preferred_element_type=F32)
        v = jnp.dot(he, wu_ref[:, span(j,1)], preferred_element_type=F32)
        for s in range(nslab):
            ug[slot, s] = g[:, LANES * s:LANES * (s + 1)]
            uv[slot, s] = v[:, LANES * s:LANES * (s + 1)]

    def conv(u, slot, cw, cb):
        cols = []
        for s in range(nslab):
            sl = slice(LANES * s, LANES * (s + 1))
            taps = [u[slot, s, pl.ds(base + t, half, stride=2), :] for t in range(FFN_CONV_W + 1)]
            even = cb[:, sl]
            odd = cb[:, sl]
            for t in range(FFN_CONV_W):
                even = even + taps[t] * cw[t:t + 1, sl]
                odd = odd + taps[t + 1] * cw[t:t + 1, sl]
            cols.append(jnp.concatenate([even, odd], axis=0))
        return jnp.concatenate(cols, axis=1)

    def gate(j, slot):
        hg = 0.5 * conv(ug, slot, cw_ref[:, span(j,0)], cb_ref[:, span(j,0)])
        silu = hg * jnp.tanh(hg) + hg
        val = conv(uv, slot, cw_ref[:, span(j,1)], cb_ref[:, span(j,1)])
        act[slot] = (silu * val).astype(BF16)

    def down(j, slot):
        return jnp.dot(act[slot], wd_ref[span(j,0), :], preferred_element_type=F32)

    assert nc % 2 == 1
    up(0, 0)
    acc[...] = jnp.zeros_like(acc)
    for j in range(0, nc - 1, 2):
        up(j + 1, 1)
        gate(j, 0)
        up(j + 2, 0)
        gate(j + 1, 1)
        acc[...] += down(j, 0) + down(j + 1, 1)
    gate(nc - 1, 0)
    mixed = acc[...] + down(nc - 1, 0)
    nlane = mixed.shape[1] // LANES
    for s in range(nlane):
        sl = slice(LANES * s, LANES * (s + 1))
        nat[s, pl.ds(0, half, stride=2), :] = mixed[:half, sl]
        nat[s, pl.ds(1, half, stride=2), :] = mixed[half:, sl]
    ffn = jnp.concatenate([nat[s] for s in range(nlane)], axis=1)
    y = x1_ref[...] + g2_ref[...] * ffn
    ms = jnp.mean(y * y, axis=-1, keepdims=True)
    o_ref[...] = y * lax.rsqrt(ms + EPS) * fg_ref[...]


def _conv_ffn(h2, x1, wu, cw, cb, wd, g2, fg, *, tm):
    b, s, d = x1.shape
    nt = s // tm
    assert wd.shape[0] % FFN_CHUNK == 0 and (wd.shape[0] // FFN_CHUNK) % 2 == 1
    halo = SUBLANES_BF16
    per = tm // halo
    nblk = s // halo
    row = lambda bb, i: (bb, i, 0)
    prev = lambda bb, i: (bb, jnp.maximum(i * per - 1, 0), 0)
    nxt = lambda bb, i: (bb, jnp.minimum((i + 1) * per, nblk - 1), 0)
    vec = lambda bb, i: (bb, 0, 0)
    return pl.pallas_call(
        functools.partial(_ffn_kernel, nt=nt, tm=tm),
        grid=(b, nt),
        in_specs=[pl.BlockSpec((None, tm, d), row), pl.BlockSpec((None, halo, d), prev),
                  pl.BlockSpec((None, halo, d), nxt), pl.BlockSpec((None, tm, d), row),
                  _const_spec(wu.shape), _const_spec(cw.shape), _const_spec(cb.shape),
                  _const_spec(wd.shape), pl.BlockSpec((None, 1, d), vec), _const_spec(fg.shape)],
        out_specs=pl.BlockSpec((None, tm, d), row),
        out_shape=jax.ShapeDtypeStruct((b, s, d), F32),
        scratch_shapes=[pltpu.VMEM((tm + 2 * halo, d), BF16),
                        pltpu.VMEM((2, FFN_CHUNK // LANES, tm + 2 * halo, LANES), F32),
                        pltpu.VMEM((2, FFN_CHUNK // LANES, tm + 2 * halo, LANES), F32),
                        pltpu.VMEM((2, tm, FFN_CHUNK), BF16),
                        pltpu.VMEM((tm, d), F32),
                        pltpu.VMEM((d // LANES, tm, LANES), F32)],
        compiler_params=_params("arbitrary", "arbitrary"),
        name="conv_ffn",
    )(h2, h2, h2, x1, wu, cw, cb, wd, g2, fg)


def _rope_tables(s):
    half = HEAD_DIM // 4
    inv = ROPE_THETA ** (-jnp.arange(half, dtype=F32) / half)

    def tables(npos, lanes_first):
        ang = jnp.arange(npos, dtype=jnp.int32).astype(F32)[:, None] * inv[None, :]
        zero = jnp.zeros((npos, 2 * half), F32)
        cos = jnp.concatenate([jnp.cos(ang)] * 2, axis=1)
        sin = jnp.concatenate([-jnp.sin(ang), jnp.sin(ang)], axis=1)
        order = (lambda t: [t, zero]) if lanes_first else (lambda t: [zero, t])
        reps = LANES // HEAD_DIM
        return [jnp.tile(jnp.concatenate(order(t), axis=1), (1, reps)) for t in (cos, sin)]

    cosr, sinr = tables(s // GRID_W, True)
    cosc, sinc = tables(GRID_W, False)
    return cosr, sinr, cosc, sinc


def _gate_weights(w_a, w_i):
    per = MXU_DIM // RNN_BLOCK_W
    eye = jnp.eye(per, dtype=F32)

    def dense(w):
        w = w.reshape(RNN_BLOCKS // per, per, RNN_BLOCK_W, RNN_BLOCK_W)
        m = w[:, :, :, None, :] * eye[None, :, None, :, None]
        return m.reshape(RNN_BLOCKS // per, MXU_DIM, MXU_DIM)

    return jnp.concatenate([dense(w_a), dense(w_i)], axis=2).astype(BF16)


def _interleave_heads(w, axis):
    shape = w.shape
    w = w.reshape(shape[:axis] + (N_KV_HEADS, Q_PER_KV, HEAD_DIM) + shape[axis + 1:])
    return jnp.swapaxes(w, axis, axis + 1).reshape(shape)


def kernel(x, c, ctx, c_ctx, w_mod, b_mod, norm1_g, w_in, rnn_conv_w, rnn_conv_b, lru_w_a, lru_b_a,
           lru_w_i, lru_b_i, lru_lam, attn_sink, gn_rnn, gn_attn, w_out, norm2_g, w_up, ffn_conv_w,
           ffn_conv_b, w_down, final_g):
    assert w_mod.shape[0] == 1, "one layer: the last layer's context outputs are never consumed"
    b, s, d = x.shape
    lx = ctx.shape[1]

    cond = jnp.concatenate([c, c_ctx[None], jnp.zeros((SUBLANES_F32 - b - 1, d), F32)], axis=0)
    mod = _modulation(cond, w_mod[0], b_mod[0])
    sh1, sc1, g1, sh2, sc2, g2 = [mod[:b, None, d * n:d * (n + 1)] for n in range(6)]
    csh1 = jnp.broadcast_to(mod[b:b + 1, None, :d], (b, 1, d))
    csc1 = jnp.broadcast_to(mod[b:b + 1, None, d:2 * d], (b, 1, d))
    gsc1 = norm1_g[0] * (1.0 + sc1)
    cgsc1 = norm1_g[0] * (1.0 + csc1)
    gsc2 = norm2_g[0] * (1.0 + sc2)

    wi = w_in[0].astype(BF16)
    q0 = 2 * D_RNN
    wi = jnp.concatenate([wi[:, :q0], _interleave_heads(wi[:, q0:q0 + D_ATTN], 1),
                          wi[:, q0 + D_ATTN:]], axis=1)
    wo = w_out[0].astype(BF16)
    wo_r = wo[:D_RNN]
    wo_a = _interleave_heads(wo[D_RNN:], 0)
    gn_a = _interleave_heads(gn_attn[0], 0)[None]
    gn_r = gn_rnn[0][None]
    cw = rnn_conv_w[0]
    cb = rnn_conv_b[0][None]
    sp = jax.nn.softplus(-lru_lam[0])

    tables = _rope_tables(s)
    xr, gx, q, k, v = _in_proj(x, gsc1, sh1, wi, tables, rope=True, tm=TM_PROJ)
    xrc, _, _, kx, vx = _in_proj(ctx, cgsc1, csh1, wi, tables, rope=False, tm=lx)

    h0 = jnp.zeros((b, 1, D_RNN), F32)
    y_dir = None
    for dr in range(2):
        wg = _gate_weights(lru_w_a[0, dr], lru_w_i[0, dr])
        gate_args = (cw, cb, wg, lru_b_a[0, dr][None], lru_b_i[0, dr][None], sp[dr][None])
        rev = dr == 1
        _, h_ctx = _rglru(xrc, *gate_args, h0, reverse=rev, tm=lx)
        fin = (y_dir, gx, gn_r) if rev else None
        y_dir, _ = _rglru(xr, *gate_args, h_ctx, reverse=rev, tm=TM_SCAN, final_args=fin)
    rn = y_dir

    an = _attention(q, k, v, kx, vx, _attn_bias(), attn_sink[0] * LOG2E, gn_a)

    x1, h2 = _out_proj(rn, an, x, wo_r, wo_a, g1, gsc2, sh2, tm=TM_PROJ)

    return _conv_ffn(h2, x1, w_up[0].astype(BF16), ffn_conv_w[0], ffn_conv_b[0][None],
                     w_down[0].astype(BF16), g2, final_g[None], tm=TM_FFN)
```

```python
import functools

import jax
import jax.numpy as jnp
import numpy as np
from jax import lax
from jax.experimental import pallas as pl
from jax.experimental.pallas import tpu as pltpu

F32 = jnp.float32
BF16 = jnp.bfloat16

EPS = 1e-6
GRID_W = 64
D_RNN = 512
RNN_BLOCKS = 8
RNN_BLOCK_W = D_RNN // RNN_BLOCKS
RNN_CONV_W = 4
RNN_CONV_LEFT = 2
LRU_C = 8.0
HEAD_DIM = 64
N_Q_HEADS = 8
N_KV_HEADS = 2
Q_PER_KV = N_Q_HEADS // N_KV_HEADS
D_ATTN = N_Q_HEADS * HEAD_DIM
D_KV = N_KV_HEADS * HEAD_DIM
WINDOW = 128
BLOCK_Q = 128
ROPE_THETA = 10000.0
NEG_INF = -1e30
LOG2E = 1.4426950408889634
F32_TINY = 2.0 ** -126
FFN_CONV_W = 3
FFN_CONV_LEFT = 1

LANES = 128
SUBLANES_F32 = 8
SUBLANES_BF16 = 16
MXU_DIM = 256
VMEM_LIMIT = 56 * 1024 * 1024

TM_PROJ = 1024
SUB_PROJ = 512
TM_SCAN = 1024
TM_FFN = 512
FFN_CHUNK = 256
ATTN_QB = 2


def _params(*sem):
    return pltpu.CompilerParams(dimension_semantics=sem, vmem_limit_bytes=VMEM_LIMIT)


def _const_spec(shape):
    zeros = (0,) * len(shape)
    return pl.BlockSpec(shape, lambda *_: zeros, pipeline_mode=pl.Buffered(1))


def _mod_kernel(cond_ref, w_ref, b_ref, o_ref):
    s = cond_ref[...]
    s = s * jax.nn.sigmoid(s)
    o_ref[...] = jnp.dot(s, w_ref[...], preferred_element_type=F32,
                         precision=lax.Precision.HIGHEST) + b_ref[...]


def _modulation(cond, w_mod, b_mod):
    rows, d = cond.shape
    n = w_mod.shape[1]
    tn = 768
    return pl.pallas_call(
        _mod_kernel,
        grid=(n // tn,),
        in_specs=[pl.BlockSpec((rows, d), lambda j: (0, 0)),
                  pl.BlockSpec((d, tn), lambda j: (0, j)),
                  pl.BlockSpec((1, tn), lambda j: (0, j))],
        out_specs=pl.BlockSpec((rows, tn), lambda j: (0, j)),
        out_shape=jax.ShapeDtypeStruct((rows, n), F32),
        compiler_params=_params("arbitrary"),
        name="modulation",
    )(cond, w_mod, b_mod.reshape(1, n))


def _rope_partner(t):
    lane = lax.broadcasted_iota(jnp.int32, t.shape, 1)
    first = (lane % 32) < 16
    return jnp.where(first, pltpu.roll(t, LANES - 16, 1), pltpu.roll(t, 16, 1))


def _inproj_kernel(x_ref, gsc_ref, sh_ref, w_ref, cosr_ref, sinr_ref, cosc_ref, sinc_ref,
                   xr_ref, gx_ref, q_ref, k_ref, v_ref, *, rope, sub):
    q0 = 2 * D_RNN
    k0 = q0 + D_ATTN
    v0 = k0 + D_KV
    scale = HEAD_DIM ** -0.5 * LOG2E
    for n in range(x_ref.shape[0] // sub):
        rows = slice(sub * n, sub * (n + 1))
        x = x_ref[rows, :]
        ms = jnp.mean(x * x, axis=-1, keepdims=True)
        h = x * lax.rsqrt(ms + EPS) * gsc_ref[...] + sh_ref[...]
        p = jnp.dot(h.astype(BF16), w_ref[...], preferred_element_type=F32)
        xr_ref[rows, :] = p[:, :D_RNN]
        gx_ref[rows, :] = jax.nn.gelu(p[:, D_RNN:2 * D_RNN])
        cols = [p[:, q0 + LANES * c:q0 + LANES * (c + 1)] for c in range(D_ATTN // LANES)]
        k = p[:, k0:v0]
        if rope:
            per = sub // GRID_W
            trows = slice(per * n, per * (n + 1))
            expand = lambda r_ref, c_ref: (r_ref[trows, :][:, None, :]
                                           + c_ref[...][None, :, :]).reshape(sub, LANES)
            cos = expand(cosr_ref, cosc_ref)
            sin = expand(sinr_ref, sinc_ref)
            cols = [t * cos + _rope_partner(t) * sin for t in cols]
            k = k * cos + _rope_partner(k) * sin
        for c, t in enumerate(cols):
            q_ref[rows, LANES * c:LANES * (c + 1)] = (t * scale).astype(BF16)
        k_ref[rows, :] = k.astype(BF16)
        v_ref[rows, :] = p[:, v0:v0 + D_KV].astype(BF16)


def _in_proj(x, gsc, sh, w_in, tables, *, rope, tm):
    b, s, d = x.shape
    n = w_in.shape[1]
    nt = s // tm
    row = lambda bb, i: (bb, i, 0)
    vec = lambda bb, i: (bb, 0, 0)
    cosr, sinr, cosc, sinc = tables
    if rope:
        assert tm % GRID_W == 0
        rtab = pl.BlockSpec((tm // GRID_W, LANES), lambda bb, i: (i, 0))
    else:
        rtab = _const_spec(cosr.shape)
    outs = [(D_RNN, F32), (D_RNN, F32), (D_ATTN, BF16), (D_KV, BF16), (D_KV, BF16)]
    return pl.pallas_call(
        functools.partial(_inproj_kernel, rope=rope, sub=min(tm, SUB_PROJ)),
        grid=(b, nt),
        in_specs=[pl.BlockSpec((None, tm, d), row),
                  pl.BlockSpec((None, 1, d), vec),
                  pl.BlockSpec((None, 1, d), vec),
                  _const_spec((d, n)),
                  rtab, rtab, _const_spec(cosc.shape), _const_spec(sinc.shape)],
        out_specs=[pl.BlockSpec((None, tm, w), row) for w, _ in outs],
        out_shape=[jax.ShapeDtypeStruct((b, s, w), dt) for w, dt in outs],
        compiler_params=_params("arbitrary", "arbitrary"),
        name="in_proj_rope" if rope else "in_proj_ctx",
    )(x, gsc, sh, w_in, cosr, sinr, cosc, sinc)


def _rglru_kernel(*refs, reverse, final, nt, tm):
    if final:
        (xr_ref, xp_ref, xn_ref, cw_ref, cb_ref, wg_ref, ba_ref, bi_ref, sp_ref, h0_ref,
         yo_ref, gx_ref, gn_ref, y_ref, hfin_ref, xe_scr, a_scr, u_scr, y_scr, h_scr) = refs
    else:
        (xr_ref, xp_ref, xn_ref, cw_ref, cb_ref, wg_ref, ba_ref, bi_ref, sp_ref, h0_ref,
         y_ref, hfin_ref, xe_scr, a_scr, u_scr, y_scr, h_scr) = refs
    i = pl.program_id(0)
    ti = (nt - 1 - i) if reverse else i
    halo = SUBLANES_F32
    nb = xr_ref.shape[0]
    nslab = D_RNN // LANES

    @pl.when(i == 0)
    def _():
        h_scr[...] = h0_ref[...]

    half = tm // 2
    base = halo - RNN_CONV_LEFT
    cw = cw_ref[...]
    cb = cb_ref[...]
    for bb in range(nb):
        cols = []
        for s in range(nslab):
            sl = slice(LANES * s, LANES * (s + 1))
            xs = xe_scr.at[bb * nslab + s]
            xs[0:halo, :] = jnp.where(ti > 0, xp_ref[bb, :, sl], 0.0)
            xs[halo:halo + tm, :] = xr_ref[bb, :, sl]
            xs[halo + tm:, :] = jnp.where(ti < nt - 1, xn_ref[bb, :, sl], 0.0)
            taps = [xs[pl.ds(base + j, half, stride=2), :] for j in range(RNN_CONV_W + 1)]
            even = cb[:, sl]
            odd = cb[:, sl]
            for j in range(RNN_CONV_W):
                even = even + taps[j] * cw[j:j + 1, sl]
                odd = odd + taps[j + 1] * cw[j:j + 1, sl]
            cols.append(jnp.concatenate([even, odd], axis=0))
        xc = jnp.concatenate(cols, axis=1)
        xb = xc.astype(BF16)

        for g in range(D_RNN // MXU_DIM):
            sl = slice(MXU_DIM * g, MXU_DIM * (g + 1))
            z = jnp.dot(xb[:, sl], wg_ref[g], preferred_element_type=F32)
            tr = jnp.tanh(z[:, :MXU_DIM] + ba_ref[:, sl])
            gi = 0.5 * jnp.tanh(z[:, MXU_DIM:] + bi_ref[:, sl]) + 0.5
            log_a = (tr + 1.0) * sp_ref[:, sl]
            a = jnp.exp(log_a)
            a_scr[bb, :, sl] = a
            w = -jnp.tanh(log_a) * (1.0 + a * a)
            u_scr[bb, :, sl] = w * lax.rsqrt(jnp.maximum(w, F32_TINY)) * (gi * xc[:, sl])

    rows = SUBLANES_F32
    ngrp = half // rows
    order = range(rows - 1, -1, -1) if reverse else range(rows)
    parity = (1, 0) if reverse else (0, 1)

    def group(g, hs):
        gg = (ngrp - 1 - g) if reverse else g
        lo = pl.multiple_of(gg * rows, rows)
        hi = pl.multiple_of(half + gg * rows, rows)
        out = pl.multiple_of(gg * 2 * rows, 2 * rows)
        hs = list(hs)
        for j in order:
            for par in parity:
                start = hi if par else lo
                for bb in range(nb):
                    av = a_scr.at[bb, pl.ds(start, rows), :]
                    uv = u_scr.at[bb, pl.ds(start, rows), :]
                    yv = y_scr.at[bb, pl.ds(out, 2 * rows), :]
                    hs[bb] = av[j:j + 1, :] * hs[bb] + uv[j:j + 1, :]
                    yv[2 * j + par:2 * j + par + 1, :] = hs[bb]
        return tuple(hs)

    hs = lax.fori_loop(0, ngrp, group, tuple(h_scr[bb] for bb in range(nb)))
    for bb in range(nb):
        h_scr[bb] = hs[bb]
        hfin_ref[bb] = hs[bb]

    if final:
        z = gx_ref[...] * (y_scr[...] + yo_ref[...])
        ms = jnp.mean(z * z, axis=-1, keepdims=True)
        y_ref[...] = (z * lax.rsqrt(ms + EPS) * gn_ref[...]).astype(BF16)
    else:
        y_ref[...] = y_scr[...]


def _rglru(xr, cw, cb, wg, ba, bi, sp, h0, *, reverse, tm, final_args=None):
    b, s, d = xr.shape
    nt = s // tm
    final = final_args is not None
    nblk = s // SUBLANES_F32
    per = tm // SUBLANES_F32
    tile = (lambda i: nt - 1 - i) if reverse else (lambda i: i)
    row = lambda i: (0, tile(i), 0)
    prev = lambda i: (0, jnp.maximum(tile(i) * per - 1, 0), 0)
    nxt = lambda i: (0, jnp.minimum((tile(i) + 1) * per, nblk - 1), 0)
    in_specs = [pl.BlockSpec((b, tm, d), row),
                pl.BlockSpec((b, SUBLANES_F32, d), prev),
                pl.BlockSpec((b, SUBLANES_F32, d), nxt),
                _const_spec(cw.shape), _const_spec(cb.shape), _const_spec(wg.shape),
                _const_spec(ba.shape), _const_spec(bi.shape), _const_spec(sp.shape),
                _const_spec(h0.shape)]
    args = [xr, xr, xr, cw, cb, wg, ba, bi, sp, h0]
    if final:
        y_other, gx, gn = final_args
        in_specs += [pl.BlockSpec((b, tm, d), row), pl.BlockSpec((b, tm, d), row),
                     _const_spec(gn.shape)]
        args += [y_other, gx, gn]
    return pl.pallas_call(
        functools.partial(_rglru_kernel, reverse=reverse, final=final, nt=nt, tm=tm),
        grid=(nt,),
        in_specs=in_specs,
        out_specs=[pl.BlockSpec((b, tm, d), row), pl.BlockSpec((b, 1, d), lambda i: (0, 0, 0))],
        out_shape=[jax.ShapeDtypeStruct((b, s, d), BF16 if final else F32),
                   jax.ShapeDtypeStruct((b, 1, d), F32)],
        scratch_shapes=[pltpu.VMEM((b * d // LANES, tm + 2 * SUBLANES_F32, LANES), F32),
                        pltpu.VMEM((b, tm, d), F32), pltpu.VMEM((b, tm, d), F32),
                        pltpu.VMEM((b, tm, d), F32), pltpu.VMEM((b, 1, d), F32)],
        compiler_params=_params("arbitrary"),
        name="rglru_" + ("bwd" if reverse else "fwd") + ("_final" if final else ""),
    )(*args)


def _attn_kernel(q_ref, kp_ref, kc_ref, kn_ref, vp_ref, vc_ref, vn_ref, kx_ref, vx_ref,
                 bfirst_ref, bmid_ref, blast_ref, sink_ref, gn_ref, o_ref, keys, vals, valx, *, qb):
    ncol = D_ATTN // LANES
    nwin = 3 * BLOCK_Q
    mid = slice(BLOCK_Q, BLOCK_Q * (qb + 1))
    keys[:BLOCK_Q, :] = kp_ref[...]
    keys[mid, :] = kc_ref[...]
    keys[BLOCK_Q * (qb + 1):, :] = kn_ref[...]
    vals[:BLOCK_Q, :D_KV] = vp_ref[...]
    vals[mid, :D_KV] = vc_ref[...]
    vals[BLOCK_Q * (qb + 1):, :D_KV] = vn_ref[...]
    valx[:, :D_KV] = vx_ref[...]
    vals[:, D_KV:] = jnp.ones((vals.shape[0], LANES), BF16)
    valx[:, D_KV:] = jnp.ones((valx.shape[0], LANES), BF16)

    low = lax.broadcasted_iota(jnp.int32, (BLOCK_Q, LANES), 1) < HEAD_DIM
    first = lax.broadcasted_iota(jnp.int32, (2 * BLOCK_Q, 1), 0) < BLOCK_Q
    zero = jnp.zeros((BLOCK_Q, LANES), BF16)
    nt_dims = (((1,), (1,)), ((), ()))
    for x in range(qb):
        rows = slice(BLOCK_Q * x, BLOCK_Q * (x + 1))
        win = slice(BLOCK_Q * x, BLOCK_Q * x + nwin)
        bias = (bfirst_ref if x == 0 else blast_ref if x == qb - 1 else bmid_ref)[...]
        outs = []
        for c in range(ncol):
            t = q_ref[rows, LANES * c:LANES * (c + 1)]
            qs = jnp.concatenate([jnp.where(low, t, zero), jnp.where(low, zero, t)], axis=0)
            s_loc = lax.dot_general(qs, keys[win, :], nt_dims, preferred_element_type=F32)
            s_ctx = lax.dot_general(qs, kx_ref[...], nt_dims, preferred_element_type=F32)
            s_loc = (s_loc.reshape(2, BLOCK_Q, nwin) + bias[None]).reshape(2 * BLOCK_Q, nwin)
            sink = jnp.where(first, sink_ref[c], sink_ref[ncol + c])
            m = jnp.maximum(jnp.maximum(jnp.max(s_loc, axis=-1, keepdims=True),
                                        jnp.max(s_ctx, axis=-1, keepdims=True)), sink)
            e_loc = jnp.exp2(s_loc - m).astype(BF16)
            e_ctx = jnp.exp2(s_ctx - m).astype(BF16)
            pv = (jnp.dot(e_loc, vals[win, :], preferred_element_type=F32)
                  + jnp.dot(e_ctx, valx[...], preferred_element_type=F32))
            o = pv[:, :D_KV] / (pv[:, D_KV:] + jnp.exp2(sink - m))
            outs.append(jnp.where(low, o[:BLOCK_Q], o[BLOCK_Q:]))
        ms = sum(jnp.sum(t * t, axis=-1, keepdims=True) for t in outs) * (1.0 / D_ATTN)
        inv = lax.rsqrt(ms + EPS)
        for c, t in enumerate(outs):
            sl = slice(LANES * c, LANES * (c + 1))
            o_ref[rows, sl] = (t * inv * gn_ref[:, sl]).astype(BF16)


def _attention(q, k, v, kx, vx, bias, sink, gn):
    b, s, _ = q.shape
    qb = ATTN_QB
    nblk = s // BLOCK_Q
    nstep = nblk // qb
    lx = kx.shape[1]
    cur = lambda bb, n: (bb, n, 0)
    prev = lambda bb, n: (bb, jnp.maximum(n * qb - 1, 0), 0)
    nxt = lambda bb, n: (bb, jnp.minimum((n + 1) * qb, nblk - 1), 0)
    ctx = lambda bb, n: (bb, 0, 0)
    first = lambda bb, n: (jnp.where(n == 0, 0, 1), 0, 0)
    last = lambda bb, n: (jnp.where(n == nstep - 1, 2, 1), 0, 0)
    edge = lambda im: pl.BlockSpec((None, BLOCK_Q, 3 * BLOCK_Q), im)
    halo = lambda im: pl.BlockSpec((None, BLOCK_Q, D_KV), im)
    body = pl.BlockSpec((None, qb * BLOCK_Q, D_KV), cur)
    side = pl.BlockSpec((None, lx, D_KV), ctx)
    return pl.pallas_call(
        functools.partial(_attn_kernel, qb=qb),
        grid=(b, nstep),
        in_specs=[pl.BlockSpec((None, qb * BLOCK_Q, D_ATTN), cur),
                  halo(prev), body, halo(nxt), halo(prev), body, halo(nxt), side, side,
                  edge(first), edge(lambda bb, n: (1, 0, 0)), edge(last),
                  pl.BlockSpec(memory_space=pltpu.SMEM), _const_spec(gn.shape)],
        out_specs=pl.BlockSpec((None, qb * BLOCK_Q, D_ATTN), cur),
        out_shape=jax.ShapeDtypeStruct((b, s, D_ATTN), BF16),
        scratch_shapes=[pltpu.VMEM(((qb + 2) * BLOCK_Q, D_KV), BF16),
                        pltpu.VMEM(((qb + 2) * BLOCK_Q, D_KV + LANES), BF16),
                        pltpu.VMEM((lx, D_KV + LANES), BF16)],
        compiler_params=_params("arbitrary", "arbitrary"),
        name="attention",
    )(q, k, k, k, v, v, v, kx, vx, bias, bias, bias, sink, gn)


def _attn_bias():
    i = np.arange(BLOCK_Q)[:, None]
    j = np.arange(3 * BLOCK_Q)[None, :]
    band = np.abs(i + BLOCK_Q - j) <= WINDOW
    variants = [band & (j >= BLOCK_Q), band, band & (j < 2 * BLOCK_Q)]
    return jnp.asarray(np.stack([np.where(ok, 0.0, NEG_INF) for ok in variants]).astype(np.float32))


def _ffn_kernel(rn_ref, rnp_ref, rnn_ref, an_ref, anp_ref, ann_ref, x_ref, xp_ref, xn_ref,
                wo_ref, g1_ref, gsc_ref, sh_ref, wu_ref, cw_ref, cb_ref, wd_ref, g2_ref, fg_ref,
                o_ref, mix, x1s, hext, ug, uv, act, acc, nat, *, nt, tm):
    d_ff = wd_ref.shape[0]
    nc = d_ff // FFN_CHUNK
    span = lambda j, branch: slice(branch * d_ff + j * FFN_CHUNK, branch * d_ff + (j + 1) * FFN_CHUNK)
    i = pl.program_id(1)
    halo = SUBLANES_BF16
    body = slice(halo, halo + tm)
    tail = slice(halo + tm, 2 * halo + tm)
    for rows, r_ref, a_ref in ((slice(0, halo), rnp_ref, anp_ref), (body, rn_ref, an_ref),
                               (tail, rnn_ref, ann_ref)):
        mix[rows, :D_RNN] = r_ref[...]
        mix[rows, D_RNN:] = a_ref[...]
    proj = jnp.dot(mix[...], wo_ref[...], preferred_element_type=F32)

    def residual(xin_ref, rows):
        return xin_ref[...] + g1_ref[...] * proj[rows]

    def modulated(x1):
        ms = jnp.mean(x1 * x1, axis=-1, keepdims=True)
        return (x1 * lax.rsqrt(ms + EPS) * gsc_ref[...] + sh_ref[...]).astype(BF16)

    x1 = residual(x_ref, body)
    x1s[...] = x1
    hext[body] = modulated(x1)
    zeros = jnp.zeros((halo, hext.shape[1]), BF16)
    hext[0:halo] = jnp.where(i > 0, modulated(residual(xp_ref, slice(0, halo))), zeros)
    hext[tail] = jnp.where(i < nt - 1, modulated(residual(xn_ref, tail)), zeros)
    base = halo - FFN_CONV_LEFT

    half = tm // 2
    nslab = FFN_CHUNK // LANES

    def up(j, slot):
        he = hext[...]
        g = jnp.dot(he, wu_ref[:, span(j,0)], preferred_element_type=F32)
        v = jnp.dot(he, wu_ref[:, span(j,1)], preferred_element_type=F32)
        for s in range(nslab):
            ug[slot, s] = g[:, LANES * s:LANES * (s + 1)]
            uv[slot, s] = v[:, LANES * s:LANES * (s + 1)]

    def conv(u, slot, cw, cb):
        cols = []
        for s in range(nslab):
            sl = slice(LANES * s, LANES * (s + 1))
            taps = [u[slot, s, pl.ds(base + t, half, stride=2), :] for t in range(FFN_CONV_W + 1)]
            even = cb[:, sl]
            odd = cb[:, sl]
            for t in range(FFN_CONV_W):
                even = even + taps[t] * cw[t:t + 1, sl]
                odd = odd + taps[t + 1] * cw[t:t + 1, sl]
            cols.append(jnp.concatenate([even, odd], axis=0))
        return jnp.concatenate(cols, axis=1)

    def gate(j, slot):
        hg = 0.5 * conv(ug, slot, cw_ref[:, span(j,0)], cb_ref[:, span(j,0)])
        silu = hg * jnp.tanh(hg) + hg
        val = conv(uv, slot, cw_ref[:, span(j,1)], cb_ref[:, span(j,1)])
        act[slot] = (silu * val).astype(BF16)

    def down(j, slot):
        return jnp.dot(act[slot], wd_ref[span(j,0), :], preferred_element_type=F32)

    assert nc % 2 == 1
    up(0, 0)
    acc[...] = jnp.zeros_like(acc)
    for j in range(0, nc - 1, 2):
        up(j + 1, 1)
        gate(j, 0)
        up(j + 2, 0)
        gate(j + 1, 1)
        acc[...] += down(j, 0) + down(j + 1, 1)
    gate(nc - 1, 0)
    mixed = acc[...] + down(nc - 1, 0)
    nlane = mixed.shape[1] // LANES
    for s in range(nlane):
        sl = slice(LANES * s, LANES * (s + 1))
        nat[s, pl.ds(0, half, stride=2), :] = mixed[:half, sl]
        nat[s, pl.ds(1, half, stride=2), :] = mixed[half:, sl]
    ffn = jnp.concatenate([nat[s] for s in range(nlane)], axis=1)
    y = x1s[...] + g2_ref[...] * ffn
    ms = jnp.mean(y * y, axis=-1, keepdims=True)
    o_ref[...] = y * lax.rsqrt(ms + EPS) * fg_ref[...]


def _mixer_out_ffn(rn, an, x, wo, g1, gsc, sh, wu, cw, cb, wd, g2, fg, *, tm):
    b, s, d = x.shape
    nt = s // tm
    assert wd.shape[0] % FFN_CHUNK == 0 and (wd.shape[0] // FFN_CHUNK) % 2 == 1
    halo = SUBLANES_BF16
    per = tm // halo
    nblk = s // halo
    row = lambda bb, i: (bb, i, 0)
    prev = lambda bb, i: (bb, jnp.maximum(i * per - 1, 0), 0)
    nxt = lambda bb, i: (bb, jnp.minimum((i + 1) * per, nblk - 1), 0)
    vec = lambda bb, i: (bb, 0, 0)
    banded = lambda w: [pl.BlockSpec((None, tm, w), row), pl.BlockSpec((None, halo, w), prev),
                        pl.BlockSpec((None, halo, w), nxt)]
    per_sample = pl.BlockSpec((None, 1, d), vec)
    return pl.pallas_call(
        functools.partial(_ffn_kernel, nt=nt, tm=tm),
        grid=(b, nt),
        in_specs=banded(D_RNN) + banded(D_ATTN) + banded(d)
        + [_const_spec(wo.shape), per_sample, per_sample, per_sample,
           _const_spec(wu.shape), _const_spec(cw.shape), _const_spec(cb.shape),
           _const_spec(wd.shape), per_sample, _const_spec(fg.shape)],
        out_specs=pl.BlockSpec((None, tm, d), row),
        out_shape=jax.ShapeDtypeStruct((b, s, d), F32),
        scratch_shapes=[pltpu.VMEM((tm + 2 * halo, D_RNN + D_ATTN), BF16),
                        pltpu.VMEM((tm, d), F32),
                        pltpu.VMEM((tm + 2 * halo, d), BF16),
                        pltpu.VMEM((2, FFN_CHUNK // LANES, tm + 2 * halo, LANES), F32),
                        pltpu.VMEM((2, FFN_CHUNK // LANES, tm + 2 * halo, LANES), F32),
                        pltpu.VMEM((2, tm, FFN_CHUNK), BF16),
                        pltpu.VMEM((tm, d), F32),
                        pltpu.VMEM((d // LANES, tm, LANES), F32)],
        compiler_params=_params("arbitrary", "arbitrary"),
        name="out_proj_conv_ffn",
    )(rn, rn, rn, an, an, an, x, x, x, wo, g1, gsc, sh, wu, cw, cb, wd, g2, fg)


def _rope_tables(s):
    half = HEAD_DIM // 4
    inv = ROPE_THETA ** (-jnp.arange(half, dtype=F32) / half)

    def tables(npos, lanes_first):
        ang = jnp.arange(npos, dtype=jnp.int32).astype(F32)[:, None] * inv[None, :]
        zero = jnp.zeros((npos, 2 * half), F32)
        cos = jnp.concatenate([jnp.cos(ang)] * 2, axis=1)
        sin = jnp.concatenate([-jnp.sin(ang), jnp.sin(ang)], axis=1)
        order = (lambda t: [t, zero]) if lanes_first else (lambda t: [zero, t])
        reps = LANES // HEAD_DIM
        return [jnp.tile(jnp.concatenate(order(t), axis=1), (1, reps)) for t in (cos, sin)]

    cosr, sinr = tables(s // GRID_W, True)
    cosc, sinc = tables(GRID_W, False)
    return cosr, sinr, cosc, sinc


def _gate_weights(w_a, w_i):
    per = MXU_DIM // RNN_BLOCK_W
    eye = jnp.eye(per, dtype=F32)

    def dense(w):
        w = w.reshape(RNN_BLOCKS // per, per, RNN_BLOCK_W, RNN_BLOCK_W)
        m = w[:, :, :, None, :] * eye[None, :, None, :, None]
        return m.reshape(RNN_BLOCKS // per, MXU_DIM, MXU_DIM)

    return (0.5 * jnp.concatenate([dense(w_a), dense(w_i)], axis=2)).astype(BF16)


def _interleave_heads(w, axis):
    shape = w.shape
    w = w.reshape(shape[:axis] + (N_KV_HEADS, Q_PER_KV, HEAD_DIM) + shape[axis + 1:])
    return jnp.swapaxes(w, axis, axis + 1).reshape(shape)


def kernel(x, c, ctx, c_ctx, w_mod, b_mod, norm1_g, w_in, rnn_conv_w, rnn_conv_b, lru_w_a, lru_b_a,
           lru_w_i, lru_b_i, lru_lam, attn_sink, gn_rnn, gn_attn, w_out, norm2_g, w_up, ffn_conv_w,
           ffn_conv_b, w_down, final_g):
    assert w_mod.shape[0] == 1, "one layer: the last layer's context outputs are never consumed"
    b, s, d = x.shape
    lx = ctx.shape[1]

    cond = jnp.concatenate([c, c_ctx[None], jnp.zeros((SUBLANES_F32 - b - 1, d), F32)], axis=0)
    mod = _modulation(cond, w_mod[0], b_mod[0])
    sh1, sc1, g1, sh2, sc2, g2 = [mod[:b, None, d * n:d * (n + 1)] for n in range(6)]
    csh1 = jnp.broadcast_to(mod[b:b + 1, None, :d], (b, 1, d))
    csc1 = jnp.broadcast_to(mod[b:b + 1, None, d:2 * d], (b, 1, d))
    gsc1 = norm1_g[0] * (1.0 + sc1)
    cgsc1 = norm1_g[0] * (1.0 + csc1)
    gsc2 = norm2_g[0] * (1.0 + sc2)

    wi = w_in[0].astype(BF16)
    q0 = 2 * D_RNN
    wi = jnp.concatenate([wi[:, :q0], _interleave_heads(wi[:, q0:q0 + D_ATTN], 1),
                          wi[:, q0 + D_ATTN:]], axis=1)
    wo = w_out[0].astype(BF16)
    wo = jnp.concatenate([wo[:D_RNN], _interleave_heads(wo[D_RNN:], 0)], axis=0)
    gn_a = _interleave_heads(gn_attn[0], 0)[None]
    gn_r = gn_rnn[0][None]
    cw = rnn_conv_w[0]
    cb = rnn_conv_b[0][None]
    sp = (-0.5 * LRU_C) * jax.nn.softplus(-lru_lam[0])

    tables = _rope_tables(s)
    xr, gx, q, k, v = _in_proj(x, gsc1, sh1, wi, tables, rope=True, tm=TM_PROJ)
    xrc, _, _, kx, vx = _in_proj(ctx, cgsc1, csh1, wi, tables, rope=False, tm=lx)

    h0 = jnp.zeros((b, 1, D_RNN), F32)
    y_dir = None
    for dr in range(2):
        wg = _gate_weights(lru_w_a[0, dr], lru_w_i[0, dr])
        gate_args = (cw, cb, wg, 0.5 * lru_b_a[0, dr][None], 0.5 * lru_b_i[0, dr][None], sp[dr][None])
        rev = dr == 1
        _, h_ctx = _rglru(xrc, *gate_args, h0, reverse=rev, tm=lx)
        fin = (y_dir, gx, gn_r) if rev else None
        y_dir, _ = _rglru(xr, *gate_args, h_ctx, reverse=rev, tm=TM_SCAN, final_args=fin)
    rn = y_dir

    an = _attention(q, k, v, kx, vx, _attn_bias(), attn_sink[0] * LOG2E, gn_a)

    return _mixer_out_ffn(rn, an, x, wo, g1, gsc2, sh2, w_up[0].astype(BF16), ffn_conv_w[0],
                          ffn_conv_b[0][None], w_down[0].astype(BF16), g2, final_g[None], tm=TM_FFN)
```

```python
import functools

import jax
import jax.numpy as jnp
import numpy as np
from jax import lax
from jax.experimental import pallas as pl
from jax.experimental.pallas import tpu as pltpu

F32 = jnp.float32
BF16 = jnp.bfloat16

EPS = 1e-6
GRID_W = 64
D_RNN = 512
RNN_BLOCKS = 8
RNN_BLOCK_W = D_RNN // RNN_BLOCKS
RNN_CONV_W = 4
RNN_CONV_LEFT = 2
LRU_C = 8.0
HEAD_DIM = 64
N_Q_HEADS = 8
N_KV_HEADS = 2
Q_PER_KV = N_Q_HEADS // N_KV_HEADS
D_ATTN = N_Q_HEADS * HEAD_DIM
D_KV = N_KV_HEADS * HEAD_DIM
WINDOW = 128
BLOCK_Q = 128
ROPE_THETA = 10000.0
NEG_INF = -1e30
LOG2E = 1.4426950408889634
F32_TINY = 2.0 ** -126
FFN_CONV_W = 3
FFN_CONV_LEFT = 1

LANES = 128
SUBLANES_F32 = 8
SUBLANES_BF16 = 16
MXU_DIM = 256
VMEM_LIMIT = 56 * 1024 * 1024

TM_PROJ = 1024
SUB_PROJ = 512
TM_SCAN = 1024
TM_FFN = 512
FFN_CHUNK = 256
ATTN_QB = 4


def _params(*sem):
    return pltpu.CompilerParams(dimension_semantics=sem, vmem_limit_bytes=VMEM_LIMIT)


def _const_spec(shape):
    zeros = (0,) * len(shape)
    return pl.BlockSpec(shape, lambda *_: zeros, pipeline_mode=pl.Buffered(1))


def _mod_kernel(cond_ref, w_ref, b_ref, o_ref):
    s = cond_ref[...]
    s = s * jax.nn.sigmoid(s)
    o_ref[...] = jnp.dot(s, w_ref[...], preferred_element_type=F32,
                         precision=lax.Precision.HIGHEST) + b_ref[...]


def _modulation(cond, w_mod, b_mod):
    rows, d = cond.shape
    n = w_mod.shape[1]
    tn = 768
    return pl.pallas_call(
        _mod_kernel,
        grid=(n // tn,),
        in_specs=[pl.BlockSpec((rows, d), lambda j: (0, 0)),
                  pl.BlockSpec((d, tn), lambda j: (0, j)),
                  pl.BlockSpec((1, tn), lambda j: (0, j))],
        out_specs=pl.BlockSpec((rows, tn), lambda j: (0, j)),
        out_shape=jax.ShapeDtypeStruct((rows, n), F32),
        compiler_params=_params("arbitrary"),
        name="modulation",
    )(cond, w_mod, b_mod.reshape(1, n))


def _rope_partner(t):
    lane = lax.broadcasted_iota(jnp.int32, t.shape, 1)
    first = (lane % 32) < 16
    return jnp.where(first, pltpu.roll(t, LANES - 16, 1), pltpu.roll(t, 16, 1))


def _inproj_kernel(x_ref, gsc_ref, sh_ref, w_ref, cosr_ref, sinr_ref, cosc_ref, sinc_ref,
                   xr_ref, gx_ref, q_ref, kv_ref, *, rope, sub):
    q0 = 2 * D_RNN
    k0 = q0 + D_ATTN
    v0 = k0 + D_KV
    scale = HEAD_DIM ** -0.5 * LOG2E
    for n in range(x_ref.shape[0] // sub):
        rows = slice(sub * n, sub * (n + 1))
        x = x_ref[rows, :]
        ms = jnp.mean(x * x, axis=-1, keepdims=True)
        h = x * lax.rsqrt(ms + EPS) * gsc_ref[...] + sh_ref[...]
        p = jnp.dot(h.astype(BF16), w_ref[...], preferred_element_type=F32)
        xr_ref[rows, :] = p[:, :D_RNN]
        gx_ref[rows, :] = jax.nn.gelu(p[:, D_RNN:2 * D_RNN])
        cols = [p[:, q0 + LANES * c:q0 + LANES * (c + 1)] for c in range(D_ATTN // LANES)]
        k = p[:, k0:v0]
        if rope:
            per = sub // GRID_W
            trows = slice(per * n, per * (n + 1))
            expand = lambda r_ref, c_ref: (r_ref[trows, :][:, None, :]
                                           + c_ref[...][None, :, :]).reshape(sub, LANES)
            cos = expand(cosr_ref, cosc_ref)
            sin = expand(sinr_ref, sinc_ref)
            cols = [t * cos + _rope_partner(t) * sin for t in cols]
            k = k * cos + _rope_partner(k) * sin
        for c, t in enumerate(cols):
            q_ref[rows, LANES * c:LANES * (c + 1)] = (t * scale).astype(BF16)
        kv_ref[rows, :D_KV] = k.astype(BF16)
        kv_ref[rows, D_KV:] = p[:, v0:v0 + D_KV].astype(BF16)


def _in_proj(x, gsc, sh, w_in, tables, *, rope, tm):
    b, s, d = x.shape
    n = w_in.shape[1]
    nt = s // tm
    row = lambda bb, i: (bb, i, 0)
    vec = lambda bb, i: (bb, 0, 0)
    cosr, sinr, cosc, sinc = tables
    if rope:
        assert tm % GRID_W == 0
        rtab = pl.BlockSpec((tm // GRID_W, LANES), lambda bb, i: (i, 0))
    else:
        rtab = _const_spec(cosr.shape)
    outs = [(D_RNN, F32), (D_RNN, F32), (D_ATTN, BF16), (2 * D_KV, BF16)]
    return pl.pallas_call(
        functools.partial(_inproj_kernel, rope=rope, sub=min(tm, SUB_PROJ)),
        grid=(b, nt),
        in_specs=[pl.BlockSpec((None, tm, d), row),
                  pl.BlockSpec((None, 1, d), vec),
                  pl.BlockSpec((None, 1, d), vec),
                  _const_spec((d, n)),
                  rtab, rtab, _const_spec(cosc.shape), _const_spec(sinc.shape)],
        out_specs=[pl.BlockSpec((None, tm, w), row) for w, _ in outs],
        out_shape=[jax.ShapeDtypeStruct((b, s, w), dt) for w, dt in outs],
        compiler_params=_params("arbitrary", "arbitrary"),
        name="in_proj_rope" if rope else "in_proj_ctx",
    )(x, gsc, sh, w_in, cosr, sinr, cosc, sinc)


def _rglru_kernel(*refs, reverse, final, nt, tm):
    if final:
        (xr_ref, xp_ref, xn_ref, cw_ref, cb_ref, wg_ref, ba_ref, bi_ref, sp_ref, h0_ref,
         yo_ref, gx_ref, gn_ref, y_ref, hfin_ref, xe_scr, a_scr, u_scr, y_scr, h_scr) = refs
    else:
        (xr_ref, xp_ref, xn_ref, cw_ref, cb_ref, wg_ref, ba_ref, bi_ref, sp_ref, h0_ref,
         y_ref, hfin_ref, xe_scr, a_scr, u_scr, y_scr, h_scr) = refs
    i = pl.program_id(0)
    ti = (nt - 1 - i) if reverse else i
    halo = SUBLANES_F32
    nb = xr_ref.shape[0]
    nslab = D_RNN // LANES

    @pl.when(i == 0)
    def _():
        h_scr[...] = h0_ref[...]

    half = tm // 2
    base = halo - RNN_CONV_LEFT
    cw = cw_ref[...]
    cb = cb_ref[...]
    for bb in range(nb):
        cols = []
        for s in range(nslab):
            sl = slice(LANES * s, LANES * (s + 1))
            xs = xe_scr.at[bb * nslab + s]
            xs[0:halo, :] = jnp.where(ti > 0, xp_ref[bb, :, sl], 0.0)
            xs[halo:halo + tm, :] = xr_ref[bb, :, sl]
            xs[halo + tm:, :] = jnp.where(ti < nt - 1, xn_ref[bb, :, sl], 0.0)
            taps = [xs[pl.ds(base + j, half, stride=2), :] for j in range(RNN_CONV_W + 1)]
            even = cb[:, sl]
            odd = cb[:, sl]
            for j in range(RNN_CONV_W):
                even = even + taps[j] * cw[j:j + 1, sl]
                odd = odd + taps[j + 1] * cw[j:j + 1, sl]
            cols.append(jnp.concatenate([even, odd], axis=0))
        xc = jnp.concatenate(cols, axis=1)
        xb = xc.astype(BF16)

        for g in range(D_RNN // MXU_DIM):
            sl = slice(MXU_DIM * g, MXU_DIM * (g + 1))
            z = jnp.dot(xb[:, sl], wg_ref[g], preferred_element_type=F32)
            tr = jnp.tanh(z[:, :MXU_DIM] + ba_ref[:, sl])
            gi = 0.5 * jnp.tanh(z[:, MXU_DIM:] + bi_ref[:, sl]) + 0.5
            log_a = (tr + 1.0) * sp_ref[:, sl]
            a = jnp.exp(log_a)
            a_scr[bb, :, sl] = a
            w = -jnp.tanh(log_a) * (1.0 + a * a)
            u_scr[bb, :, sl] = w * lax.rsqrt(jnp.maximum(w, F32_TINY)) * (gi * xc[:, sl])

    rows = SUBLANES_F32
    ngrp = half // rows
    order = range(rows - 1, -1, -1) if reverse else range(rows)
    parity = (1, 0) if reverse else (0, 1)

    def group(g, hs):
        gg = (ngrp - 1 - g) if reverse else g
        lo = pl.multiple_of(gg * rows, rows)
        hi = pl.multiple_of(half + gg * rows, rows)
        out = pl.multiple_of(gg * 2 * rows, 2 * rows)
        hs = list(hs)
        for j in order:
            for par in parity:
                start = hi if par else lo
                for bb in range(nb):
                    av = a_scr.at[bb, pl.ds(start, rows), :]
                    uv = u_scr.at[bb, pl.ds(start, rows), :]
                    yv = y_scr.at[bb, pl.ds(out, 2 * rows), :]
                    hs[bb] = av[j:j + 1, :] * hs[bb] + uv[j:j + 1, :]
                    yv[2 * j + par:2 * j + par + 1, :] = hs[bb]
        return tuple(hs)

    hs = lax.fori_loop(0, ngrp, group, tuple(h_scr[bb] for bb in range(nb)))
    for bb in range(nb):
        h_scr[bb] = hs[bb]
        hfin_ref[bb] = hs[bb]

    if final:
        z = gx_ref[...] * (y_scr[...] + yo_ref[...])
        ms = jnp.mean(z * z, axis=-1, keepdims=True)
        y_ref[...] = (z * lax.rsqrt(ms + EPS) * gn_ref[...]).astype(BF16)
    else:
        y_ref[...] = y_scr[...]


def _rglru(xr, cw, cb, wg, ba, bi, sp, h0, *, reverse, tm, final_args=None):
    b, s, d = xr.shape
    nt = s // tm
    final = final_args is not None
    nblk = s // SUBLANES_F32
    per = tm // SUBLANES_F32
    tile = (lambda i: nt - 1 - i) if reverse else (lambda i: i)
    row = lambda i: (0, tile(i), 0)
    prev = lambda i: (0, jnp.maximum(tile(i) * per - 1, 0), 0)
    nxt = lambda i: (0, jnp.minimum((tile(i) + 1) * per, nblk - 1), 0)
    in_specs = [pl.BlockSpec((b, tm, d), row),
                pl.BlockSpec((b, SUBLANES_F32, d), prev),
                pl.BlockSpec((b, SUBLANES_F32, d), nxt),
                _const_spec(cw.shape), _const_spec(cb.shape), _const_spec(wg.shape),
                _const_spec(ba.shape), _const_spec(bi.shape), _const_spec(sp.shape),
                _const_spec(h0.shape)]
    args = [xr, xr, xr, cw, cb, wg, ba, bi, sp, h0]
    if final:
        y_other, gx, gn = final_args
        in_specs += [pl.BlockSpec((b, tm, d), row), pl.BlockSpec((b, tm, d), row),
                     _const_spec(gn.shape)]
        args += [y_other, gx, gn]
    return pl.pallas_call(
        functools.partial(_rglru_kernel, reverse=reverse, final=final, nt=nt, tm=tm),
        grid=(nt,),
        in_specs=in_specs,
        out_specs=[pl.BlockSpec((b, tm, d), row), pl.BlockSpec((b, 1, d), lambda i: (0, 0, 0))],
        out_shape=[jax.ShapeDtypeStruct((b, s, d), BF16 if final else F32),
                   jax.ShapeDtypeStruct((b, 1, d), F32)],
        scratch_shapes=[pltpu.VMEM((b * d // LANES, tm + 2 * SUBLANES_F32, LANES), F32),
                        pltpu.VMEM((b, tm, d), F32), pltpu.VMEM((b, tm, d), F32),
                        pltpu.VMEM((b, tm, d), F32), pltpu.VMEM((b, 1, d), F32)],
        compiler_params=_params("arbitrary"),
        name="rglru_" + ("bwd" if reverse else "fwd") + ("_final" if final else ""),
    )(*args)


def _attn_kernel(q_ref, kvp_ref, kvc_ref, kvn_ref, kvx_ref, bias_ref, sink_ref, gn_ref, o_ref,
                 keys, vals, keyx, valx, *, qb, nstep):
    ncol = D_ATTN // LANES
    nwin = 3 * BLOCK_Q
    step = pl.program_id(1)
    band = ((slice(0, BLOCK_Q), kvp_ref), (slice(BLOCK_Q, BLOCK_Q * (qb + 1)), kvc_ref),
            (slice(BLOCK_Q * (qb + 1), BLOCK_Q * (qb + 2)), kvn_ref))
    for rows, kv_ref in band:
        keys[rows, :] = kv_ref[:, :D_KV]
        vals[rows, :D_KV] = kv_ref[:, D_KV:]
    keyx[...] = kvx_ref[:, :D_KV]
    valx[:, :D_KV] = kvx_ref[:, D_KV:]
    vals[:, D_KV:] = jnp.ones((vals.shape[0], LANES), BF16)
    valx[:, D_KV:] = jnp.ones((valx.shape[0], LANES), BF16)

    low = lax.broadcasted_iota(jnp.int32, (BLOCK_Q, LANES), 1) < HEAD_DIM
    first = lax.broadcasted_iota(jnp.int32, (2 * BLOCK_Q, 1), 0) < BLOCK_Q
    zero = jnp.zeros((BLOCK_Q, LANES), BF16)
    nt_dims = (((1,), (1,)), ((), ()))
    for x in range(qb):
        rows = slice(BLOCK_Q * x, BLOCK_Q * (x + 1))
        win = slice(BLOCK_Q * x, BLOCK_Q * x + nwin)
        variant = 1
        if x == 0:
            variant = jnp.where(step == 0, 0, variant)
        if x == qb - 1:
            variant = jnp.where(step == nstep - 1, 2, variant)
        bias = bias_ref[variant]
        outs = []
        for c in range(ncol):
            t = q_ref[rows, LANES * c:LANES * (c + 1)]
            qs = jnp.concatenate([jnp.where(low, t, zero), jnp.where(low, zero, t)], axis=0)
            s_loc = lax.dot_general(qs, keys[win, :], nt_dims, preferred_element_type=F32)
            s_ctx = lax.dot_general(qs, keyx[...], nt_dims, preferred_element_type=F32)
            s_loc = (s_loc.reshape(2, BLOCK_Q, nwin) + bias[None]).reshape(2 * BLOCK_Q, nwin)
            sink = jnp.where(first, sink_ref[c], sink_ref[ncol + c])
            m = jnp.maximum(jnp.maximum(jnp.max(s_loc, axis=-1, keepdims=True),
                                        jnp.max(s_ctx, axis=-1, keepdims=True)), sink)
            e_loc = jnp.exp2(s_loc - m).astype(BF16)
            e_ctx = jnp.exp2(s_ctx - m).astype(BF16)
            pv = (jnp.dot(e_loc, vals[win, :], preferred_element_type=F32)
                  + jnp.dot(e_ctx, valx[...], preferred_element_type=F32))
            o = pv[:, :D_KV] / (pv[:, D_KV:] + jnp.exp2(sink - m))
            outs.append(jnp.where(low, o[:BLOCK_Q], o[BLOCK_Q:]))
        ms = sum(jnp.sum(t * t, axis=-1, keepdims=True) for t in outs) * (1.0 / D_ATTN)
        inv = lax.rsqrt(ms + EPS)
        for c, t in enumerate(outs):
            sl = slice(LANES * c, LANES * (c + 1))
            o_ref[rows, sl] = (t * inv * gn_ref[:, sl]).astype(BF16)


def _attention(q, kv, kvx, bias, sink, gn):
    b, s, _ = q.shape
    qb = ATTN_QB
    nblk = s // BLOCK_Q
    nstep = nblk // qb
    lx = kvx.shape[1]
    cur = lambda bb, n: (bb, n, 0)
    prev = lambda bb, n: (bb, jnp.maximum(n * qb - 1, 0), 0)
    nxt = lambda bb, n: (bb, jnp.minimum((n + 1) * qb, nblk - 1), 0)
    halo = lambda im: pl.BlockSpec((None, BLOCK_Q, 2 * D_KV), im)
    return pl.pallas_call(
        functools.partial(_attn_kernel, qb=qb, nstep=nstep),
        grid=(b, nstep),
        in_specs=[pl.BlockSpec((None, qb * BLOCK_Q, D_ATTN), cur),
                  halo(prev), pl.BlockSpec((None, qb * BLOCK_Q, 2 * D_KV), cur), halo(nxt),
                  pl.BlockSpec((None, lx, 2 * D_KV), lambda bb, n: (bb, 0, 0)),
                  _const_spec(bias.shape), pl.BlockSpec(memory_space=pltpu.SMEM),
                  _const_spec(gn.shape)],
        out_specs=pl.BlockSpec((None, qb * BLOCK_Q, D_ATTN), cur),
        out_shape=jax.ShapeDtypeStruct((b, s, D_ATTN), BF16),
        scratch_shapes=[pltpu.VMEM(((qb + 2) * BLOCK_Q, D_KV), BF16),
                        pltpu.VMEM(((qb + 2) * BLOCK_Q, D_KV + LANES), BF16),
                        pltpu.VMEM((lx, D_KV), BF16),
                        pltpu.VMEM((lx, D_KV + LANES), BF16)],
        compiler_params=_params("arbitrary", "arbitrary"),
        name="attention",
    )(q, kv, kv, kv, kvx, bias, sink, gn)


def _attn_bias():
    i = np.arange(BLOCK_Q)[:, None]
    j = np.arange(3 * BLOCK_Q)[None, :]
    band = np.abs(i + BLOCK_Q - j) <= WINDOW
    variants = [band & (j >= BLOCK_Q), band, band & (j < 2 * BLOCK_Q)]
    return jnp.asarray(np.stack([np.where(ok, 0.0, NEG_INF) for ok in variants]).astype(np.float32))


def _ffn_kernel(rn_ref, rnp_ref, rnn_ref, an_ref, anp_ref, ann_ref, x_ref, xp_ref, xn_ref,
                wo_ref, g1_ref, gsc_ref, sh_ref, wu_ref, cw_ref, cb_ref, wd_ref, g2_ref, fg_ref,
                o_ref, mix, x1s, hext, ug, uv, act, acc, nat, *, nt, tm):
    d_ff = wd_ref.shape[0]
    nc = d_ff // FFN_CHUNK
    span = lambda j, branch: slice(branch * d_ff + j * FFN_CHUNK, branch * d_ff + (j + 1) * FFN_CHUNK)
    i = pl.program_id(1)
    halo = SUBLANES_BF16
    body = slice(halo, halo + tm)
    tail = slice(halo + tm, 2 * halo + tm)
    for rows, r_ref, a_ref in ((slice(0, halo), rnp_ref, anp_ref), (body, rn_ref, an_ref),
                               (tail, rnn_ref, ann_ref)):
        mix[rows, :D_RNN] = r_ref[...]
        mix[rows, D_RNN:] = a_ref[...]
    proj = jnp.dot(mix[...], wo_ref[...], preferred_element_type=F32)

    def residual(xin_ref, rows):
        return xin_ref[...] + g1_ref[...] * proj[rows]

    def modulated(x1):
        ms = jnp.mean(x1 * x1, axis=-1, keepdims=True)
        return (x1 * lax.rsqrt(ms + EPS) * gsc_ref[...] + sh_ref[...]).astype(BF16)

    x1 = residual(x_ref, body)
    x1s[...] = x1
    hext[body] = modulated(x1)
    zeros = jnp.zeros((halo, hext.shape[1]), BF16)
    hext[0:halo] = jnp.where(i > 0, modulated(residual(xp_ref, slice(0, halo))), zeros)
    hext[tail] = jnp.where(i < nt - 1, modulated(residual(xn_ref, tail)), zeros)
    base = halo - FFN_CONV_LEFT

    half = tm // 2
    nslab = FFN_CHUNK // LANES

    def up(j, slot):
        he = hext[...]
        g = jnp.dot(he, wu_ref[:, span(j,0)], preferred_element_type=F32)
        v = jnp.dot(he, wu_ref[:, span(j,1)], preferred_element_type=F32)
        for s in range(nslab):
            ug[slot, s] = g[:, LANES * s:LANES * (s + 1)]
            uv[slot, s] = v[:, LANES * s:LANES * (s + 1)]

    def conv(u, slot, cw, cb):
        cols = []
        for s in range(nslab):
            sl = slice(LANES * s, LANES * (s + 1))
            taps = [u[slot, s, pl.ds(base + t, half, stride=2), :] for t in range(FFN_CONV_W + 1)]
            even = cb[:, sl]
            odd = cb[:, sl]
            for t in range(FFN_CONV_W):
                even = even + taps[t] * cw[t:t + 1, sl]
                odd = odd + taps[t + 1] * cw[t:t + 1, sl]
            cols.append(jnp.concatenate([even, odd], axis=0))
        return jnp.concatenate(cols, axis=1)

    def gate(j, slot):
        hg = 0.5 * conv(ug, slot, cw_ref[:, span(j,0)], cb_ref[:, span(j,0)])
        silu = hg * jnp.tanh(hg) + hg
        val = conv(uv, slot, cw_ref[:, span(j,1)], cb_ref[:, span(j,1)])
        act[slot] = (silu * val).astype(BF16)

    def down(j, slot):
        return jnp.dot(act[slot], wd_ref[span(j,0), :], preferred_element_type=F32)

    assert nc % 2 == 1
    up(0, 0)
    acc[...] = jnp.zeros_like(acc)
    for j in range(0, nc - 1, 2):
        up(j + 1, 1)
        gate(j, 0)
        up(j + 2, 0)
        gate(j + 1, 1)
        acc[...] += down(j, 0) + down(j + 1, 1)
    gate(nc - 1, 0)
    mixed = acc[...] + down(nc - 1, 0)
    nlane = mixed.shape[1] // LANES
    for s in range(nlane):
        sl = slice(LANES * s, LANES * (s + 1))
        nat[s, pl.ds(0, half, stride=2), :] = mixed[:half, sl]
        nat[s, pl.ds(1, half, stride=2), :] = mixed[half:, sl]
    ffn = jnp.concatenate([nat[s] for s in range(nlane)], axis=1)
    y = x1s[...] + g2_ref[...] * ffn
    ms = jnp.mean(y * y, axis=-1, keepdims=True)
    o_ref[...] = y * lax.rsqrt(ms + EPS) * fg_ref[...]


def _mixer_out_ffn(rn, an, x, wo, g1, gsc, sh, wu, cw, cb, wd, g2, fg, *, tm):
    b, s, d = x.shape
    nt = s // tm
    assert wd.shape[0] % FFN_CHUNK == 0 and (wd.shape[0] // FFN_CHUNK) % 2 == 1
    halo = SUBLANES_BF16
    per = tm // halo
    nblk = s // halo
    row = lambda bb, i: (bb, i, 0)
    prev = lambda bb, i: (bb, jnp.maximum(i * per - 1, 0), 0)
    nxt = lambda bb, i: (bb, jnp.minimum((i + 1) * per, nblk - 1), 0)
    vec = lambda bb, i: (bb, 0, 0)
    banded = lambda w: [pl.BlockSpec((None, tm, w), row), pl.BlockSpec((None, halo, w), prev),
                        pl.BlockSpec((None, halo, w), nxt)]
    per_sample = pl.BlockSpec((None, 1, d), vec)
    return pl.pallas_call(
        functools.partial(_ffn_kernel, nt=nt, tm=tm),
        grid=(b, nt),
        in_specs=banded(D_RNN) + banded(D_ATTN) + banded(d)
        + [_const_spec(wo.shape), per_sample, per_sample, per_sample,
           _const_spec(wu.shape), _const_spec(cw.shape), _const_spec(cb.shape),
           _const_spec(wd.shape), per_sample, _const_spec(fg.shape)],
        out_specs=pl.BlockSpec((None, tm, d), row),
        out_shape=jax.ShapeDtypeStruct((b, s, d), F32),
        scratch_shapes=[pltpu.VMEM((tm + 2 * halo, D_RNN + D_ATTN), BF16),
                        pltpu.VMEM((tm, d), F32),
                        pltpu.VMEM((tm + 2 * halo, d), BF16),
                        pltpu.VMEM((2, FFN_CHUNK // LANES, tm + 2 * halo, LANES), F32),
                        pltpu.VMEM((2, FFN_CHUNK // LANES, tm + 2 * halo, LANES), F32),
                        pltpu.VMEM((2, tm, FFN_CHUNK), BF16),
                        pltpu.VMEM((tm, d), F32),
                        pltpu.VMEM((d // LANES, tm, LANES), F32)],
        compiler_params=_params("arbitrary", "arbitrary"),
        name="out_proj_conv_ffn",
    )(rn, rn, rn, an, an, an, x, x, x, wo, g1, gsc, sh, wu, cw, cb, wd, g2, fg)


def _rope_tables(s):
    half = HEAD_DIM // 4
    inv = ROPE_THETA ** (-jnp.arange(half, dtype=F32) / half)

    def tables(npos, lanes_first):
        ang = jnp.arange(npos, dtype=jnp.int32).astype(F32)[:, None] * inv[None, :]
        zero = jnp.zeros((npos, 2 * half), F32)
        cos = jnp.concatenate([jnp.cos(ang)] * 2, axis=1)
        sin = jnp.concatenate([-jnp.sin(ang), jnp.sin(ang)], axis=1)
        order = (lambda t: [t, zero]) if lanes_first else (lambda t: [zero, t])
        reps = LANES // HEAD_DIM
        return [jnp.tile(jnp.concatenate(order(t), axis=1), (1, reps)) for t in (cos, sin)]

    cosr, sinr = tables(s // GRID_W, True)
    cosc, sinc = tables(GRID_W, False)
    return cosr, sinr, cosc, sinc


def _gate_weights(w_a, w_i):
    per = MXU_DIM // RNN_BLOCK_W
    eye = jnp.eye(per, dtype=F32)

    def dense(w):
        w = w.reshape(RNN_BLOCKS // per, per, RNN_BLOCK_W, RNN_BLOCK_W)
        m = w[:, :, :, None, :] * eye[None, :, None, :, None]
        return m.reshape(RNN_BLOCKS // per, MXU_DIM, MXU_DIM)

    return (0.5 * jnp.concatenate([dense(w_a), dense(w_i)], axis=2)).astype(BF16)


def _interleave_heads(w, axis):
    shape = w.shape
    w = w.reshape(shape[:axis] + (N_KV_HEADS, Q_PER_KV, HEAD_DIM) + shape[axis + 1:])
    return jnp.swapaxes(w, axis, axis + 1).reshape(shape)


def kernel(x, c, ctx, c_ctx, w_mod, b_mod, norm1_g, w_in, rnn_conv_w, rnn_conv_b, lru_w_a, lru_b_a,
           lru_w_i, lru_b_i, lru_lam, attn_sink, gn_rnn, gn_attn, w_out, norm2_g, w_up, ffn_conv_w,
           ffn_conv_b, w_down, final_g):
    assert w_mod.shape[0] == 1, "one layer: the last layer's context outputs are never consumed"
    b, s, d = x.shape
    lx = ctx.shape[1]

    cond = jnp.concatenate([c, c_ctx[None], jnp.zeros((SUBLANES_F32 - b - 1, d), F32)], axis=0)
    mod = _modulation(cond, w_mod[0], b_mod[0])
    sh1, sc1, g1, sh2, sc2, g2 = [mod[:b, None, d * n:d * (n + 1)] for n in range(6)]
    csh1 = jnp.broadcast_to(mod[b:b + 1, None, :d], (b, 1, d))
    csc1 = jnp.broadcast_to(mod[b:b + 1, None, d:2 * d], (b, 1, d))
    gsc1 = norm1_g[0] * (1.0 + sc1)
    cgsc1 = norm1_g[0] * (1.0 + csc1)
    gsc2 = norm2_g[0] * (1.0 + sc2)

    wi = w_in[0].astype(BF16)
    q0 = 2 * D_RNN
    wi = jnp.concatenate([wi[:, :q0], _interleave_heads(wi[:, q0:q0 + D_ATTN], 1),
                          wi[:, q0 + D_ATTN:]], axis=1)
    wo = w_out[0].astype(BF16)
    wo = jnp.concatenate([wo[:D_RNN], _interleave_heads(wo[D_RNN:], 0)], axis=0)
    gn_a = _interleave_heads(gn_attn[0], 0)[None]
    gn_r = gn_rnn[0][None]
    cw = rnn_conv_w[0]
    cb = rnn_conv_b[0][None]
    sp = (-0.5 * LRU_C) * jax.nn.softplus(-lru_lam[0])

    tables = _rope_tables(s)
    xr, gx, q, kv = _in_proj(x, gsc1, sh1, wi, tables, rope=True, tm=TM_PROJ)
    xrc, _, _, kvx = _in_proj(ctx, cgsc1, csh1, wi, tables, rope=False, tm=lx)

    h0 = jnp.zeros((b, 1, D_RNN), F32)
    y_dir = None
    for dr in range(2):
        wg = _gate_weights(lru_w_a[0, dr], lru_w_i[0, dr])
        gate_args = (cw, cb, wg, 0.5 * lru_b_a[0, dr][None], 0.5 * lru_b_i[0, dr][None], sp[dr][None])
        rev = dr == 1
        _, h_ctx = _rglru(xrc, *gate_args, h0, reverse=rev, tm=lx)
        fin = (y_dir, gx, gn_r) if rev else None
        y_dir, _ = _rglru(xr, *gate_args, h_ctx, reverse=rev, tm=TM_SCAN, final_args=fin)
    rn = y_dir

    an = _attention(q, kv, kvx, _attn_bias(), attn_sink[0] * LOG2E, gn_a)

    return _mixer_out_ffn(rn, an, x, wo, g1, gsc2, sh2, w_up[0].astype(BF16), ffn_conv_w[0],
                          ffn_conv_b[0][None], w_down[0].astype(BF16), g2, final_g[None], tm=TM_FFN)
```

```python
import functools

import jax
import jax.numpy as jnp
import numpy as np
from jax import lax
from jax.experimental import pallas as pl
from jax.experimental.pallas import tpu as pltpu

F32 = jnp.float32
BF16 = jnp.bfloat16

EPS = 1e-6
GRID_W = 64
D_RNN = 512
RNN_BLOCKS = 8
RNN_BLOCK_W = D_RNN // RNN_BLOCKS
RNN_CONV_W = 4
RNN_CONV_LEFT = 2
LRU_C = 8.0
HEAD_DIM = 64
N_Q_HEADS = 8
N_KV_HEADS = 2
Q_PER_KV = N_Q_HEADS // N_KV_HEADS
D_ATTN = N_Q_HEADS * HEAD_DIM
D_KV = N_KV_HEADS * HEAD_DIM
WINDOW = 128
BLOCK_Q = 128
ROPE_THETA = 10000.0
NEG_INF = -1e30
LOG2E = 1.4426950408889634
F32_TINY = 2.0 ** -126
FFN_CONV_W = 3
FFN_CONV_LEFT = 1

LANES = 128
SUBLANES_F32 = 8
SUBLANES_BF16 = 16
MXU_DIM = 256
VMEM_LIMIT = 56 * 1024 * 1024

TM_PROJ = 1024
SUB_PROJ = 512
TM_SCAN = 1024
TM_FFN = 512
FFN_CHUNK = 256
ATTN_QB = 4


def _params(*sem):
    return pltpu.CompilerParams(dimension_semantics=sem, vmem_limit_bytes=VMEM_LIMIT)


def _const_spec(shape):
    zeros = (0,) * len(shape)
    return pl.BlockSpec(shape, lambda *_: zeros, pipeline_mode=pl.Buffered(1))


def _mod_kernel(cond_ref, w_ref, b_ref, o_ref):
    s = cond_ref[...]
    s = s * jax.nn.sigmoid(s)
    o_ref[...] = jnp.dot(s, w_ref[...], preferred_element_type=F32,
                         precision=lax.Precision.HIGHEST) + b_ref[...]


def _modulation(cond, w_mod, b_mod):
    rows, d = cond.shape
    n = w_mod.shape[1]
    tn = 768
    return pl.pallas_call(
        _mod_kernel,
        grid=(n // tn,),
        in_specs=[pl.BlockSpec((rows, d), lambda j: (0, 0)),
                  pl.BlockSpec((d, tn), lambda j: (0, j)),
                  pl.BlockSpec((1, tn), lambda j: (0, j))],
        out_specs=pl.BlockSpec((rows, tn), lambda j: (0, j)),
        out_shape=jax.ShapeDtypeStruct((rows, n), F32),
        compiler_params=_params("arbitrary"),
        name="modulation",
    )(cond, w_mod, b_mod.reshape(1, n))


def _rope_partner(t):
    lane = lax.broadcasted_iota(jnp.int32, t.shape, 1)
    first = (lane % 32) < 16
    return jnp.where(first, pltpu.roll(t, LANES - 16, 1), pltpu.roll(t, 16, 1))


def _inproj_kernel(x_ref, xp_ref, xn_ref, gsc_ref, sh_ref, w_ref, cw_ref, cb_ref,
                   cosr_ref, sinr_ref, cosc_ref, sinc_ref,
                   xc_ref, gx_ref, q_ref, kv_ref, hsc, pslab, *, rope, sub, nt):
    i = pl.program_id(1)
    tm = x_ref.shape[0]
    halo = SUBLANES_BF16
    half = sub // 2
    base = halo - RNN_CONV_LEFT
    q0 = D_RNN
    k0 = q0 + D_ATTN
    v0 = k0 + D_KV
    scale = HEAD_DIM ** -0.5 * LOG2E
    nslab = D_RNN // LANES

    def modulated(x):
        ms = jnp.mean(x * x, axis=-1, keepdims=True)
        return (x * lax.rsqrt(ms + EPS) * gsc_ref[...] + sh_ref[...]).astype(BF16)

    zeros = jnp.zeros((halo, hsc.shape[1]), BF16)
    hsc[0:halo] = jnp.where(i > 0, modulated(xp_ref[...]), zeros)
    hsc[halo + tm:] = jnp.where(i < nt - 1, modulated(xn_ref[...]), zeros)
    for n in range(tm // sub):
        hsc[halo + sub * n:halo + sub * (n + 1)] = modulated(x_ref[sub * n:sub * (n + 1), :])

    cw = cw_ref[...]
    cb = cb_ref[...]
    for n in range(tm // sub):
        rows = slice(sub * n, sub * (n + 1))
        pr = jnp.dot(hsc[sub * n:sub * (n + 1) + 2 * halo], w_ref[:, :D_RNN],
                     preferred_element_type=F32)
        p = jnp.dot(hsc[halo + sub * n:halo + sub * (n + 1)], w_ref[:, D_RNN:],
                    preferred_element_type=F32)
        cols = []
        for s in range(nslab):
            sl = slice(LANES * s, LANES * (s + 1))
            slab = pslab.at[n * nslab + s]
            slab[...] = pr[:, sl]
            taps = [slab[pl.ds(base + j, half, stride=2), :] for j in range(RNN_CONV_W + 1)]
            even = cb[:, sl]
            odd = cb[:, sl]
            for j in range(RNN_CONV_W):
                even = even + taps[j] * cw[j:j + 1, sl]
                odd = odd + taps[j + 1] * cw[j:j + 1, sl]
            cols.append(jnp.concatenate([even, odd], axis=0))
        xc_ref[rows, :] = jnp.concatenate(cols, axis=1)
        gx_ref[rows, :] = jax.nn.gelu(p[:, :D_RNN])
        cols = [p[:, q0 + LANES * c:q0 + LANES * (c + 1)] for c in range(D_ATTN // LANES)]
        k = p[:, k0:v0]
        if rope:
            per = sub // GRID_W
            trows = slice(per * n, per * (n + 1))
            expand = lambda r_ref, c_ref: (r_ref[trows, :][:, None, :]
                                           + c_ref[...][None, :, :]).reshape(sub, LANES)
            cos = expand(cosr_ref, cosc_ref)
            sin = expand(sinr_ref, sinc_ref)
            cols = [t * cos + _rope_partner(t) * sin for t in cols]
            k = k * cos + _rope_partner(k) * sin
        for c, t in enumerate(cols):
            q_ref[rows, LANES * c:LANES * (c + 1)] = (t * scale).astype(BF16)
        kv_ref[rows, :D_KV] = k.astype(BF16)
        kv_ref[rows, D_KV:] = p[:, v0:v0 + D_KV].astype(BF16)


def _in_proj(x, gsc, sh, w_in, cw, cb, tables, *, rope, tm):
    b, s, d = x.shape
    n = w_in.shape[1]
    nt = s // tm
    sub = min(tm, SUB_PROJ)
    halo = SUBLANES_BF16
    per = tm // halo
    row = lambda bb, i: (bb, i, 0)
    prev = lambda bb, i: (bb, jnp.maximum(i * per - 1, 0), 0)
    nxt = lambda bb, i: (bb, jnp.minimum((i + 1) * per, s // halo - 1), 0)
    vec = lambda bb, i: (bb, 0, 0)
    cosr, sinr, cosc, sinc = tables
    if rope:
        assert tm % GRID_W == 0
        rtab = pl.BlockSpec((tm // GRID_W, LANES), lambda bb, i: (i, 0))
    else:
        rtab = _const_spec(cosr.shape)
    outs = [(D_RNN, F32), (D_RNN, F32), (D_ATTN, BF16), (2 * D_KV, BF16)]
    return pl.pallas_call(
        functools.partial(_inproj_kernel, rope=rope, sub=sub, nt=nt),
        grid=(b, nt),
        in_specs=[pl.BlockSpec((None, tm, d), row),
                  pl.BlockSpec((None, halo, d), prev),
                  pl.BlockSpec((None, halo, d), nxt),
                  pl.BlockSpec((None, 1, d), vec),
                  pl.BlockSpec((None, 1, d), vec),
                  _const_spec((d, n)), _const_spec(cw.shape), _const_spec(cb.shape),
                  rtab, rtab, _const_spec(cosc.shape), _const_spec(sinc.shape)],
        out_specs=[pl.BlockSpec((None, tm, w), row) for w, _ in outs],
        out_shape=[jax.ShapeDtypeStruct((b, s, w), dt) for w, dt in outs],
        scratch_shapes=[pltpu.VMEM((tm + 2 * halo, d), BF16),
                        pltpu.VMEM((tm // sub * (D_RNN // LANES), sub + 2 * halo, LANES), F32)],
        compiler_params=_params("arbitrary", "arbitrary"),
        name="in_proj_rope" if rope else "in_proj_ctx",
    )(x, x, x, gsc, sh, w_in, cw, cb, cosr, sinr, cosc, sinc)


def _rglru_kernel(*refs, reverse, final, blk):
    if final:
        (xc_ref, wg_ref, ba_ref, bi_ref, sp_ref, h0_ref, yo_ref, gx_ref, gn_ref,
         y_ref, hfin_ref, a_scr, u_scr, y_scr, h_scr) = refs
    else:
        (xc_ref, wg_ref, ba_ref, bi_ref, sp_ref, h0_ref,
         y_ref, hfin_ref, a_scr, u_scr, y_scr, h_scr) = refs
    i = pl.program_id(0)
    nb, tm, _ = xc_ref.shape

    @pl.when(i == 0)
    def _():
        h_scr[...] = h0_ref[...]

    for bb in range(nb):
        xc = xc_ref[bb]
        xb = xc.astype(BF16)
        for g in range(D_RNN // MXU_DIM):
            sl = slice(MXU_DIM * g, MXU_DIM * (g + 1))
            z = jnp.dot(xb[:, sl], wg_ref[g], preferred_element_type=F32)
            tr = jnp.tanh(z[:, :MXU_DIM] + ba_ref[:, sl])
            gi = 0.5 * jnp.tanh(z[:, MXU_DIM:] + bi_ref[:, sl]) + 0.5
            log_a = (tr + 1.0) * sp_ref[:, sl]
            a = jnp.exp(log_a)
            a_scr[bb, :, sl] = a
            w = -jnp.tanh(log_a) * (1.0 + a * a)
            u_scr[bb, :, sl] = w * lax.rsqrt(jnp.maximum(w, F32_TINY)) * (gi * xc[:, sl])

    rows = SUBLANES_F32
    half = blk // 2
    ngrp = half // rows
    order = range(rows - 1, -1, -1) if reverse else range(rows)
    parity = (1, 0) if reverse else (0, 1)
    spans = range(tm // blk - 1, -1, -1) if reverse else range(tm // blk)

    def group(g, hs, off):
        gg = (ngrp - 1 - g) if reverse else g
        lo = pl.multiple_of(off + gg * rows, rows)
        hi = pl.multiple_of(off + half + gg * rows, rows)
        out = pl.multiple_of(off + gg * 2 * rows, 2 * rows)
        hs = list(hs)
        for j in order:
            for par in parity:
                start = hi if par else lo
                for bb in range(nb):
                    av = a_scr.at[bb, pl.ds(start, rows), :]
                    uv = u_scr.at[bb, pl.ds(start, rows), :]
                    yv = y_scr.at[bb, pl.ds(out, 2 * rows), :]
                    hs[bb] = av[j:j + 1, :] * hs[bb] + uv[j:j + 1, :]
                    yv[2 * j + par:2 * j + par + 1, :] = hs[bb]
        return tuple(hs)

    hs = tuple(h_scr[bb] for bb in range(nb))
    for span in spans:
        hs = lax.fori_loop(0, ngrp, functools.partial(group, off=span * blk), hs)
    for bb in range(nb):
        h_scr[bb] = hs[bb]
        hfin_ref[bb] = hs[bb]

    if final:
        z = gx_ref[...] * (y_scr[...] + yo_ref[...])
        ms = jnp.mean(z * z, axis=-1, keepdims=True)
        y_ref[...] = (z * lax.rsqrt(ms + EPS) * gn_ref[...]).astype(BF16)
    else:
        y_ref[...] = y_scr[...]


def _rglru(xc, wg, ba, bi, sp, h0, *, reverse, tm, blk, final_args=None):
    b, s, d = xc.shape
    nt = s // tm
    final = final_args is not None
    row = (lambda i: (0, nt - 1 - i, 0)) if reverse else (lambda i: (0, i, 0))
    in_specs = [pl.BlockSpec((b, tm, d), row), _const_spec(wg.shape),
                _const_spec(ba.shape), _const_spec(bi.shape), _const_spec(sp.shape),
                _const_spec(h0.shape)]
    args = [xc, wg, ba, bi, sp, h0]
    if final:
        y_other, gx, gn = final_args
        in_specs += [pl.BlockSpec((b, tm, d), row), pl.BlockSpec((b, tm, d), row),
                     _const_spec(gn.shape)]
        args += [y_other, gx, gn]
    return pl.pallas_call(
        functools.partial(_rglru_kernel, reverse=reverse, final=final, blk=blk),
        grid=(nt,),
        in_specs=in_specs,
        out_specs=[pl.BlockSpec((b, tm, d), row), pl.BlockSpec((b, 1, d), lambda i: (0, 0, 0))],
        out_shape=[jax.ShapeDtypeStruct((b, s, d), BF16 if final else F32),
                   jax.ShapeDtypeStruct((b, 1, d), F32)],
        scratch_shapes=[pltpu.VMEM((b, tm, d), F32), pltpu.VMEM((b, tm, d), F32),
                        pltpu.VMEM((b, tm, d), F32), pltpu.VMEM((b, 1, d), F32)],
        compiler_params=_params("arbitrary"),
        name="rglru_" + ("bwd" if reverse else "fwd") + ("_final" if final else ""),
    )(*args)


def _attn_kernel(q_ref, kvp_ref, kvc_ref, kvn_ref, kvx_ref, bias_ref, sink_ref, gn_ref, o_ref,
                 keys, vals, keyx, valx, *, qb, nstep):
    ncol = D_ATTN // LANES
    nwin = 3 * BLOCK_Q
    step = pl.program_id(1)
    band = ((slice(0, BLOCK_Q), kvp_ref), (slice(BLOCK_Q, BLOCK_Q * (qb + 1)), kvc_ref),
            (slice(BLOCK_Q * (qb + 1), BLOCK_Q * (qb + 2)), kvn_ref))
    for rows, kv_ref in band:
        keys[rows, :] = kv_ref[:, :D_KV]
        vals[rows, :D_KV] = kv_ref[:, D_KV:]
    keyx[...] = kvx_ref[:, :D_KV]
    valx[:, :D_KV] = kvx_ref[:, D_KV:]
    vals[:, D_KV:] = jnp.ones((vals.shape[0], LANES), BF16)
    valx[:, D_KV:] = jnp.ones((valx.shape[0], LANES), BF16)

    low = lax.broadcasted_iota(jnp.int32, (BLOCK_Q, LANES), 1) < HEAD_DIM
    first = lax.broadcasted_iota(jnp.int32, (2 * BLOCK_Q, 1), 0) < BLOCK_Q
    zero = jnp.zeros((BLOCK_Q, LANES), BF16)
    nt_dims = (((1,), (1,)), ((), ()))
    for x in range(qb):
        rows = slice(BLOCK_Q * x, BLOCK_Q * (x + 1))
        win = slice(BLOCK_Q * x, BLOCK_Q * x + nwin)
        variant = 1
        if x == 0:
            variant = jnp.where(step == 0, 0, variant)
        if x == qb - 1:
            variant = jnp.where(step == nstep - 1, 2, variant)
        bias = bias_ref[variant]
        outs = []
        for c in range(ncol):
            t = q_ref[rows, LANES * c:LANES * (c + 1)]
            qs = jnp.concatenate([jnp.where(low, t, zero), jnp.where(low, zero, t)], axis=0)
            s_loc = lax.dot_general(qs, keys[win, :], nt_dims, preferred_element_type=F32)
            s_ctx = lax.dot_general(qs, keyx[...], nt_dims, preferred_element_type=F32)
            s_loc = (s_loc.reshape(2, BLOCK_Q, nwin) + bias[None]).reshape(2 * BLOCK_Q, nwin)
            sink = jnp.where(first, sink_ref[c], sink_ref[ncol + c])
            m = jnp.maximum(jnp.maximum(jnp.max(s_loc, axis=-1, keepdims=True),
                                        jnp.max(s_ctx, axis=-1, keepdims=True)), sink)
            e_loc = jnp.exp2(s_loc - m).astype(BF16)
            e_ctx = jnp.exp2(s_ctx - m).astype(BF16)
            pv = (jnp.dot(e_loc, vals[win, :], preferred_element_type=F32)
                  + jnp.dot(e_ctx, valx[...], preferred_element_type=F32))
            o = pv[:, :D_KV] / (pv[:, D_KV:] + jnp.exp2(sink - m))
            outs.append(jnp.where(low, o[:BLOCK_Q], o[BLOCK_Q:]))
        ms = sum(jnp.sum(t * t, axis=-1, keepdims=True) for t in outs) * (1.0 / D_ATTN)
        inv = lax.rsqrt(ms + EPS)
        for c, t in enumerate(outs):
            sl = slice(LANES * c, LANES * (c + 1))
            o_ref[rows, sl] = (t * inv * gn_ref[:, sl]).astype(BF16)


def _attention(q, kv, kvx, bias, sink, gn):
    b, s, _ = q.shape
    qb = ATTN_QB
    nblk = s // BLOCK_Q
    nstep = nblk // qb
    lx = kvx.shape[1]
    cur = lambda bb, n: (bb, n, 0)
    prev = lambda bb, n: (bb, jnp.maximum(n * qb - 1, 0), 0)
    nxt = lambda bb, n: (bb, jnp.minimum((n + 1) * qb, nblk - 1), 0)
    halo = lambda im: pl.BlockSpec((None, BLOCK_Q, 2 * D_KV), im)
    return pl.pallas_call(
        functools.partial(_attn_kernel, qb=qb, nstep=nstep),
        grid=(b, nstep),
        in_specs=[pl.BlockSpec((None, qb * BLOCK_Q, D_ATTN), cur),
                  halo(prev), pl.BlockSpec((None, qb * BLOCK_Q, 2 * D_KV), cur), halo(nxt),
                  pl.BlockSpec((None, lx, 2 * D_KV), lambda bb, n: (bb, 0, 0)),
                  _const_spec(bias.shape), pl.BlockSpec(memory_space=pltpu.SMEM),
                  _const_spec(gn.shape)],
        out_specs=pl.BlockSpec((None, qb * BLOCK_Q, D_ATTN), cur),
        out_shape=jax.ShapeDtypeStruct((b, s, D_ATTN), BF16),
        scratch_shapes=[pltpu.VMEM(((qb + 2) * BLOCK_Q, D_KV), BF16),
                        pltpu.VMEM(((qb + 2) * BLOCK_Q, D_KV + LANES), BF16),
                        pltpu.VMEM((lx, D_KV), BF16),
                        pltpu.VMEM((lx, D_KV + LANES), BF16)],
        compiler_params=_params("arbitrary", "arbitrary"),
        name="attention",
    )(q, kv, kv, kv, kvx, bias, sink, gn)


def _attn_bias():
    i = np.arange(BLOCK_Q)[:, None]
    j = np.arange(3 * BLOCK_Q)[None, :]
    band = np.abs(i + BLOCK_Q - j) <= WINDOW
    variants = [band & (j >= BLOCK_Q), band, band & (j < 2 * BLOCK_Q)]
    return jnp.asarray(np.stack([np.where(ok, 0.0, NEG_INF) for ok in variants]).astype(np.float32))


def _ffn_kernel(rn_ref, rnp_ref, rnn_ref, an_ref, anp_ref, ann_ref, x_ref, xp_ref, xn_ref,
                wo_ref, g1_ref, gsc_ref, sh_ref, wu_ref, cw_ref, cb_ref, wd_ref, g2_ref, fg_ref,
                o_ref, mix, x1s, hext, ug, uv, act, acc, nat, *, nt, tm):
    d_ff = wd_ref.shape[0]
    nc = d_ff // FFN_CHUNK
    span = lambda j, branch: slice(branch * d_ff + j * FFN_CHUNK, branch * d_ff + (j + 1) * FFN_CHUNK)
    i = pl.program_id(1)
    halo = SUBLANES_BF16
    body = slice(halo, halo + tm)
    tail = slice(halo + tm, 2 * halo + tm)
    for rows, r_ref, a_ref in ((slice(0, halo), rnp_ref, anp_ref), (body, rn_ref, an_ref),
                               (tail, rnn_ref, ann_ref)):
        mix[rows, :D_RNN] = r_ref[...]
        mix[rows, D_RNN:] = a_ref[...]
    proj = jnp.dot(mix[...], wo_ref[...], preferred_element_type=F32)

    def residual(xin_ref, rows):
        return xin_ref[...] + g1_ref[...] * proj[rows]

    def modulated(x1):
        ms = jnp.mean(x1 * x1, axis=-1, keepdims=True)
        return (x1 * lax.rsqrt(ms + EPS) * gsc_ref[...] + sh_ref[...]).astype(BF16)

    x1 = residual(x_ref, body)
    x1s[...] = x1
    hext[body] = modulated(x1)
    zeros = jnp.zeros((halo, hext.shape[1]), BF16)
    hext[0:halo] = jnp.where(i > 0, modulated(residual(xp_ref, slice(0, halo))), zeros)
    hext[tail] = jnp.where(i < nt - 1, modulated(residual(xn_ref, tail)), zeros)
    base = halo - FFN_CONV_LEFT

    half = tm // 2
    nslab = FFN_CHUNK // LANES

    def up(j, slot):
        he = hext[...]
        g = jnp.dot(he, wu_ref[:, span(j,0)], preferred_element_type=F32)
        v = jnp.dot(he, wu_ref[:, span(j,1)], preferred_element_type=F32)
        for s in range(nslab):
            ug[slot, s] = g[:, LANES * s:LANES * (s + 1)]
            uv[slot, s] = v[:, LANES * s:LANES * (s + 1)]

    def conv(u, slot, cw, cb):
        cols = []
        for s in range(nslab):
            sl = slice(LANES * s, LANES * (s + 1))
            taps = [u[slot, s, pl.ds(base + t, half, stride=2), :] for t in range(FFN_CONV_W + 1)]
            even = cb[:, sl]
            odd = cb[:, sl]
            for t in range(FFN_CONV_W):
                even = even + taps[t] * cw[t:t + 1, sl]
                odd = odd + taps[t + 1] * cw[t:t + 1, sl]
            cols.append(jnp.concatenate([even, odd], axis=0))
        return jnp.concatenate(cols, axis=1)

    def gate(j, slot):
        hg = 0.5 * conv(ug, slot, cw_ref[:, span(j,0)], cb_ref[:, span(j,0)])
        silu = hg * jnp.tanh(hg) + hg
        val = conv(uv, slot, cw_ref[:, span(j,1)], cb_ref[:, span(j,1)])
        act[slot] = (silu * val).astype(BF16)

    def down(j, slot):
        return jnp.dot(act[slot], wd_ref[span(j,0), :], preferred_element_type=F32)

    assert nc % 2 == 1
    up(0, 0)
    acc[...] = jnp.zeros_like(acc)
    for j in range(0, nc - 1, 2):
        up(j + 1, 1)
        gate(j, 0)
        up(j + 2, 0)
        gate(j + 1, 1)
        acc[...] += down(j, 0) + down(j + 1, 1)
    gate(nc - 1, 0)
    mixed = acc[...] + down(nc - 1, 0)
    nlane = mixed.shape[1] // LANES
    for s in range(nlane):
        sl = slice(LANES * s, LANES * (s + 1))
        nat[s, pl.ds(0, half, stride=2), :] = mixed[:half, sl]
        nat[s, pl.ds(1, half, stride=2), :] = mixed[half:, sl]
    ffn = jnp.concatenate([nat[s] for s in range(nlane)], axis=1)
    y = x1s[...] + g2_ref[...] * ffn
    ms = jnp.mean(y * y, axis=-1, keepdims=True)
    o_ref[...] = y * lax.rsqrt(ms + EPS) * fg_ref[...]


def _mixer_out_ffn(rn, an, x, wo, g1, gsc, sh, wu, cw, cb, wd, g2, fg, *, tm):
    b, s, d = x.shape
    nt = s // tm
    assert wd.shape[0] % FFN_CHUNK == 0 and (wd.shape[0] // FFN_CHUNK) % 2 == 1
    halo = SUBLANES_BF16
    per = tm // halo
    nblk = s // halo
    row = lambda bb, i: (bb, i, 0)
    prev = lambda bb, i: (bb, jnp.maximum(i * per - 1, 0), 0)
    nxt = lambda bb, i: (bb, jnp.minimum((i + 1) * per, nblk - 1), 0)
    vec = lambda bb, i: (bb, 0, 0)
    banded = lambda w: [pl.BlockSpec((None, tm, w), row), pl.BlockSpec((None, halo, w), prev),
                        pl.BlockSpec((None, halo, w), nxt)]
    per_sample = pl.BlockSpec((None, 1, d), vec)
    return pl.pallas_call(
        functools.partial(_ffn_kernel, nt=nt, tm=tm),
        grid=(b, nt),
        in_specs=banded(D_RNN) + banded(D_ATTN) + banded(d)
        + [_const_spec(wo.shape), per_sample, per_sample, per_sample,
           _const_spec(wu.shape), _const_spec(cw.shape), _const_spec(cb.shape),
           _const_spec(wd.shape), per_sample, _const_spec(fg.shape)],
        out_specs=pl.BlockSpec((None, tm, d), row),
        out_shape=jax.ShapeDtypeStruct((b, s, d), F32),
        scratch_shapes=[pltpu.VMEM((tm + 2 * halo, D_RNN + D_ATTN), BF16),
                        pltpu.VMEM((tm, d), F32),
                        pltpu.VMEM((tm + 2 * halo, d), BF16),
                        pltpu.VMEM((2, FFN_CHUNK // LANES, tm + 2 * halo, LANES), F32),
                        pltpu.VMEM((2, FFN_CHUNK // LANES, tm + 2 * halo, LANES), F32),
                        pltpu.VMEM((2, tm, FFN_CHUNK), BF16),
                        pltpu.VMEM((tm, d), F32),
                        pltpu.VMEM((d // LANES, tm, LANES), F32)],
        compiler_params=_params("arbitrary", "arbitrary"),
        name="out_proj_conv_ffn",
    )(rn, rn, rn, an, an, an, x, x, x, wo, g1, gsc, sh, wu, cw, cb, wd, g2, fg)


def _rope_tables(s):
    half = HEAD_DIM // 4
    inv = ROPE_THETA ** (-jnp.arange(half, dtype=F32) / half)

    def tables(npos, lanes_first):
        ang = jnp.arange(npos, dtype=jnp.int32).astype(F32)[:, None] * inv[None, :]
        zero = jnp.zeros((npos, 2 * half), F32)
        cos = jnp.concatenate([jnp.cos(ang)] * 2, axis=1)
        sin = jnp.concatenate([-jnp.sin(ang), jnp.sin(ang)], axis=1)
        order = (lambda t: [t, zero]) if lanes_first else (lambda t: [zero, t])
        reps = LANES // HEAD_DIM
        return [jnp.tile(jnp.concatenate(order(t), axis=1), (1, reps)) for t in (cos, sin)]

    cosr, sinr = tables(s // GRID_W, True)
    cosc, sinc = tables(GRID_W, False)
    return cosr, sinr, cosc, sinc


def _gate_weights(w_a, w_i):
    per = MXU_DIM // RNN_BLOCK_W
    eye = jnp.eye(per, dtype=F32)

    def dense(w):
        w = w.reshape(RNN_BLOCKS // per, per, RNN_BLOCK_W, RNN_BLOCK_W)
        m = w[:, :, :, None, :] * eye[None, :, None, :, None]
        return m.reshape(RNN_BLOCKS // per, MXU_DIM, MXU_DIM)

    return (0.5 * jnp.concatenate([dense(w_a), dense(w_i)], axis=2)).astype(BF16)


def _interleave_heads(w, axis):
    shape = w.shape
    w = w.reshape(shape[:axis] + (N_KV_HEADS, Q_PER_KV, HEAD_DIM) + shape[axis + 1:])
    return jnp.swapaxes(w, axis, axis + 1).reshape(shape)


def kernel(x, c, ctx, c_ctx, w_mod, b_mod, norm1_g, w_in, rnn_conv_w, rnn_conv_b, lru_w_a, lru_b_a,
           lru_w_i, lru_b_i, lru_lam, attn_sink, gn_rnn, gn_attn, w_out, norm2_g, w_up, ffn_conv_w,
           ffn_conv_b, w_down, final_g):
    assert w_mod.shape[0] == 1, "one layer: the last layer's context outputs are never consumed"
    b, s, d = x.shape
    lx = ctx.shape[1]

    cond = jnp.concatenate([c, c_ctx[None], jnp.zeros((SUBLANES_F32 - b - 1, d), F32)], axis=0)
    mod = _modulation(cond, w_mod[0], b_mod[0])
    sh1, sc1, g1, sh2, sc2, g2 = [mod[:b, None, d * n:d * (n + 1)] for n in range(6)]
    csh1 = jnp.broadcast_to(mod[b:b + 1, None, :d], (b, 1, d))
    csc1 = jnp.broadcast_to(mod[b:b + 1, None, d:2 * d], (b, 1, d))
    gsc1 = norm1_g[0] * (1.0 + sc1)
    cgsc1 = norm1_g[0] * (1.0 + csc1)
    gsc2 = norm2_g[0] * (1.0 + sc2)

    wi = w_in[0].astype(BF16)
    q0 = 2 * D_RNN
    wi = jnp.concatenate([wi[:, :q0], _interleave_heads(wi[:, q0:q0 + D_ATTN], 1),
                          wi[:, q0 + D_ATTN:]], axis=1)
    wo = w_out[0].astype(BF16)
    wo = jnp.concatenate([wo[:D_RNN], _interleave_heads(wo[D_RNN:], 0)], axis=0)
    gn_a = _interleave_heads(gn_attn[0], 0)[None]
    gn_r = gn_rnn[0][None]
    cw = rnn_conv_w[0]
    cb = rnn_conv_b[0][None]
    sp = (-0.5 * LRU_C) * jax.nn.softplus(-lru_lam[0])

    tables = _rope_tables(s)
    xc, gx, q, kv = _in_proj(x, gsc1, sh1, wi, cw, cb, tables, rope=True, tm=TM_PROJ)
    xcc, _, _, kvx = _in_proj(ctx, cgsc1, csh1, wi, cw, cb, tables, rope=False, tm=lx)

    h0 = jnp.zeros((b, 1, D_RNN), F32)
    y_dir = None
    for dr in range(2):
        wg = _gate_weights(lru_w_a[0, dr], lru_w_i[0, dr])
        gate_args = (wg, 0.5 * lru_b_a[0, dr][None], 0.5 * lru_b_i[0, dr][None], sp[dr][None])
        rev = dr == 1
        _, h_ctx = _rglru(xcc, *gate_args, h0, reverse=rev, tm=lx, blk=min(lx, SUB_PROJ))
        fin = (y_dir, gx, gn_r) if rev else None
        y_dir, _ = _rglru(xc, *gate_args, h_ctx, reverse=rev, tm=TM_SCAN, blk=SUB_PROJ,
                          final_args=fin)
    rn = y_dir

    an = _attention(q, kv, kvx, _attn_bias(), attn_sink[0] * LOG2E, gn_a)

    return _mixer_out_ffn(rn, an, x, wo, g1, gsc2, sh2, w_up[0].astype(BF16), ffn_conv_w[0],
                          ffn_conv_b[0][None], w_down[0].astype(BF16), g2, final_g[None], tm=TM_FFN)
```

```python
import functools

import jax
import jax.numpy as jnp
import numpy as np
from jax import lax
from jax.experimental import pallas as pl
from jax.experimental.pallas import tpu as pltpu

F32 = jnp.float32
BF16 = jnp.bfloat16

EPS = 1e-6
GRID_W = 64
D_RNN = 512
RNN_BLOCKS = 8
RNN_BLOCK_W = D_RNN // RNN_BLOCKS
RNN_CONV_W = 4
RNN_CONV_LEFT = 2
LRU_C = 8.0
HEAD_DIM = 64
N_Q_HEADS = 8
N_KV_HEADS = 2
Q_PER_KV = N_Q_HEADS // N_KV_HEADS
D_ATTN = N_Q_HEADS * HEAD_DIM
D_KV = N_KV_HEADS * HEAD_DIM
WINDOW = 128
BLOCK_Q = 128
ROPE_THETA = 10000.0
NEG_INF = -1e30
LOG2E = 1.4426950408889634
F32_TINY = 2.0 ** -126
FFN_CONV_W = 3
FFN_CONV_LEFT = 1

LANES = 128
SUBLANES_F32 = 8
SUBLANES_BF16 = 16
MXU_DIM = 256
VMEM_LIMIT = 56 * 1024 * 1024

TM_PROJ = 1024
SUB_PROJ = 512
TM_SCAN = 1024
TM_FFN = 512
FFN_CHUNK = 256
ATTN_QB = 4
ATTN_LOOKAHEAD = 1


def _params(*sem):
    return pltpu.CompilerParams(dimension_semantics=sem, vmem_limit_bytes=VMEM_LIMIT)


def _const_spec(shape):
    zeros = (0,) * len(shape)
    return pl.BlockSpec(shape, lambda *_: zeros, pipeline_mode=pl.Buffered(1))


def _mod_kernel(cond_ref, w_ref, b_ref, o_ref):
    s = cond_ref[...]
    s = s * jax.nn.sigmoid(s)
    o_ref[...] = jnp.dot(s, w_ref[...], preferred_element_type=F32,
                         precision=lax.Precision.HIGHEST) + b_ref[...]


def _modulation(cond, w_mod, b_mod):
    rows, d = cond.shape
    n = w_mod.shape[1]
    tn = 768
    return pl.pallas_call(
        _mod_kernel,
        grid=(n // tn,),
        in_specs=[pl.BlockSpec((rows, d), lambda j: (0, 0)),
                  pl.BlockSpec((d, tn), lambda j: (0, j)),
                  pl.BlockSpec((1, tn), lambda j: (0, j))],
        out_specs=pl.BlockSpec((rows, tn), lambda j: (0, j)),
        out_shape=jax.ShapeDtypeStruct((rows, n), F32),
        compiler_params=_params("arbitrary"),
        name="modulation",
    )(cond, w_mod, b_mod.reshape(1, n))


def _rope_partner(t):
    lane = lax.broadcasted_iota(jnp.int32, t.shape, 1)
    first = (lane % 32) < 16
    return jnp.where(first, pltpu.roll(t, LANES - 16, 1), pltpu.roll(t, 16, 1))


def _inproj_kernel(x_ref, xp_ref, xn_ref, gsc_ref, sh_ref, w_ref, cw_ref, cb_ref,
                   cosr_ref, sinr_ref, cosc_ref, sinc_ref,
                   xc_ref, gx_ref, q_ref, kv_ref, hsc, pslab, *, rope, sub, nt):
    i = pl.program_id(1)
    tm = x_ref.shape[0]
    halo = SUBLANES_BF16
    half = sub // 2
    base = halo - RNN_CONV_LEFT
    q0 = D_RNN
    k0 = q0 + D_ATTN
    v0 = k0 + D_KV
    scale = HEAD_DIM ** -0.5 * LOG2E
    nslab = D_RNN // LANES

    def modulated(x):
        ms = jnp.mean(x * x, axis=-1, keepdims=True)
        return (x * lax.rsqrt(ms + EPS) * gsc_ref[...] + sh_ref[...]).astype(BF16)

    zeros = jnp.zeros((halo, hsc.shape[1]), BF16)
    hsc[0:halo] = jnp.where(i > 0, modulated(xp_ref[...]), zeros)
    hsc[halo + tm:] = jnp.where(i < nt - 1, modulated(xn_ref[...]), zeros)
    for n in range(tm // sub):
        hsc[halo + sub * n:halo + sub * (n + 1)] = modulated(x_ref[sub * n:sub * (n + 1), :])

    cw = cw_ref[...]
    cb = cb_ref[...]
    def project(n):
        pr = jnp.dot(hsc[sub * n:sub * (n + 1) + 2 * halo], w_ref[:, :D_RNN],
                     preferred_element_type=F32)
        p = jnp.dot(hsc[halo + sub * n:halo + sub * (n + 1)], w_ref[:, D_RNN:],
                    preferred_element_type=F32)
        return pr, p

    def epilogue(n, pr, p):
        rows = slice(sub * n, sub * (n + 1))
        cols = []
        for s in range(nslab):
            sl = slice(LANES * s, LANES * (s + 1))
            slab = pslab.at[n * nslab + s]
            slab[...] = pr[:, sl]
            taps = [slab[pl.ds(base + j, half, stride=2), :] for j in range(RNN_CONV_W + 1)]
            even = cb[:, sl]
            odd = cb[:, sl]
            for j in range(RNN_CONV_W):
                even = even + taps[j] * cw[j:j + 1, sl]
                odd = odd + taps[j + 1] * cw[j:j + 1, sl]
            cols.append(jnp.concatenate([even, odd], axis=0))
        xc_ref[rows, :] = jnp.concatenate(cols, axis=1)
        gx_ref[rows, :] = jax.nn.gelu(p[:, :D_RNN])
        cols = [p[:, q0 + LANES * c:q0 + LANES * (c + 1)] for c in range(D_ATTN // LANES)]
        k = p[:, k0:v0]
        if rope:
            per = sub // GRID_W
            trows = slice(per * n, per * (n + 1))
            expand = lambda r_ref, c_ref: (r_ref[trows, :][:, None, :]
                                           + c_ref[...][None, :, :]).reshape(sub, LANES)
            cos = expand(cosr_ref, cosc_ref)
            sin = expand(sinr_ref, sinc_ref)
            cols = [t * cos + _rope_partner(t) * sin for t in cols]
            k = k * cos + _rope_partner(k) * sin
        for c, t in enumerate(cols):
            q_ref[rows, LANES * c:LANES * (c + 1)] = (t * scale).astype(BF16)
        kv_ref[rows, :D_KV] = k.astype(BF16)
        kv_ref[rows, D_KV:] = p[:, v0:v0 + D_KV].astype(BF16)

    nsub = tm // sub
    ready = project(0)
    for n in range(nsub):
        following = project(n + 1) if n + 1 < nsub else None
        epilogue(n, *ready)
        ready = following


def _in_proj(x, gsc, sh, w_in, cw, cb, tables, *, rope, tm):
    b, s, d = x.shape
    n = w_in.shape[1]
    nt = s // tm
    sub = min(tm, SUB_PROJ)
    halo = SUBLANES_BF16
    per = tm // halo
    row = lambda bb, i: (bb, i, 0)
    prev = lambda bb, i: (bb, jnp.maximum(i * per - 1, 0), 0)
    nxt = lambda bb, i: (bb, jnp.minimum((i + 1) * per, s // halo - 1), 0)
    vec = lambda bb, i: (bb, 0, 0)
    cosr, sinr, cosc, sinc = tables
    if rope:
        assert tm % GRID_W == 0
        rtab = pl.BlockSpec((tm // GRID_W, LANES), lambda bb, i: (i, 0))
    else:
        rtab = _const_spec(cosr.shape)
    outs = [(D_RNN, F32), (D_RNN, F32), (D_ATTN, BF16), (2 * D_KV, BF16)]
    return pl.pallas_call(
        functools.partial(_inproj_kernel, rope=rope, sub=sub, nt=nt),
        grid=(b, nt),
        in_specs=[pl.BlockSpec((None, tm, d), row),
                  pl.BlockSpec((None, halo, d), prev),
                  pl.BlockSpec((None, halo, d), nxt),
                  pl.BlockSpec((None, 1, d), vec),
                  pl.BlockSpec((None, 1, d), vec),
                  _const_spec((d, n)), _const_spec(cw.shape), _const_spec(cb.shape),
                  rtab, rtab, _const_spec(cosc.shape), _const_spec(sinc.shape)],
        out_specs=[pl.BlockSpec((None, tm, w), row) for w, _ in outs],
        out_shape=[jax.ShapeDtypeStruct((b, s, w), dt) for w, dt in outs],
        scratch_shapes=[pltpu.VMEM((tm + 2 * halo, d), BF16),
                        pltpu.VMEM((tm // sub * (D_RNN // LANES), sub + 2 * halo, LANES), F32)],
        compiler_params=_params("arbitrary", "arbitrary"),
        name="in_proj_rope" if rope else "in_proj_ctx",
    )(x, x, x, gsc, sh, w_in, cw, cb, cosr, sinr, cosc, sinc)


def _rglru_kernel(*refs, reverse, final, blk):
    if final:
        (xc_ref, wg_ref, ba_ref, bi_ref, sp_ref, h0_ref, yo_ref, gx_ref, gn_ref,
         y_ref, hfin_ref, a_scr, u_scr, y_scr, h_scr) = refs
    else:
        (xc_ref, wg_ref, ba_ref, bi_ref, sp_ref, h0_ref,
         y_ref, hfin_ref, a_scr, u_scr, y_scr, h_scr) = refs
    i = pl.program_id(0)
    nb, tm, _ = xc_ref.shape

    @pl.when(i == 0)
    def _():
        h_scr[...] = h0_ref[...]

    for bb in range(nb):
        xc = xc_ref[bb]
        xb = xc.astype(BF16)
        for g in range(D_RNN // MXU_DIM):
            sl = slice(MXU_DIM * g, MXU_DIM * (g + 1))
            z = jnp.dot(xb[:, sl], wg_ref[g], preferred_element_type=F32)
            tr = jnp.tanh(z[:, :MXU_DIM] + ba_ref[:, sl])
            gi = 0.5 * jnp.tanh(z[:, MXU_DIM:] + bi_ref[:, sl]) + 0.5
            log_a = (tr + 1.0) * sp_ref[:, sl]
            a = jnp.exp(log_a)
            a_scr[bb, :, sl] = a
            w = -jnp.tanh(log_a) * (1.0 + a * a)
            u_scr[bb, :, sl] = w * lax.rsqrt(jnp.maximum(w, F32_TINY)) * (gi * xc[:, sl])

    rows = SUBLANES_F32
    half = blk // 2
    ngrp = half // rows
    order = range(rows - 1, -1, -1) if reverse else range(rows)
    parity = (1, 0) if reverse else (0, 1)
    spans = range(tm // blk - 1, -1, -1) if reverse else range(tm // blk)

    def group(g, hs, off):
        gg = (ngrp - 1 - g) if reverse else g
        lo = pl.multiple_of(off + gg * rows, rows)
        hi = pl.multiple_of(off + half + gg * rows, rows)
        out = pl.multiple_of(off + gg * 2 * rows, 2 * rows)
        hs = list(hs)
        for j in order:
            for par in parity:
                start = hi if par else lo
                for bb in range(nb):
                    av = a_scr.at[bb, pl.ds(start, rows), :]
                    uv = u_scr.at[bb, pl.ds(start, rows), :]
                    yv = y_scr.at[bb, pl.ds(out, 2 * rows), :]
                    hs[bb] = av[j:j + 1, :] * hs[bb] + uv[j:j + 1, :]
                    yv[2 * j + par:2 * j + par + 1, :] = hs[bb]
        return tuple(hs)

    hs = tuple(h_scr[bb] for bb in range(nb))
    for span in spans:
        hs = lax.fori_loop(0, ngrp, functools.partial(group, off=span * blk), hs)
    for bb in range(nb):
        h_scr[bb] = hs[bb]
        hfin_ref[bb] = hs[bb]

    if final:
        z = gx_ref[...] * (y_scr[...] + yo_ref[...])
        ms = jnp.mean(z * z, axis=-1, keepdims=True)
        y_ref[...] = (z * lax.rsqrt(ms + EPS) * gn_ref[...]).astype(BF16)
    else:
        y_ref[...] = y_scr[...]


def _rglru(xc, wg, ba, bi, sp, h0, *, reverse, tm, blk, final_args=None):
    b, s, d = xc.shape
    nt = s // tm
    final = final_args is not None
    row = (lambda i: (0, nt - 1 - i, 0)) if reverse else (lambda i: (0, i, 0))
    in_specs = [pl.BlockSpec((b, tm, d), row), _const_spec(wg.shape),
                _const_spec(ba.shape), _const_spec(bi.shape), _const_spec(sp.shape),
                _const_spec(h0.shape)]
    args = [xc, wg, ba, bi, sp, h0]
    if final:
        y_other, gx, gn = final_args
        in_specs += [pl.BlockSpec((b, tm, d), row), pl.BlockSpec((b, tm, d), row),
                     _const_spec(gn.shape)]
        args += [y_other, gx, gn]
    return pl.pallas_call(
        functools.partial(_rglru_kernel, reverse=reverse, final=final, blk=blk),
        grid=(nt,),
        in_specs=in_specs,
        out_specs=[pl.BlockSpec((b, tm, d), row), pl.BlockSpec((b, 1, d), lambda i: (0, 0, 0))],
        out_shape=[jax.ShapeDtypeStruct((b, s, d), BF16 if final else F32),
                   jax.ShapeDtypeStruct((b, 1, d), F32)],
        scratch_shapes=[pltpu.VMEM((b, tm, d), F32), pltpu.VMEM((b, tm, d), F32),
                        pltpu.VMEM((b, tm, d), F32), pltpu.VMEM((b, 1, d), F32)],
        compiler_params=_params("arbitrary"),
        name="rglru_" + ("bwd" if reverse else "fwd") + ("_final" if final else ""),
    )(*args)


def _attn_kernel(q_ref, kvp_ref, kvc_ref, kvn_ref, kvx_ref, bias_ref, sink_ref, gn_ref, o_ref,
                 keys, vals, keyx, valx, *, qb, nstep):
    ncol = D_ATTN // LANES
    nwin = 3 * BLOCK_Q
    step = pl.program_id(1)
    band = ((slice(0, BLOCK_Q), kvp_ref), (slice(BLOCK_Q, BLOCK_Q * (qb + 1)), kvc_ref),
            (slice(BLOCK_Q * (qb + 1), BLOCK_Q * (qb + 2)), kvn_ref))
    for rows, kv_ref in band:
        keys[rows, :] = kv_ref[:, :D_KV]
        vals[rows, :D_KV] = kv_ref[:, D_KV:]
    keyx[...] = kvx_ref[:, :D_KV]
    valx[:, :D_KV] = kvx_ref[:, D_KV:]
    vals[:, D_KV:] = jnp.ones((vals.shape[0], LANES), BF16)
    valx[:, D_KV:] = jnp.ones((valx.shape[0], LANES), BF16)

    low = lax.broadcasted_iota(jnp.int32, (BLOCK_Q, LANES), 1) < HEAD_DIM
    first = lax.broadcasted_iota(jnp.int32, (2 * BLOCK_Q, 1), 0) < BLOCK_Q
    zero = jnp.zeros((BLOCK_Q, LANES), BF16)
    nt_dims = (((1,), (1,)), ((), ()))
    def block_bias(x):
        variant = 1
        if x == 0:
            variant = jnp.where(step == 0, 0, variant)
        if x == qb - 1:
            variant = jnp.where(step == nstep - 1, 2, variant)
        return bias_ref[variant]

    def scores(x, c, bias):
        win = slice(BLOCK_Q * x, BLOCK_Q * x + nwin)
        t = q_ref[BLOCK_Q * x:BLOCK_Q * (x + 1), LANES * c:LANES * (c + 1)]
        qs = jnp.concatenate([jnp.where(low, t, zero), jnp.where(low, zero, t)], axis=0)
        s_loc = lax.dot_general(qs, keys[win, :], nt_dims, preferred_element_type=F32)
        s_ctx = lax.dot_general(qs, keyx[...], nt_dims, preferred_element_type=F32)
        s_loc = (s_loc.reshape(2, BLOCK_Q, nwin) + bias[None]).reshape(2 * BLOCK_Q, nwin)
        sink = jnp.where(first, sink_ref[c], sink_ref[ncol + c])
        m = jnp.maximum(jnp.maximum(jnp.max(s_loc, axis=-1, keepdims=True),
                                    jnp.max(s_ctx, axis=-1, keepdims=True)), sink)
        e_loc = jnp.exp2(s_loc - m).astype(BF16)
        e_ctx = jnp.exp2(s_ctx - m).astype(BF16)
        return x, e_loc, e_ctx, jnp.exp2(sink - m)

    def weighted(x, e_loc, e_ctx, e_sink):
        win = slice(BLOCK_Q * x, BLOCK_Q * x + nwin)
        pv = (jnp.dot(e_loc, vals[win, :], preferred_element_type=F32)
              + jnp.dot(e_ctx, valx[...], preferred_element_type=F32))
        o = pv[:, :D_KV] / (pv[:, D_KV:] + e_sink)
        return jnp.where(low, o[:BLOCK_Q], o[BLOCK_Q:])

    def finish(x, outs):
        ms = sum(jnp.sum(t * t, axis=-1, keepdims=True) for t in outs) * (1.0 / D_ATTN)
        inv = lax.rsqrt(ms + EPS)
        for c, t in enumerate(outs):
            sl = slice(LANES * c, LANES * (c + 1))
            o_ref[BLOCK_Q * x:BLOCK_Q * (x + 1), sl] = (t * inv * gn_ref[:, sl]).astype(BF16)

    pending = []
    outs = [[] for _ in range(qb)]

    def retire():
        chain = pending.pop(0)
        x = chain[0]
        outs[x].append(weighted(*chain))
        if len(outs[x]) == ncol:
            finish(x, outs[x])

    for x in range(qb):
        bias = block_bias(x)
        for c in range(ncol):
            pending.append(scores(x, c, bias))
            if len(pending) > ATTN_LOOKAHEAD:
                retire()
    while pending:
        retire()


def _attention(q, kv, kvx, bias, sink, gn):
    b, s, _ = q.shape
    qb = ATTN_QB
    nblk = s // BLOCK_Q
    nstep = nblk // qb
    lx = kvx.shape[1]
    cur = lambda bb, n: (bb, n, 0)
    prev = lambda bb, n: (bb, jnp.maximum(n * qb - 1, 0), 0)
    nxt = lambda bb, n: (bb, jnp.minimum((n + 1) * qb, nblk - 1), 0)
    halo = lambda im: pl.BlockSpec((None, BLOCK_Q, 2 * D_KV), im)
    return pl.pallas_call(
        functools.partial(_attn_kernel, qb=qb, nstep=nstep),
        grid=(b, nstep),
        in_specs=[pl.BlockSpec((None, qb * BLOCK_Q, D_ATTN), cur),
                  halo(prev), pl.BlockSpec((None, qb * BLOCK_Q, 2 * D_KV), cur), halo(nxt),
                  pl.BlockSpec((None, lx, 2 * D_KV), lambda bb, n: (bb, 0, 0)),
                  _const_spec(bias.shape), pl.BlockSpec(memory_space=pltpu.SMEM),
                  _const_spec(gn.shape)],
        out_specs=pl.BlockSpec((None, qb * BLOCK_Q, D_ATTN), cur),
        out_shape=jax.ShapeDtypeStruct((b, s, D_ATTN), BF16),
        scratch_shapes=[pltpu.VMEM(((qb + 2) * BLOCK_Q, D_KV), BF16),
                        pltpu.VMEM(((qb + 2) * BLOCK_Q, D_KV + LANES), BF16),
                        pltpu.VMEM((lx, D_KV), BF16),
                        pltpu.VMEM((lx, D_KV + LANES), BF16)],
        compiler_params=_params("arbitrary", "arbitrary"),
        name="attention",
    )(q, kv, kv, kv, kvx, bias, sink, gn)


def _attn_bias():
    i = np.arange(BLOCK_Q)[:, None]
    j = np.arange(3 * BLOCK_Q)[None, :]
    band = np.abs(i + BLOCK_Q - j) <= WINDOW
    variants = [band & (j >= BLOCK_Q), band, band & (j < 2 * BLOCK_Q)]
    return jnp.asarray(np.stack([np.where(ok, 0.0, NEG_INF) for ok in variants]).astype(np.float32))


def _ffn_kernel(rn_ref, rnp_ref, rnn_ref, an_ref, anp_ref, ann_ref, x_ref, xp_ref, xn_ref,
                wo_ref, g1_ref, gsc_ref, sh_ref, wu_ref, cw_ref, cb_ref, wd_ref, g2_ref, fg_ref,
                o_ref, mix, x1s, hext, ug, uv, act, acc, nat, *, nt, tm):
    d_ff = wd_ref.shape[0]
    nc = d_ff // FFN_CHUNK
    span = lambda j, branch: slice(branch * d_ff + j * FFN_CHUNK, branch * d_ff + (j + 1) * FFN_CHUNK)
    i = pl.program_id(1)
    halo = SUBLANES_BF16
    body = slice(halo, halo + tm)
    tail = slice(halo + tm, 2 * halo + tm)
    for rows, r_ref, a_ref in ((slice(0, halo), rnp_ref, anp_ref), (body, rn_ref, an_ref),
                               (tail, rnn_ref, ann_ref)):
        mix[rows, :D_RNN] = r_ref[...]
        mix[rows, D_RNN:] = a_ref[...]
    proj = jnp.dot(mix[...], wo_ref[...], preferred_element_type=F32)

    def residual(xin_ref, rows):
        return xin_ref[...] + g1_ref[...] * proj[rows]

    def modulated(x1):
        ms = jnp.mean(x1 * x1, axis=-1, keepdims=True)
        return (x1 * lax.rsqrt(ms + EPS) * gsc_ref[...] + sh_ref[...]).astype(BF16)

    x1 = residual(x_ref, body)
    x1s[...] = x1
    hext[body] = modulated(x1)
    zeros = jnp.zeros((halo, hext.shape[1]), BF16)
    hext[0:halo] = jnp.where(i > 0, modulated(residual(xp_ref, slice(0, halo))), zeros)
    hext[tail] = jnp.where(i < nt - 1, modulated(residual(xn_ref, tail)), zeros)
    base = halo - FFN_CONV_LEFT

    half = tm // 2
    nslab = FFN_CHUNK // LANES

    def up(j, slot):
        he = hext[...]
        g = jnp.dot(he, wu_ref[:, span(j,0)], preferred_element_type=F32)
        v = jnp.dot(he, wu_ref[:, span(j,1)], preferred_element_type=F32)
        for s in range(nslab):
            ug[slot, s] = g[:, LANES * s:LANES * (s + 1)]
            uv[slot, s] = v[:, LANES * s:LANES * (s + 1)]

    def conv(u, slot, cw, cb):
        cols = []
        for s in range(nslab):
            sl = slice(LANES * s, LANES * (s + 1))
            taps = [u[slot, s, pl.ds(base + t, half, stride=2), :] for t in range(FFN_CONV_W + 1)]
            even = cb[:, sl]
            odd = cb[:, sl]
            for t in range(FFN_CONV_W):
                even = even + taps[t] * cw[t:t + 1, sl]
                odd = odd + taps[t + 1] * cw[t:t + 1, sl]
            cols.append(jnp.concatenate([even, odd], axis=0))
        return jnp.concatenate(cols, axis=1)

    def gate(j, slot):
        hg = 0.5 * conv(ug, slot, cw_ref[:, span(j,0)], cb_ref[:, span(j,0)])
        silu = hg * jnp.tanh(hg) + hg
        val = conv(uv, slot, cw_ref[:, span(j,1)], cb_ref[:, span(j,1)])
        act[slot] = (silu * val).astype(BF16)

    def down(j, slot):
        return jnp.dot(act[slot], wd_ref[span(j,0), :], preferred_element_type=F32)

    up(0, 0)
    for j in range(nc):
        if j + 1 < nc:
            up(j + 1, (j + 1) % 2)
        gate(j, j % 2)
        if j == 1:
            acc[...] = down(0, 0)
        elif j > 1:
            acc[...] += down(j - 1, (j - 1) % 2)
    mixed = acc[...] + down(nc - 1, (nc - 1) % 2)
    nlane = mixed.shape[1] // LANES
    for s in range(nlane):
        sl = slice(LANES * s, LANES * (s + 1))
        nat[s, pl.ds(0, half, stride=2), :] = mixed[:half, sl]
        nat[s, pl.ds(1, half, stride=2), :] = mixed[half:, sl]
    ffn = jnp.concatenate([nat[s] for s in range(nlane)], axis=1)
    y = x1s[...] + g2_ref[...] * ffn
    ms = jnp.mean(y * y, axis=-1, keepdims=True)
    o_ref[...] = y * lax.rsqrt(ms + EPS) * fg_ref[...]


def _mixer_out_ffn(rn, an, x, wo, g1, gsc, sh, wu, cw, cb, wd, g2, fg, *, tm):
    b, s, d = x.shape
    nt = s // tm
    assert wd.shape[0] % FFN_CHUNK == 0 and (wd.shape[0] // FFN_CHUNK) % 2 == 1
    halo = SUBLANES_BF16
    per = tm // halo
    nblk = s // halo
    row = lambda bb, i: (bb, i, 0)
    prev = lambda bb, i: (bb, jnp.maximum(i * per - 1, 0), 0)
    nxt = lambda bb, i: (bb, jnp.minimum((i + 1) * per, nblk - 1), 0)
    vec = lambda bb, i: (bb, 0, 0)
    banded = lambda w: [pl.BlockSpec((None, tm, w), row), pl.BlockSpec((None, halo, w), prev),
                        pl.BlockSpec((None, halo, w), nxt)]
    per_sample = pl.BlockSpec((None, 1, d), vec)
    return pl.pallas_call(
        functools.partial(_ffn_kernel, nt=nt, tm=tm),
        grid=(b, nt),
        in_specs=banded(D_RNN) + banded(D_ATTN) + banded(d)
        + [_const_spec(wo.shape), per_sample, per_sample, per_sample,
           _const_spec(wu.shape), _const_spec(cw.shape), _const_spec(cb.shape),
           _const_spec(wd.shape), per_sample, _const_spec(fg.shape)],
        out_specs=pl.BlockSpec((None, tm, d), row),
        out_shape=jax.ShapeDtypeStruct((b, s, d), F32),
        scratch_shapes=[pltpu.VMEM((tm + 2 * halo, D_RNN + D_ATTN), BF16),
                        pltpu.VMEM((tm, d), F32),
                        pltpu.VMEM((tm + 2 * halo, d), BF16),
                        pltpu.VMEM((2, FFN_CHUNK // LANES, tm + 2 * halo, LANES), F32),
                        pltpu.VMEM((2, FFN_CHUNK // LANES, tm + 2 * halo, LANES), F32),
                        pltpu.VMEM((2, tm, FFN_CHUNK), BF16),
                        pltpu.VMEM((tm, d), F32),
                        pltpu.VMEM((d // LANES, tm, LANES), F32)],
        compiler_params=_params("arbitrary", "arbitrary"),
        name="out_proj_conv_ffn",
    )(rn, rn, rn, an, an, an, x, x, x, wo, g1, gsc, sh, wu, cw, cb, wd, g2, fg)


def _rope_tables(s):
    half = HEAD_DIM // 4
    inv = ROPE_THETA ** (-jnp.arange(half, dtype=F32) / half)

    def tables(npos, lanes_first):
        ang = jnp.arange(npos, dtype=jnp.int32).astype(F32)[:, None] * inv[None, :]
        zero = jnp.zeros((npos, 2 * half), F32)
        cos = jnp.concatenate([jnp.cos(ang)] * 2, axis=1)
        sin = jnp.concatenate([-jnp.sin(ang), jnp.sin(ang)], axis=1)
        order = (lambda t: [t, zero]) if lanes_first else (lambda t: [zero, t])
        reps = LANES // HEAD_DIM
        return [jnp.tile(jnp.concatenate(order(t), axis=1), (1, reps)) for t in (cos, sin)]

    cosr, sinr = tables(s // GRID_W, True)
    cosc, sinc = tables(GRID_W, False)
    return cosr, sinr, cosc, sinc


def _gate_weights(w_a, w_i):
    per = MXU_DIM // RNN_BLOCK_W
    eye = jnp.eye(per, dtype=F32)

    def dense(w):
        w = w.reshape(RNN_BLOCKS // per, per, RNN_BLOCK_W, RNN_BLOCK_W)
        m = w[:, :, :, None, :] * eye[None, :, None, :, None]
        return m.reshape(RNN_BLOCKS // per, MXU_DIM, MXU_DIM)

    return (0.5 * jnp.concatenate([dense(w_a), dense(w_i)], axis=2)).astype(BF16)


def _interleave_heads(w, axis):
    shape = w.shape
    w = w.reshape(shape[:axis] + (N_KV_HEADS, Q_PER_KV, HEAD_DIM) + shape[axis + 1:])
    return jnp.swapaxes(w, axis, axis + 1).reshape(shape)


def kernel(x, c, ctx, c_ctx, w_mod, b_mod, norm1_g, w_in, rnn_conv_w, rnn_conv_b, lru_w_a, lru_b_a,
           lru_w_i, lru_b_i, lru_lam, attn_sink, gn_rnn, gn_attn, w_out, norm2_g, w_up, ffn_conv_w,
           ffn_conv_b, w_down, final_g):
    assert w_mod.shape[0] == 1, "one layer: the last layer's context outputs are never consumed"
    b, s, d = x.shape
    lx = ctx.shape[1]

    cond = jnp.concatenate([c, c_ctx[None], jnp.zeros((SUBLANES_F32 - b - 1, d), F32)], axis=0)
    mod = _modulation(cond, w_mod[0], b_mod[0])
    sh1, sc1, g1, sh2, sc2, g2 = [mod[:b, None, d * n:d * (n + 1)] for n in range(6)]
    csh1 = jnp.broadcast_to(mod[b:b + 1, None, :d], (b, 1, d))
    csc1 = jnp.broadcast_to(mod[b:b + 1, None, d:2 * d], (b, 1, d))
    gsc1 = norm1_g[0] * (1.0 + sc1)
    cgsc1 = norm1_g[0] * (1.0 + csc1)
    gsc2 = norm2_g[0] * (1.0 + sc2)

    wi = w_in[0].astype(BF16)
    q0 = 2 * D_RNN
    wi = jnp.concatenate([wi[:, :q0], _interleave_heads(wi[:, q0:q0 + D_ATTN], 1),
                          wi[:, q0 + D_ATTN:]], axis=1)
    wo = w_out[0].astype(BF16)
    wo = jnp.concatenate([wo[:D_RNN], _interleave_heads(wo[D_RNN:], 0)], axis=0)
    gn_a = _interleave_heads(gn_attn[0], 0)[None]
    gn_r = gn_rnn[0][None]
    cw = rnn_conv_w[0]
    cb = rnn_conv_b[0][None]
    sp = (-0.5 * LRU_C) * jax.nn.softplus(-lru_lam[0])

    tables = _rope_tables(s)
    xc, gx, q, kv = _in_proj(x, gsc1, sh1, wi, cw, cb, tables, rope=True, tm=TM_PROJ)
    xcc, _, _, kvx = _in_proj(ctx, cgsc1, csh1, wi, cw, cb, tables, rope=False, tm=lx)

    h0 = jnp.zeros((b, 1, D_RNN), F32)
    y_dir = None
    for dr in range(2):
        wg = _gate_weights(lru_w_a[0, dr], lru_w_i[0, dr])
        gate_args = (wg, 0.5 * lru_b_a[0, dr][None], 0.5 * lru_b_i[0, dr][None], sp[dr][None])
        rev = dr == 1
        _, h_ctx = _rglru(xcc, *gate_args, h0, reverse=rev, tm=lx, blk=min(lx, SUB_PROJ))
        fin = (y_dir, gx, gn_r) if rev else None
        y_dir, _ = _rglru(xc, *gate_args, h_ctx, reverse=rev, tm=TM_SCAN, blk=SUB_PROJ,
                          final_args=fin)
    rn = y_dir

    an = _attention(q, kv, kvx, _attn_bias(), attn_sink[0] * LOG2E, gn_a)

    return _mixer_out_ffn(rn, an, x, wo, g1, gsc2, sh2, w_up[0].astype(BF16), ffn_conv_w[0],
                          ffn_conv_b[0][None], w_down[0].astype(BF16), g2, final_g[None], tm=TM_FFN)
```

```python
import functools

import jax
import jax.numpy as jnp
import numpy as np
from jax import lax
from jax.experimental import pallas as pl
from jax.experimental.pallas import tpu as pltpu

F32 = jnp.float32
BF16 = jnp.bfloat16

EPS = 1e-6
GRID_W = 64
D_RNN = 512
RNN_BLOCKS = 8
RNN_BLOCK_W = D_RNN // RNN_BLOCKS
RNN_CONV_W = 4
RNN_CONV_LEFT = 2
LRU_C = 8.0
HEAD_DIM = 64
N_Q_HEADS = 8
N_KV_HEADS = 2
Q_PER_KV = N_Q_HEADS // N_KV_HEADS
D_ATTN = N_Q_HEADS * HEAD_DIM
D_KV = N_KV_HEADS * HEAD_DIM
WINDOW = 128
BLOCK_Q = 128
ROPE_THETA = 10000.0
NEG_INF = -1e30
LOG2E = 1.4426950408889634
F32_TINY = 2.0 ** -126
FFN_CONV_W = 3
FFN_CONV_LEFT = 1

LANES = 128
SUBLANES_F32 = 8
SUBLANES_BF16 = 16
MXU_DIM = 256
VMEM_LIMIT = 56 * 1024 * 1024

TM_PROJ = 2048
SUB_PROJ = 512
TM_SCAN = 1024
TM_FFN = 512
FFN_CHUNK = 256
ATTN_QB = 8
ATTN_LOOKAHEAD = 1


def _params(*sem):
    return pltpu.CompilerParams(dimension_semantics=sem, vmem_limit_bytes=VMEM_LIMIT)


def _const_spec(shape):
    zeros = (0,) * len(shape)
    return pl.BlockSpec(shape, lambda *_: zeros, pipeline_mode=pl.Buffered(1))


def _mod_kernel(cond_ref, w_ref, b_ref, o_ref):
    s = cond_ref[...]
    s = s * jax.nn.sigmoid(s)
    o_ref[...] = jnp.dot(s, w_ref[...], preferred_element_type=F32,
                         precision=lax.Precision.HIGHEST) + b_ref[...]


def _modulation(cond, w_mod, b_mod):
    rows, d = cond.shape
    n = w_mod.shape[1]
    tn = 768
    return pl.pallas_call(
        _mod_kernel,
        grid=(n // tn,),
        in_specs=[pl.BlockSpec((rows, d), lambda j: (0, 0)),
                  pl.BlockSpec((d, tn), lambda j: (0, j)),
                  pl.BlockSpec((1, tn), lambda j: (0, j))],
        out_specs=pl.BlockSpec((rows, tn), lambda j: (0, j)),
        out_shape=jax.ShapeDtypeStruct((rows, n), F32),
        compiler_params=_params("arbitrary"),
        name="modulation",
    )(cond, w_mod, b_mod.reshape(1, n))


def _rope_partner(t):
    lane = lax.broadcasted_iota(jnp.int32, t.shape, 1)
    first = (lane % 32) < 16
    return jnp.where(first, pltpu.roll(t, LANES - 16, 1), pltpu.roll(t, 16, 1))


def _inproj_kernel(x_ref, xp_ref, xn_ref, gsc_ref, sh_ref, w_ref, cw_ref, cb_ref,
                   cosr_ref, sinr_ref, cosc_ref, sinc_ref,
                   xc_ref, gx_ref, q_ref, kv_ref, hsc, pslab, *, rope, sub, nt):
    i = pl.program_id(1)
    tm = x_ref.shape[0]
    halo = SUBLANES_BF16
    half = sub // 2
    base = halo - RNN_CONV_LEFT
    q0 = D_RNN
    k0 = q0 + D_ATTN
    v0 = k0 + D_KV
    scale = HEAD_DIM ** -0.5 * LOG2E
    nslab = D_RNN // LANES

    def modulated(x):
        ms = jnp.mean(x * x, axis=-1, keepdims=True)
        return (x * lax.rsqrt(ms + EPS) * gsc_ref[...] + sh_ref[...]).astype(BF16)

    zeros = jnp.zeros((halo, hsc.shape[1]), BF16)
    hsc[0:halo] = jnp.where(i > 0, modulated(xp_ref[...]), zeros)
    hsc[halo + tm:] = jnp.where(i < nt - 1, modulated(xn_ref[...]), zeros)

    for n in range(tm // sub):
        hsc[halo + sub * n:halo + sub * (n + 1)] = modulated(x_ref[sub * n:sub * (n + 1), :])

    cw = cw_ref[...]
    cb = cb_ref[...]
    def project(n):
        pr = jnp.dot(hsc[sub * n:sub * (n + 1) + 2 * halo], w_ref[:, :D_RNN],
                     preferred_element_type=F32)
        p = jnp.dot(hsc[halo + sub * n:halo + sub * (n + 1)], w_ref[:, D_RNN:],
                    preferred_element_type=F32)
        return pr, p

    def epilogue(n, pr, p):
        rows = slice(sub * n, sub * (n + 1))
        cols = []
        for s in range(nslab):
            sl = slice(LANES * s, LANES * (s + 1))
            slab = pslab.at[n * nslab + s]
            slab[...] = pr[:, sl]
            taps = [slab[pl.ds(base + j, half, stride=2), :] for j in range(RNN_CONV_W + 1)]
            even = cb[:, sl]
            odd = cb[:, sl]
            for j in range(RNN_CONV_W):
                even = even + taps[j] * cw[j:j + 1, sl]
                odd = odd + taps[j + 1] * cw[j:j + 1, sl]
            cols.append(jnp.concatenate([even, odd], axis=0))
        xc_ref[rows, :] = jnp.concatenate(cols, axis=1)
        gx_ref[rows, :] = jax.nn.gelu(p[:, :D_RNN])
        cols = [p[:, q0 + LANES * c:q0 + LANES * (c + 1)] for c in range(D_ATTN // LANES)]
        k = p[:, k0:v0]
        if rope:
            per = sub // GRID_W
            trows = slice(per * n, per * (n + 1))
            expand = lambda r_ref, c_ref: (r_ref[trows, :][:, None, :]
                                           + c_ref[...][None, :, :]).reshape(sub, LANES)
            cos = expand(cosr_ref, cosc_ref)
            sin = expand(sinr_ref, sinc_ref)
            cols = [t * cos + _rope_partner(t) * sin for t in cols]
            k = k * cos + _rope_partner(k) * sin
        for c, t in enumerate(cols):
            q_ref[rows, LANES * c:LANES * (c + 1)] = (t * scale).astype(BF16)
        kv_ref[rows, :D_KV] = k.astype(BF16)
        kv_ref[rows, D_KV:] = p[:, v0:v0 + D_KV].astype(BF16)

    nsub = tm // sub
    ready = project(0)
    for n in range(nsub):
        following = project(n + 1) if n + 1 < nsub else None
        epilogue(n, *ready)
        ready = following


def _in_proj(x, gsc, sh, w_in, cw, cb, tables, *, rope, tm):
    b, s, d = x.shape
    n = w_in.shape[1]
    nt = s // tm
    sub = min(tm, SUB_PROJ)
    halo = SUBLANES_BF16
    per = tm // halo
    row = lambda bb, i: (bb, i, 0)
    prev = lambda bb, i: (bb, jnp.maximum(i * per - 1, 0), 0)
    nxt = lambda bb, i: (bb, jnp.minimum((i + 1) * per, s // halo - 1), 0)
    vec = lambda bb, i: (bb, 0, 0)
    cosr, sinr, cosc, sinc = tables
    if rope:
        assert tm % GRID_W == 0
        rtab = pl.BlockSpec((tm // GRID_W, LANES), lambda bb, i: (i, 0))
    else:
        rtab = _const_spec(cosr.shape)
    outs = [(D_RNN, F32), (D_RNN, F32), (D_ATTN, BF16), (2 * D_KV, BF16)]
    return pl.pallas_call(
        functools.partial(_inproj_kernel, rope=rope, sub=sub, nt=nt),
        grid=(b, nt),
        in_specs=[pl.BlockSpec((None, tm, d), row),
                  pl.BlockSpec((None, halo, d), prev),
                  pl.BlockSpec((None, halo, d), nxt),
                  pl.BlockSpec((None, 1, d), vec),
                  pl.BlockSpec((None, 1, d), vec),
                  _const_spec((d, n)), _const_spec(cw.shape), _const_spec(cb.shape),
                  rtab, rtab, _const_spec(cosc.shape), _const_spec(sinc.shape)],
        out_specs=[pl.BlockSpec((None, tm, w), row) for w, _ in outs],
        out_shape=[jax.ShapeDtypeStruct((b, s, w), dt) for w, dt in outs],
        scratch_shapes=[pltpu.VMEM((tm + 2 * halo, d), BF16),
                        pltpu.VMEM((tm // sub * (D_RNN // LANES), sub + 2 * halo, LANES), F32)],
        compiler_params=_params("arbitrary", "arbitrary"),
        name="in_proj_rope" if rope else "in_proj_ctx",
    )(x, x, x, gsc, sh, w_in, cw, cb, cosr, sinr, cosc, sinc)


def _rglru_kernel(*refs, reverse, final, blk):
    if final:
        (xc_ref, wg_ref, ba_ref, bi_ref, sp_ref, h0_ref, yo_ref, gx_ref, gn_ref,
         y_ref, hfin_ref, a_scr, u_scr, y_scr, h_scr) = refs
    else:
        (xc_ref, wg_ref, ba_ref, bi_ref, sp_ref, h0_ref,
         y_ref, hfin_ref, a_scr, u_scr, y_scr, h_scr) = refs
    i = pl.program_id(0)
    nb, tm, _ = xc_ref.shape

    @pl.when(i == 0)
    def _():
        h_scr[...] = h0_ref[...]

    for bb in range(nb):
        xc = xc_ref[bb]
        xb = xc.astype(BF16)
        for g in range(D_RNN // MXU_DIM):
            sl = slice(MXU_DIM * g, MXU_DIM * (g + 1))
            z = jnp.dot(xb[:, sl], wg_ref[g], preferred_element_type=F32)
            tr = jnp.tanh(z[:, :MXU_DIM] + ba_ref[:, sl])
            gi = 0.5 * jnp.tanh(z[:, MXU_DIM:] + bi_ref[:, sl]) + 0.5
            log_a = (tr + 1.0) * sp_ref[:, sl]
            a = jnp.exp(log_a)
            a_scr[bb, :, sl] = a
            w = -jnp.tanh(log_a) * (1.0 + a * a)
            u_scr[bb, :, sl] = w * lax.rsqrt(jnp.maximum(w, F32_TINY)) * (gi * xc[:, sl])

    rows = SUBLANES_F32
    half = blk // 2
    ngrp = half // rows
    order = range(rows - 1, -1, -1) if reverse else range(rows)
    parity = (1, 0) if reverse else (0, 1)
    spans = range(tm // blk - 1, -1, -1) if reverse else range(tm // blk)

    def group(g, hs, off):
        gg = (ngrp - 1 - g) if reverse else g
        lo = pl.multiple_of(off + gg * rows, rows)
        hi = pl.multiple_of(off + half + gg * rows, rows)
        out = pl.multiple_of(off + gg * 2 * rows, 2 * rows)
        hs = list(hs)
        for j in order:
            for par in parity:
                start = hi if par else lo
                for bb in range(nb):
                    av = a_scr.at[bb, pl.ds(start, rows), :]
                    uv = u_scr.at[bb, pl.ds(start, rows), :]
                    yv = y_scr.at[bb, pl.ds(out, 2 * rows), :]
                    hs[bb] = av[j:j + 1, :] * hs[bb] + uv[j:j + 1, :]
                    yv[2 * j + par:2 * j + par + 1, :] = hs[bb]
        return tuple(hs)

    hs = tuple(h_scr[bb] for bb in range(nb))
    for span in spans:
        hs = lax.fori_loop(0, ngrp, functools.partial(group, off=span * blk), hs)
    for bb in range(nb):
        h_scr[bb] = hs[bb]
        hfin_ref[bb] = hs[bb]

    if final:
        z = gx_ref[...] * (y_scr[...] + yo_ref[...])
        ms = jnp.mean(z * z, axis=-1, keepdims=True)
        y_ref[...] = (z * lax.rsqrt(ms + EPS) * gn_ref[...]).astype(BF16)
    else:
        y_ref[...] = y_scr[...]


def _rglru(xc, wg, ba, bi, sp, h0, *, reverse, tm, blk, final_args=None):
    b, s, d = xc.shape
    nt = s // tm
    final = final_args is not None
    row = (lambda i: (0, nt - 1 - i, 0)) if reverse else (lambda i: (0, i, 0))
    in_specs = [pl.BlockSpec((b, tm, d), row), _const_spec(wg.shape),
                _const_spec(ba.shape), _const_spec(bi.shape), _const_spec(sp.shape),
                _const_spec(h0.shape)]
    args = [xc, wg, ba, bi, sp, h0]
    if final:
        y_other, gx, gn = final_args
        in_specs += [pl.BlockSpec((b, tm, d), row), pl.BlockSpec((b, tm, d), row),
                     _const_spec(gn.shape)]
        args += [y_other, gx, gn]
    return pl.pallas_call(
        functools.partial(_rglru_kernel, reverse=reverse, final=final, blk=blk),
        grid=(nt,),
        in_specs=in_specs,
        out_specs=[pl.BlockSpec((b, tm, d), row), pl.BlockSpec((b, 1, d), lambda i: (0, 0, 0))],
        out_shape=[jax.ShapeDtypeStruct((b, s, d), BF16 if final else F32),
                   jax.ShapeDtypeStruct((b, 1, d), F32)],
        scratch_shapes=[pltpu.VMEM((b, tm, d), F32), pltpu.VMEM((b, tm, d), F32),
                        pltpu.VMEM((b, tm, d), F32), pltpu.VMEM((b, 1, d), F32)],
        compiler_params=_params("arbitrary"),
        name="rglru_" + ("bwd" if reverse else "fwd") + ("_final" if final else ""),
    )(*args)


def _attn_kernel(q_ref, kvp_ref, kvc_ref, kvn_ref, kvx_ref, bias_ref, sink_ref, gn_ref, o_ref,
                 keys, vals, keyx, valx, *, qb, nstep):
    ncol = D_ATTN // LANES
    nwin = 3 * BLOCK_Q
    step = pl.program_id(1)
    band = ((slice(0, BLOCK_Q), kvp_ref), (slice(BLOCK_Q, BLOCK_Q * (qb + 1)), kvc_ref),
            (slice(BLOCK_Q * (qb + 1), BLOCK_Q * (qb + 2)), kvn_ref))
    for rows, kv_ref in band:
        keys[rows, :] = kv_ref[:, :D_KV]
        vals[rows, :D_KV] = kv_ref[:, D_KV:]
    keyx[...] = kvx_ref[:, :D_KV]
    valx[:, :D_KV] = kvx_ref[:, D_KV:]
    vals[:, D_KV:] = jnp.ones((vals.shape[0], LANES), BF16)
    valx[:, D_KV:] = jnp.ones((valx.shape[0], LANES), BF16)

    low = lax.broadcasted_iota(jnp.int32, (BLOCK_Q, LANES), 1) < HEAD_DIM
    first = lax.broadcasted_iota(jnp.int32, (2 * BLOCK_Q, 1), 0) < BLOCK_Q
    zero = jnp.zeros((BLOCK_Q, LANES), BF16)
    nt_dims = (((1,), (1,)), ((), ()))
    def block_bias(x):
        variant = 1
        if x == 0:
            variant = jnp.where(step == 0, 0, variant)
        if x == qb - 1:
            variant = jnp.where(step == nstep - 1, 2, variant)
        return bias_ref[variant]

    def scores(x, c, bias):
        win = slice(BLOCK_Q * x, BLOCK_Q * x + nwin)
        t = q_ref[BLOCK_Q * x:BLOCK_Q * (x + 1), LANES * c:LANES * (c + 1)]
        qs = jnp.concatenate([jnp.where(low, t, zero), jnp.where(low, zero, t)], axis=0)
        s_loc = lax.dot_general(qs, keys[win, :], nt_dims, preferred_element_type=F32)
        s_ctx = lax.dot_general(qs, keyx[...], nt_dims, preferred_element_type=F32)
        s_loc = (s_loc.reshape(2, BLOCK_Q, nwin) + bias[None]).reshape(2 * BLOCK_Q, nwin)
        sink = jnp.where(first, sink_ref[c], sink_ref[ncol + c])
        m = jnp.maximum(jnp.maximum(jnp.max(s_loc, axis=-1, keepdims=True),
                                    jnp.max(s_ctx, axis=-1, keepdims=True)), sink)
        e_loc = jnp.exp2(s_loc - m).astype(BF16)
        e_ctx = jnp.exp2(s_ctx - m).astype(BF16)
        return x, e_loc, e_ctx, jnp.exp2(sink - m)

    def weighted(x, e_loc, e_ctx, e_sink):
        win = slice(BLOCK_Q * x, BLOCK_Q * x + nwin)
        pv = (jnp.dot(e_loc, vals[win, :], preferred_element_type=F32)
              + jnp.dot(e_ctx, valx[...], preferred_element_type=F32))
        o = pv[:, :D_KV] / (pv[:, D_KV:] + e_sink)
        return jnp.where(low, o[:BLOCK_Q], o[BLOCK_Q:])

    def finish(x, outs):
        ms = sum(jnp.sum(t * t, axis=-1, keepdims=True) for t in outs) * (1.0 / D_ATTN)
        inv = lax.rsqrt(ms + EPS)
        for c, t in enumerate(outs):
            sl = slice(LANES * c, LANES * (c + 1))
            o_ref[BLOCK_Q * x:BLOCK_Q * (x + 1), sl] = (t * inv * gn_ref[:, sl]).astype(BF16)

    pending = []
    outs = [[] for _ in range(qb)]

    def retire():
        chain = pending.pop(0)
        x = chain[0]
        outs[x].append(weighted(*chain))
        if len(outs[x]) == ncol:
            finish(x, outs[x])

    for x in range(qb):
        bias = block_bias(x)
        for c in range(ncol):
            pending.append(scores(x, c, bias))
            if len(pending) > ATTN_LOOKAHEAD:
                retire()
    while pending:
        retire()


def _attention(q, kv, kvx, bias, sink, gn):
    b, s, _ = q.shape
    qb = ATTN_QB
    nblk = s // BLOCK_Q
    nstep = nblk // qb
    lx = kvx.shape[1]
    cur = lambda bb, n: (bb, n, 0)
    prev = lambda bb, n: (bb, jnp.maximum(n * qb - 1, 0), 0)
    nxt = lambda bb, n: (bb, jnp.minimum((n + 1) * qb, nblk - 1), 0)
    halo = lambda im: pl.BlockSpec((None, BLOCK_Q, 2 * D_KV), im)
    return pl.pallas_call(
        functools.partial(_attn_kernel, qb=qb, nstep=nstep),
        grid=(b, nstep),
        in_specs=[pl.BlockSpec((None, qb * BLOCK_Q, D_ATTN), cur),
                  halo(prev), pl.BlockSpec((None, qb * BLOCK_Q, 2 * D_KV), cur), halo(nxt),
                  pl.BlockSpec((None, lx, 2 * D_KV), lambda bb, n: (bb, 0, 0)),
                  _const_spec(bias.shape), pl.BlockSpec(memory_space=pltpu.SMEM),
                  _const_spec(gn.shape)],
        out_specs=pl.BlockSpec((None, qb * BLOCK_Q, D_ATTN), cur),
        out_shape=jax.ShapeDtypeStruct((b, s, D_ATTN), BF16),
        scratch_shapes=[pltpu.VMEM(((qb + 2) * BLOCK_Q, D_KV), BF16),
                        pltpu.VMEM(((qb + 2) * BLOCK_Q, D_KV + LANES), BF16),
                        pltpu.VMEM((lx, D_KV), BF16),
                        pltpu.VMEM((lx, D_KV + LANES), BF16)],
        compiler_params=_params("arbitrary", "arbitrary"),
        name="attention",
    )(q, kv, kv, kv, kvx, bias, sink, gn)


def _attn_bias():
    i = np.arange(BLOCK_Q)[:, None]
    j = np.arange(3 * BLOCK_Q)[None, :]
    band = np.abs(i + BLOCK_Q - j) <= WINDOW
    variants = [band & (j >= BLOCK_Q), band, band & (j < 2 * BLOCK_Q)]
    return jnp.asarray(np.stack([np.where(ok, 0.0, NEG_INF) for ok in variants]).astype(np.float32))


def _ffn_kernel(rn_ref, rnp_ref, rnn_ref, an_ref, anp_ref, ann_ref, x_ref, xp_ref, xn_ref,
                wo_ref, g1_ref, gsc_ref, sh_ref, wu_ref, cw_ref, cb_ref, wd_ref, g2_ref, fg_ref,
                o_ref, mix, x1s, hext, ug, uv, act, acc, nat, *, nt, tm):
    d_ff = wd_ref.shape[0]
    nc = d_ff // FFN_CHUNK
    span = lambda j, branch: slice(branch * d_ff + j * FFN_CHUNK, branch * d_ff + (j + 1) * FFN_CHUNK)
    i = pl.program_id(1)
    halo = SUBLANES_BF16
    body = slice(halo, halo + tm)
    tail = slice(halo + tm, 2 * halo + tm)
    for rows, r_ref, a_ref in ((slice(0, halo), rnp_ref, anp_ref), (body, rn_ref, an_ref),
                               (tail, rnn_ref, ann_ref)):
        mix[rows, :D_RNN] = r_ref[...]
        mix[rows, D_RNN:] = a_ref[...]
    proj = jnp.dot(mix[...], wo_ref[...], preferred_element_type=F32)

    def residual(xin_ref, rows):
        return xin_ref[...] + g1_ref[...] * proj[rows]

    def modulated(x1):
        ms = jnp.mean(x1 * x1, axis=-1, keepdims=True)
        return (x1 * lax.rsqrt(ms + EPS) * gsc_ref[...] + sh_ref[...]).astype(BF16)

    x1 = residual(x_ref, body)
    x1s[...] = x1
    hext[body] = modulated(x1)
    zeros = jnp.zeros((halo, hext.shape[1]), BF16)
    hext[0:halo] = jnp.where(i > 0, modulated(residual(xp_ref, slice(0, halo))), zeros)
    hext[tail] = jnp.where(i < nt - 1, modulated(residual(xn_ref, tail)), zeros)

    base = halo - FFN_CONV_LEFT

    half = tm // 2
    nslab = FFN_CHUNK // LANES

    def up(j, slot):
        he = hext[...]
        g = jnp.dot(he, wu_ref[:, span(j,0)], preferred_element_type=F32)
        v = jnp.dot(he, wu_ref[:, span(j,1)], preferred_element_type=F32)
        for s in range(nslab):
            ug[slot, s] = g[:, LANES * s:LANES * (s + 1)]
            uv[slot, s] = v[:, LANES * s:LANES * (s + 1)]

    def conv(u, slot, cw, cb):
        cols = []
        for s in range(nslab):
            sl = slice(LANES * s, LANES * (s + 1))
            taps = [u[slot, s, pl.ds(base + t, half, stride=2), :] for t in range(FFN_CONV_W + 1)]
            even = cb[:, sl]
            odd = cb[:, sl]
            for t in range(FFN_CONV_W):
                even = even + taps[t] * cw[t:t + 1, sl]
                odd = odd + taps[t + 1] * cw[t:t + 1, sl]
            cols.append(jnp.concatenate([even, odd], axis=0))
        return jnp.concatenate(cols, axis=1)

    def gate(j, slot):
        hg = 0.5 * conv(ug, slot, cw_ref[:, span(j,0)], cb_ref[:, span(j,0)])
        silu = hg * jnp.tanh(hg) + hg
        val = conv(uv, slot, cw_ref[:, span(j,1)], cb_ref[:, span(j,1)])
        act[slot] = (silu * val).astype(BF16)

    def down(j, slot):
        return jnp.dot(act[slot], wd_ref[span(j,0), :], preferred_element_type=F32)

    up(0, 0)
    for j in range(nc):
        if j + 1 < nc:
            up(j + 1, (j + 1) % 2)
        gate(j, j % 2)
        if j == 1:
            acc[...] = down(0, 0)
        elif j > 1:
            acc[...] += down(j - 1, (j - 1) % 2)
    mixed = acc[...] + down(nc - 1, (nc - 1) % 2)
    nlane = mixed.shape[1] // LANES
    for s in range(nlane):
        sl = slice(LANES * s, LANES * (s + 1))
        nat[s, pl.ds(0, half, stride=2), :] = mixed[:half, sl]
        nat[s, pl.ds(1, half, stride=2), :] = mixed[half:, sl]
    ffn = jnp.concatenate([nat[s] for s in range(nlane)], axis=1)
    y = x1s[...] + g2_ref[...] * ffn
    ms = jnp.mean(y * y, axis=-1, keepdims=True)
    o_ref[...] = y * lax.rsqrt(ms + EPS) * fg_ref[...]


def _mixer_out_ffn(rn, an, x, wo, g1, gsc, sh, wu, cw, cb, wd, g2, fg, *, tm):
    b, s, d = x.shape
    nt = s // tm
    assert wd.shape[0] % FFN_CHUNK == 0 and (wd.shape[0] // FFN_CHUNK) % 2 == 1
    halo = SUBLANES_BF16
    per = tm // halo
    nblk = s // halo
    row = lambda bb, i: (bb, i, 0)
    prev = lambda bb, i: (bb, jnp.maximum(i * per - 1, 0), 0)
    nxt = lambda bb, i: (bb, jnp.minimum((i + 1) * per, nblk - 1), 0)
    vec = lambda bb, i: (bb, 0, 0)
    banded = lambda w: [pl.BlockSpec((None, tm, w), row), pl.BlockSpec((None, halo, w), prev),
                        pl.BlockSpec((None, halo, w), nxt)]
    per_sample = pl.BlockSpec((None, 1, d), vec)
    return pl.pallas_call(
        functools.partial(_ffn_kernel, nt=nt, tm=tm),
        grid=(b, nt),
        in_specs=banded(D_RNN) + banded(D_ATTN) + banded(d)
        + [_const_spec(wo.shape), per_sample, per_sample, per_sample,
           _const_spec(wu.shape), _const_spec(cw.shape), _const_spec(cb.shape),
           _const_spec(wd.shape), per_sample, _const_spec(fg.shape)],
        out_specs=pl.BlockSpec((None, tm, d), row),
        out_shape=jax.ShapeDtypeStruct((b, s, d), F32),
        scratch_shapes=[pltpu.VMEM((tm + 2 * halo, D_RNN + D_ATTN), BF16),
                        pltpu.VMEM((tm, d), F32),
                        pltpu.VMEM((tm + 2 * halo, d), BF16),
                        pltpu.VMEM((2, FFN_CHUNK // LANES, tm + 2 * halo, LANES), F32),
                        pltpu.VMEM((2, FFN_CHUNK // LANES, tm + 2 * halo, LANES), F32),
                        pltpu.VMEM((2, tm, FFN_CHUNK), BF16),
                        pltpu.VMEM((tm, d), F32),
                        pltpu.VMEM((d // LANES, tm, LANES), F32)],
        compiler_params=_params("arbitrary", "arbitrary"),
        name="out_proj_conv_ffn",
    )(rn, rn, rn, an, an, an, x, x, x, wo, g1, gsc, sh, wu, cw, cb, wd, g2, fg)


def _rope_tables(s):
    half = HEAD_DIM // 4
    inv = ROPE_THETA ** (-jnp.arange(half, dtype=F32) / half)

    def tables(npos, lanes_first):
        ang = jnp.arange(npos, dtype=jnp.int32).astype(F32)[:, None] * inv[None, :]
        zero = jnp.zeros((npos, 2 * half), F32)
        cos = jnp.concatenate([jnp.cos(ang)] * 2, axis=1)
        sin = jnp.concatenate([-jnp.sin(ang), jnp.sin(ang)], axis=1)
        order = (lambda t: [t, zero]) if lanes_first else (lambda t: [zero, t])
        reps = LANES // HEAD_DIM
        return [jnp.tile(jnp.concatenate(order(t), axis=1), (1, reps)) for t in (cos, sin)]

    cosr, sinr = tables(s // GRID_W, True)
    cosc, sinc = tables(GRID_W, False)
    return cosr, sinr, cosc, sinc


def _gate_weights(w_a, w_i):
    per = MXU_DIM // RNN_BLOCK_W
    eye = jnp.eye(per, dtype=F32)

    def dense(w):
        w = w.reshape(RNN_BLOCKS // per, per, RNN_BLOCK_W, RNN_BLOCK_W)
        m = w[:, :, :, None, :] * eye[None, :, None, :, None]
        return m.reshape(RNN_BLOCKS // per, MXU_DIM, MXU_DIM)

    return (0.5 * jnp.concatenate([dense(w_a), dense(w_i)], axis=2)).astype(BF16)


def _interleave_heads(w, axis):
    shape = w.shape
    w = w.reshape(shape[:axis] + (N_KV_HEADS, Q_PER_KV, HEAD_DIM) + shape[axis + 1:])
    return jnp.swapaxes(w, axis, axis + 1).reshape(shape)


def kernel(x, c, ctx, c_ctx, w_mod, b_mod, norm1_g, w_in, rnn_conv_w, rnn_conv_b, lru_w_a, lru_b_a,
           lru_w_i, lru_b_i, lru_lam, attn_sink, gn_rnn, gn_attn, w_out, norm2_g, w_up, ffn_conv_w,
           ffn_conv_b, w_down, final_g):
    assert w_mod.shape[0] == 1, "one layer: the last layer's context outputs are never consumed"
    b, s, d = x.shape
    lx = ctx.shape[1]

    cond = jnp.concatenate([c, c_ctx[None], jnp.zeros((SUBLANES_F32 - b - 1, d), F32)], axis=0)
    mod = _modulation(cond, w_mod[0], b_mod[0])
    sh1, sc1, g1, sh2, sc2, g2 = [mod[:b, None, d * n:d * (n + 1)] for n in range(6)]
    csh1 = jnp.broadcast_to(mod[b:b + 1, None, :d], (b, 1, d))
    csc1 = jnp.broadcast_to(mod[b:b + 1, None, d:2 * d], (b, 1, d))
    gsc1 = norm1_g[0] * (1.0 + sc1)
    cgsc1 = norm1_g[0] * (1.0 + csc1)
    gsc2 = norm2_g[0] * (1.0 + sc2)

    wi = w_in[0].astype(BF16)
    q0 = 2 * D_RNN
    wi = jnp.concatenate([wi[:, :q0], _interleave_heads(wi[:, q0:q0 + D_ATTN], 1),
                          wi[:, q0 + D_ATTN:]], axis=1)
    wo = w_out[0].astype(BF16)
    wo = jnp.concatenate([wo[:D_RNN], _interleave_heads(wo[D_RNN:], 0)], axis=0)
    gn_a = _interleave_heads(gn_attn[0], 0)[None]
    gn_r = gn_rnn[0][None]
    cw = rnn_conv_w[0]
    cb = rnn_conv_b[0][None]
    sp = (-0.5 * LRU_C) * jax.nn.softplus(-lru_lam[0])

    tables = _rope_tables(s)
    xc, gx, q, kv = _in_proj(x, gsc1, sh1, wi, cw, cb, tables, rope=True, tm=TM_PROJ)
    xcc, _, _, kvx = _in_proj(ctx, cgsc1, csh1, wi, cw, cb, tables, rope=False, tm=lx)

    h0 = jnp.zeros((b, 1, D_RNN), F32)
    y_dir = None
    for dr in range(2):
        wg = _gate_weights(lru_w_a[0, dr], lru_w_i[0, dr])
        gate_args = (wg, 0.5 * lru_b_a[0, dr][None], 0.5 * lru_b_i[0, dr][None], sp[dr][None])
        rev = dr == 1
        _, h_ctx = _rglru(xcc, *gate_args, h0, reverse=rev, tm=lx, blk=min(lx, SUB_PROJ))
        fin = (y_dir, gx, gn_r) if rev else None
        y_dir, _ = _rglru(xc, *gate_args, h_ctx, reverse=rev, tm=TM_SCAN, blk=SUB_PROJ,
                          final_args=fin)
    rn = y_dir

    an = _attention(q, kv, kvx, _attn_bias(), attn_sink[0] * LOG2E, gn_a)

    return _mixer_out_ffn(rn, an, x, wo, g1, gsc2, sh2, w_up[0].astype(BF16), ffn_conv_w[0],
                          ffn_conv_b[0][None], w_down[0].astype(BF16), g2, final_g[None], tm=TM_FFN)
```

```python
import functools

import jax
import jax.numpy as jnp
import numpy as np
from jax import lax
from jax.experimental import pallas as pl
from jax.experimental.pallas import tpu as pltpu

F32 = jnp.float32
BF16 = jnp.bfloat16

EPS = 1e-6
GRID_W = 64
D_RNN = 512
RNN_BLOCKS = 8
RNN_BLOCK_W = D_RNN // RNN_BLOCKS
RNN_CONV_W = 4
RNN_CONV_LEFT = 2
LRU_C = 8.0
HEAD_DIM = 64
N_Q_HEADS = 8
N_KV_HEADS = 2
Q_PER_KV = N_Q_HEADS // N_KV_HEADS
D_ATTN = N_Q_HEADS * HEAD_DIM
D_KV = N_KV_HEADS * HEAD_DIM
WINDOW = 128
BLOCK_Q = 128
ROPE_THETA = 10000.0
NEG_INF = -1e30
LOG2E = 1.4426950408889634
F32_TINY = 2.0 ** -126
FFN_CONV_W = 3
FFN_CONV_LEFT = 1

LANES = 128
SUBLANES_F32 = 8
SUBLANES_BF16 = 16
MXU_DIM = 256
VMEM_LIMIT = 56 * 1024 * 1024

TM_PROJ = 2048
SUB_PROJ = 512
TM_SCAN = 1024
TM_FFN = 512
FFN_CHUNK = 256
ATTN_QB = 8
ATTN_LOOKAHEAD = 1


def _params(*sem):
    return pltpu.CompilerParams(dimension_semantics=sem, vmem_limit_bytes=VMEM_LIMIT)


def _const_spec(shape):
    zeros = (0,) * len(shape)
    return pl.BlockSpec(shape, lambda *_: zeros, pipeline_mode=pl.Buffered(1))


def _mod_kernel(condt_ref, w_ref, b_ref, o_ref, *, nrows):
    st = condt_ref[...]
    st = st * jax.nn.sigmoid(st)
    w = w_ref[...]
    rows = [jnp.sum(w * st[:, r:r + 1], axis=0, keepdims=True) for r in range(nrows)]
    rows.append(jnp.zeros((o_ref.shape[0] - nrows, w.shape[1]), F32))
    o_ref[...] = jnp.concatenate(rows, axis=0) + b_ref[...]


def _modulation(cond, nrows, w_mod, b_mod):
    rows, d = cond.shape
    n = w_mod.shape[1]
    tn = 768
    return pl.pallas_call(
        functools.partial(_mod_kernel, nrows=nrows),
        grid=(n // tn,),
        in_specs=[pl.BlockSpec((d, rows), lambda j: (0, 0)),
                  pl.BlockSpec((d, tn), lambda j: (0, j)),
                  pl.BlockSpec((1, tn), lambda j: (0, j))],
        out_specs=pl.BlockSpec((rows, tn), lambda j: (0, j)),
        out_shape=jax.ShapeDtypeStruct((rows, n), F32),
        compiler_params=_params("arbitrary"),
        name="modulation",
    )(cond.T, w_mod, b_mod.reshape(1, n))


def _rope_partner(t):
    lane = lax.broadcasted_iota(jnp.int32, t.shape, 1)
    first = (lane % 32) < 16
    return jnp.where(first, pltpu.roll(t, LANES - 16, 1), pltpu.roll(t, 16, 1))


def _inproj_kernel(x_ref, xp_ref, xn_ref, sh_ref, sc_ref, ng_ref, w_ref, cw_ref, cb_ref,
                   cosr_ref, sinr_ref, cosc_ref, sinc_ref,
                   xc_ref, gx_ref, q_ref, kv_ref, hsc, pslab, *, rope, sub, nt):
    i = pl.program_id(1)
    tm = x_ref.shape[0]
    halo = SUBLANES_BF16
    half = sub // 2
    base = halo - RNN_CONV_LEFT
    q0 = D_RNN
    k0 = q0 + D_ATTN
    v0 = k0 + D_KV
    scale = HEAD_DIM ** -0.5 * LOG2E
    nslab = D_RNN // LANES
    gain = ng_ref[...] * (1.0 + sc_ref[...])

    def modulated(x):
        ms = jnp.mean(x * x, axis=-1, keepdims=True)
        return (x * lax.rsqrt(ms + EPS) * gain + sh_ref[...]).astype(BF16)

    zeros = jnp.zeros((halo, hsc.shape[1]), BF16)
    hsc[0:halo] = jnp.where(i > 0, modulated(xp_ref[...]), zeros)
    hsc[halo + tm:] = jnp.where(i < nt - 1, modulated(xn_ref[...]), zeros)

    for n in range(tm // sub):
        hsc[halo + sub * n:halo + sub * (n + 1)] = modulated(x_ref[sub * n:sub * (n + 1), :])

    cw = cw_ref[...]
    cb = cb_ref[...]
    def project(n):
        pr = jnp.dot(hsc[sub * n:sub * (n + 1) + 2 * halo], w_ref[:, :D_RNN],
                     preferred_element_type=F32)
        p = jnp.dot(hsc[halo + sub * n:halo + sub * (n + 1)], w_ref[:, D_RNN:],
                    preferred_element_type=F32)
        return pr, p

    def epilogue(n, pr, p):
        rows = slice(sub * n, sub * (n + 1))
        cols = []
        for s in range(nslab):
            sl = slice(LANES * s, LANES * (s + 1))
            slab = pslab.at[n * nslab + s]
            slab[...] = pr[:, sl]
            taps = [slab[pl.ds(base + j, half, stride=2), :] for j in range(RNN_CONV_W + 1)]
            even = cb[:, sl]
            odd = cb[:, sl]
            for j in range(RNN_CONV_W):
                even = even + taps[j] * cw[j:j + 1, sl]
                odd = odd + taps[j + 1] * cw[j:j + 1, sl]
            cols.append(jnp.concatenate([even, odd], axis=0))
        xc_ref[rows, :] = jnp.concatenate(cols, axis=1)
        gx_ref[rows, :] = jax.nn.gelu(p[:, :D_RNN]).astype(BF16)
        cols = [p[:, q0 + LANES * c:q0 + LANES * (c + 1)] for c in range(D_ATTN // LANES)]
        k = p[:, k0:v0]
        if rope:
            per = sub // GRID_W
            trows = slice(per * n, per * (n + 1))
            expand = lambda r_ref, c_ref: (r_ref[trows, :][:, None, :]
                                           + c_ref[...][None, :, :]).reshape(sub, LANES)
            cos = expand(cosr_ref, cosc_ref)
            sin = expand(sinr_ref, sinc_ref)
            cols = [t * cos + _rope_partner(t) * sin for t in cols]
            k = k * cos + _rope_partner(k) * sin
        for c, t in enumerate(cols):
            q_ref[rows, LANES * c:LANES * (c + 1)] = (t * scale).astype(BF16)
        kv_ref[rows, :D_KV] = k.astype(BF16)
        kv_ref[rows, D_KV:] = p[:, v0:v0 + D_KV].astype(BF16)

    nsub = tm // sub
    ready = project(0)
    for n in range(nsub):
        following = project(n + 1) if n + 1 < nsub else None
        epilogue(n, *ready)
        ready = following


def _mod_spec(d, chunk, row=None):
    return pl.BlockSpec((None, 1, d), lambda bb, i: (bb if row is None else row, 0, chunk))


def _in_proj(x, mod, mod_row, norm_g, w_in, cw, cb, tables, *, rope, tm):
    b, s, d = x.shape
    n = w_in.shape[1]
    nt = s // tm
    sub = min(tm, SUB_PROJ)
    halo = SUBLANES_BF16
    per = tm // halo
    row = lambda bb, i: (bb, i, 0)
    prev = lambda bb, i: (bb, jnp.maximum(i * per - 1, 0), 0)
    nxt = lambda bb, i: (bb, jnp.minimum((i + 1) * per, s // halo - 1), 0)
    vec = lambda bb, i: (bb, 0, 0)
    cosr, sinr, cosc, sinc = tables
    if rope:
        assert tm % GRID_W == 0
        rtab = pl.BlockSpec((tm // GRID_W, LANES), lambda bb, i: (i, 0))
    else:
        rtab = _const_spec(cosr.shape)
    outs = [(D_RNN, F32), (D_RNN, BF16), (D_ATTN, BF16), (2 * D_KV, BF16)]
    return pl.pallas_call(
        functools.partial(_inproj_kernel, rope=rope, sub=sub, nt=nt),
        grid=(b, nt),
        in_specs=[pl.BlockSpec((None, tm, d), row),
                  pl.BlockSpec((None, halo, d), prev),
                  pl.BlockSpec((None, halo, d), nxt),
                  _mod_spec(d, 0, mod_row), _mod_spec(d, 1, mod_row), _const_spec(norm_g.shape),
                  _const_spec((d, n)), _const_spec(cw.shape), _const_spec(cb.shape),
                  rtab, rtab, _const_spec(cosc.shape), _const_spec(sinc.shape)],
        out_specs=[pl.BlockSpec((None, tm, w), row) for w, _ in outs],
        out_shape=[jax.ShapeDtypeStruct((b, s, w), dt) for w, dt in outs],
        scratch_shapes=[pltpu.VMEM((tm + 2 * halo, d), BF16),
                        pltpu.VMEM((tm // sub * (D_RNN // LANES), sub + 2 * halo, LANES), F32)],
        compiler_params=_params("arbitrary", "arbitrary"),
        name="in_proj_rope" if rope else "in_proj_ctx",
    )(x, x, x, mod, mod, norm_g, w_in, cw, cb, cosr, sinr, cosc, sinc)


def _rglru_kernel(*refs, reverse, final, blk):
    if final:
        (xc_ref, wg_ref, ba_ref, bi_ref, sp_ref, h0_ref, yo_ref, gx_ref, gn_ref,
         y_ref, hfin_ref, a_scr, u_scr, y_scr, h_scr) = refs
    else:
        (xc_ref, wg_ref, ba_ref, bi_ref, sp_ref, h0_ref,
         y_ref, hfin_ref, a_scr, u_scr, y_scr, h_scr) = refs
    i = pl.program_id(0)
    nb, tm, _ = xc_ref.shape

    @pl.when(i == 0)
    def _():
        h_scr[...] = h0_ref[...]

    for bb in range(nb):
        xc = xc_ref[bb]
        xb = xc.astype(BF16)
        for g in range(D_RNN // MXU_DIM):
            sl = slice(MXU_DIM * g, MXU_DIM * (g + 1))
            z = jnp.dot(xb[:, sl], wg_ref[g], preferred_element_type=F32)
            tr = jnp.tanh(z[:, :MXU_DIM] + ba_ref[:, sl])
            gi = 0.5 * jnp.tanh(z[:, MXU_DIM:] + bi_ref[:, sl]) + 0.5
            log_a = (tr + 1.0) * sp_ref[:, sl]
            a = jnp.exp(log_a)
            a_scr[bb, :, sl] = a
            w = -jnp.tanh(log_a) * (1.0 + a * a)
            u_scr[bb, :, sl] = w * lax.rsqrt(jnp.maximum(w, F32_TINY)) * (gi * xc[:, sl])

    rows = SUBLANES_F32
    half = blk // 2
    ngrp = half // rows
    order = range(rows - 1, -1, -1) if reverse else range(rows)
    parity = (1, 0) if reverse else (0, 1)
    spans = range(tm // blk - 1, -1, -1) if reverse else range(tm // blk)

    def group(g, hs, off):
        gg = (ngrp - 1 - g) if reverse else g
        lo = pl.multiple_of(off + gg * rows, rows)
        hi = pl.multiple_of(off + half + gg * rows, rows)
        out = pl.multiple_of(off + gg * 2 * rows, 2 * rows)
        hs = list(hs)
        for j in order:
            for par in parity:
                start = hi if par else lo
                for bb in range(nb):
                    av = a_scr.at[bb, pl.ds(start, rows), :]
                    uv = u_scr.at[bb, pl.ds(start, rows), :]
                    yv = y_scr.at[bb, pl.ds(out, 2 * rows), :]
                    hs[bb] = av[j:j + 1, :] * hs[bb] + uv[j:j + 1, :]
                    yv[2 * j + par:2 * j + par + 1, :] = hs[bb]
        return tuple(hs)

    hs = tuple(h_scr[bb] for bb in range(nb))
    for span in spans:
        hs = lax.fori_loop(0, ngrp, functools.partial(group, off=span * blk), hs)
    for bb in range(nb):
        h_scr[bb] = hs[bb]
        hfin_ref[bb] = hs[bb]

    if final:
        z = gx_ref[...] * (y_scr[...] + yo_ref[...])
        ms = jnp.mean(z * z, axis=-1, keepdims=True)
        y_ref[...] = (z * lax.rsqrt(ms + EPS) * gn_ref[...]).astype(BF16)
    else:
        y_ref[...] = y_scr[...]


def _rglru(xc, wg, ba, bi, sp, h0, *, reverse, tm, blk, final_args=None):
    b, s, d = xc.shape
    nt = s // tm
    final = final_args is not None
    row = (lambda i: (0, nt - 1 - i, 0)) if reverse else (lambda i: (0, i, 0))
    in_specs = [pl.BlockSpec((b, tm, d), row), _const_spec(wg.shape),
                _const_spec(ba.shape), _const_spec(bi.shape), _const_spec(sp.shape),
                _const_spec(h0.shape)]
    args = [xc, wg, ba, bi, sp, h0]
    if final:
        y_other, gx, gn = final_args
        in_specs += [pl.BlockSpec((b, tm, d), row), pl.BlockSpec((b, tm, d), row),
                     _const_spec(gn.shape)]
        args += [y_other, gx, gn]
    return pl.pallas_call(
        functools.partial(_rglru_kernel, reverse=reverse, final=final, blk=blk),
        grid=(nt,),
        in_specs=in_specs,
        out_specs=[pl.BlockSpec((b, tm, d), row), pl.BlockSpec((b, 1, d), lambda i: (0, 0, 0))],
        out_shape=[jax.ShapeDtypeStruct((b, s, d), BF16 if final else F32),
                   jax.ShapeDtypeStruct((b, 1, d), F32)],
        scratch_shapes=[pltpu.VMEM((b, tm, d), F32), pltpu.VMEM((b, tm, d), F32),
                        pltpu.VMEM((b, tm, d), F32), pltpu.VMEM((b, 1, d), F32)],
        compiler_params=_params("arbitrary"),
        name="rglru_" + ("bwd" if reverse else "fwd") + ("_final" if final else ""),
    )(*args)


def _attn_kernel(q_ref, kvp_ref, kvc_ref, kvn_ref, kvx_ref, bias_ref, sink_ref, gn_ref, o_ref,
                 keys, vals, keyx, valx, *, qb, nstep):
    ncol = D_ATTN // LANES
    nwin = 3 * BLOCK_Q
    step = pl.program_id(1)
    band = ((slice(0, BLOCK_Q), kvp_ref), (slice(BLOCK_Q, BLOCK_Q * (qb + 1)), kvc_ref),
            (slice(BLOCK_Q * (qb + 1), BLOCK_Q * (qb + 2)), kvn_ref))
    for rows, kv_ref in band:
        keys[rows, :] = kv_ref[:, :D_KV]
        vals[rows, :D_KV] = kv_ref[:, D_KV:]
    keyx[...] = kvx_ref[:, :D_KV]
    valx[:, :D_KV] = kvx_ref[:, D_KV:]
    vals[:, D_KV:] = jnp.ones((vals.shape[0], LANES), BF16)
    valx[:, D_KV:] = jnp.ones((valx.shape[0], LANES), BF16)

    low = lax.broadcasted_iota(jnp.int32, (BLOCK_Q, LANES), 1) < HEAD_DIM
    first = lax.broadcasted_iota(jnp.int32, (2 * BLOCK_Q, 1), 0) < BLOCK_Q
    zero = jnp.zeros((BLOCK_Q, LANES), BF16)
    nt_dims = (((1,), (1,)), ((), ()))
    def block_bias(x):
        variant = 1
        if x == 0:
            variant = jnp.where(step == 0, 0, variant)
        if x == qb - 1:
            variant = jnp.where(step == nstep - 1, 2, variant)
        return bias_ref[variant]

    def scores(x, c, bias):
        win = slice(BLOCK_Q * x, BLOCK_Q * x + nwin)
        t = q_ref[BLOCK_Q * x:BLOCK_Q * (x + 1), LANES * c:LANES * (c + 1)]
        qs = jnp.concatenate([jnp.where(low, t, zero), jnp.where(low, zero, t)], axis=0)
        s_loc = lax.dot_general(qs, keys[win, :], nt_dims, preferred_element_type=F32)
        s_ctx = lax.dot_general(qs, keyx[...], nt_dims, preferred_element_type=F32)
        s_loc = (s_loc.reshape(2, BLOCK_Q, nwin) + bias[None]).reshape(2 * BLOCK_Q, nwin)
        sink = jnp.where(first, sink_ref[c], sink_ref[ncol + c])
        m = jnp.maximum(jnp.maximum(jnp.max(s_loc, axis=-1, keepdims=True),
                                    jnp.max(s_ctx, axis=-1, keepdims=True)), sink)
        e_loc = jnp.exp2(s_loc - m).astype(BF16)
        e_ctx = jnp.exp2(s_ctx - m).astype(BF16)
        return x, e_loc, e_ctx, jnp.exp2(sink - m)

    def weighted(x, e_loc, e_ctx, e_sink):
        win = slice(BLOCK_Q * x, BLOCK_Q * x + nwin)
        pv = (jnp.dot(e_loc, vals[win, :], preferred_element_type=F32)
              + jnp.dot(e_ctx, valx[...], preferred_element_type=F32))
        o = pv[:, :D_KV] / (pv[:, D_KV:] + e_sink)
        return jnp.where(low, o[:BLOCK_Q], o[BLOCK_Q:])

    def finish(x, outs):
        ms = sum(jnp.sum(t * t, axis=-1, keepdims=True) for t in outs) * (1.0 / D_ATTN)
        inv = lax.rsqrt(ms + EPS)
        for c, t in enumerate(outs):
            sl = slice(LANES * c, LANES * (c + 1))
            o_ref[BLOCK_Q * x:BLOCK_Q * (x + 1), sl] = (t * inv * gn_ref[:, sl]).astype(BF16)

    pending = []
    outs = [[] for _ in range(qb)]

    def retire():
        chain = pending.pop(0)
        x = chain[0]
        outs[x].append(weighted(*chain))
        if len(outs[x]) == ncol:
            finish(x, outs[x])

    for x in range(qb):
        bias = block_bias(x)
        for c in range(ncol):
            pending.append(scores(x, c, bias))
            if len(pending) > ATTN_LOOKAHEAD:
                retire()
    while pending:
        retire()


def _attention(q, kv, kvx, bias, sink, gn):
    b, s, _ = q.shape
    qb = ATTN_QB
    nblk = s // BLOCK_Q
    nstep = nblk // qb
    lx = kvx.shape[1]
    cur = lambda bb, n: (bb, n, 0)
    prev = lambda bb, n: (bb, jnp.maximum(n * qb - 1, 0), 0)
    nxt = lambda bb, n: (bb, jnp.minimum((n + 1) * qb, nblk - 1), 0)
    halo = lambda im: pl.BlockSpec((None, BLOCK_Q, 2 * D_KV), im)
    return pl.pallas_call(
        functools.partial(_attn_kernel, qb=qb, nstep=nstep),
        grid=(b, nstep),
        in_specs=[pl.BlockSpec((None, qb * BLOCK_Q, D_ATTN), cur),
                  halo(prev), pl.BlockSpec((None, qb * BLOCK_Q, 2 * D_KV), cur), halo(nxt),
                  pl.BlockSpec((None, lx, 2 * D_KV), lambda bb, n: (bb, 0, 0)),
                  _const_spec(bias.shape), pl.BlockSpec(memory_space=pltpu.SMEM),
                  _const_spec(gn.shape)],
        out_specs=pl.BlockSpec((None, qb * BLOCK_Q, D_ATTN), cur),
        out_shape=jax.ShapeDtypeStruct((b, s, D_ATTN), BF16),
        scratch_shapes=[pltpu.VMEM(((qb + 2) * BLOCK_Q, D_KV), BF16),
                        pltpu.VMEM(((qb + 2) * BLOCK_Q, D_KV + LANES), BF16),
                        pltpu.VMEM((lx, D_KV), BF16),
                        pltpu.VMEM((lx, D_KV + LANES), BF16)],
        compiler_params=_params("arbitrary", "arbitrary"),
        name="attention",
    )(q, kv, kv, kv, kvx, bias, sink, gn)


def _attn_bias():
    i = np.arange(BLOCK_Q)[:, None]
    j = np.arange(3 * BLOCK_Q)[None, :]
    band = np.abs(i + BLOCK_Q - j) <= WINDOW
    variants = [band & (j >= BLOCK_Q), band, band & (j < 2 * BLOCK_Q)]
    return jnp.asarray(np.stack([np.where(ok, 0.0, NEG_INF) for ok in variants]).astype(np.float32))


def _ffn_kernel(rn_ref, rnp_ref, rnn_ref, an_ref, anp_ref, ann_ref, x_ref, xp_ref, xn_ref,
                wo_ref, g1_ref, sh_ref, sc_ref, ng_ref, wu_ref, cw_ref, cb_ref, wd_ref, g2_ref, fg_ref,
                o_ref, mix, x1s, hext, ug, uv, act, acc, nat, *, nt, tm):
    d_ff = wd_ref.shape[0]
    nc = d_ff // FFN_CHUNK
    span = lambda j, branch: slice(branch * d_ff + j * FFN_CHUNK, branch * d_ff + (j + 1) * FFN_CHUNK)
    i = pl.program_id(1)
    halo = SUBLANES_BF16
    body = slice(halo, halo + tm)
    tail = slice(halo + tm, 2 * halo + tm)
    for rows, r_ref, a_ref in ((slice(0, halo), rnp_ref, anp_ref), (body, rn_ref, an_ref),
                               (tail, rnn_ref, ann_ref)):
        mix[rows, :D_RNN] = r_ref[...]
        mix[rows, D_RNN:] = a_ref[...]
    proj = jnp.dot(mix[...], wo_ref[...], preferred_element_type=F32)

    def residual(xin_ref, rows):
        return xin_ref[...] + g1_ref[...] * proj[rows]

    gain = ng_ref[...] * (1.0 + sc_ref[...])

    def modulated(x1):
        ms = jnp.mean(x1 * x1, axis=-1, keepdims=True)
        return (x1 * lax.rsqrt(ms + EPS) * gain + sh_ref[...]).astype(BF16)

    x1 = residual(x_ref, body)
    x1s[...] = x1
    hext[body] = modulated(x1)
    zeros = jnp.zeros((halo, hext.shape[1]), BF16)
    hext[0:halo] = jnp.where(i > 0, modulated(residual(xp_ref, slice(0, halo))), zeros)
    hext[tail] = jnp.where(i < nt - 1, modulated(residual(xn_ref, tail)), zeros)

    base = halo - FFN_CONV_LEFT

    half = tm // 2
    nslab = FFN_CHUNK // LANES

    def up(j, slot):
        he = hext[...]
        g = jnp.dot(he, wu_ref[:, span(j,0)], preferred_element_type=F32)
        v = jnp.dot(he, wu_ref[:, span(j,1)], preferred_element_type=F32)
        for s in range(nslab):
            ug[slot, s] = g[:, LANES * s:LANES * (s + 1)]
            uv[slot, s] = v[:, LANES * s:LANES * (s + 1)]

    def conv(u, slot, cw, cb):
        cols = []
        for s in range(nslab):
            sl = slice(LANES * s, LANES * (s + 1))
            taps = [u[slot, s, pl.ds(base + t, half, stride=2), :] for t in range(FFN_CONV_W + 1)]
            even = cb[:, sl]
            odd = cb[:, sl]
            for t in range(FFN_CONV_W):
                even = even + taps[t] * cw[t:t + 1, sl]
                odd = odd + taps[t + 1] * cw[t:t + 1, sl]
            cols.append(jnp.concatenate([even, odd], axis=0))
        return jnp.concatenate(cols, axis=1)

    def gate(j, slot):
        hg = 0.5 * conv(ug, slot, cw_ref[:, span(j,0)], cb_ref[:, span(j,0)])
        silu = hg * jnp.tanh(hg) + hg
        val = conv(uv, slot, cw_ref[:, span(j,1)], cb_ref[:, span(j,1)])
        act[slot] = (silu * val).astype(BF16)

    def down(j, slot):
        return jnp.dot(act[slot], wd_ref[span(j,0), :], preferred_element_type=F32)

    up(0, 0)
    for j in range(nc):
        if j + 1 < nc:
            up(j + 1, (j + 1) % 2)
        gate(j, j % 2)
        if j == 1:
            acc[...] = down(0, 0)
        elif j > 1:
            acc[...] += down(j - 1, (j - 1) % 2)
    mixed = acc[...] + down(nc - 1, (nc - 1) % 2)
    nlane = mixed.shape[1] // LANES
    for s in range(nlane):
        sl = slice(LANES * s, LANES * (s + 1))
        nat[s, pl.ds(0, half, stride=2), :] = mixed[:half, sl]
        nat[s, pl.ds(1, half, stride=2), :] = mixed[half:, sl]
    ffn = jnp.concatenate([nat[s] for s in range(nlane)], axis=1)
    y = x1s[...] + g2_ref[...] * ffn
    ms = jnp.mean(y * y, axis=-1, keepdims=True)
    o_ref[...] = y * lax.rsqrt(ms + EPS) * fg_ref[...]


def _mixer_out_ffn(rn, an, x, wo, mod, norm_g, wu, cw, cb, wd, fg, *, tm):
    b, s, d = x.shape
    nt = s // tm
    assert wd.shape[0] % FFN_CHUNK == 0 and (wd.shape[0] // FFN_CHUNK) % 2 == 1
    halo = SUBLANES_BF16
    per = tm // halo
    nblk = s // halo
    row = lambda bb, i: (bb, i, 0)
    prev = lambda bb, i: (bb, jnp.maximum(i * per - 1, 0), 0)
    nxt = lambda bb, i: (bb, jnp.minimum((i + 1) * per, nblk - 1), 0)
    banded = lambda w: [pl.BlockSpec((None, tm, w), row), pl.BlockSpec((None, halo, w), prev),
                        pl.BlockSpec((None, halo, w), nxt)]
    return pl.pallas_call(
        functools.partial(_ffn_kernel, nt=nt, tm=tm),
        grid=(b, nt),
        in_specs=banded(D_RNN) + banded(D_ATTN) + banded(d)
        + [_const_spec(wo.shape), _mod_spec(d, 2), _mod_spec(d, 3), _mod_spec(d, 4),
           _const_spec(norm_g.shape),
           _const_spec(wu.shape), _const_spec(cw.shape), _const_spec(cb.shape),
           _const_spec(wd.shape), _mod_spec(d, 5), _const_spec(fg.shape)],
        out_specs=pl.BlockSpec((None, tm, d), row),
        out_shape=jax.ShapeDtypeStruct((b, s, d), F32),
        scratch_shapes=[pltpu.VMEM((tm + 2 * halo, D_RNN + D_ATTN), BF16),
                        pltpu.VMEM((tm, d), F32),
                        pltpu.VMEM((tm + 2 * halo, d), BF16),
                        pltpu.VMEM((2, FFN_CHUNK // LANES, tm + 2 * halo, LANES), F32),
                        pltpu.VMEM((2, FFN_CHUNK // LANES, tm + 2 * halo, LANES), F32),
                        pltpu.VMEM((2, tm, FFN_CHUNK), BF16),
                        pltpu.VMEM((tm, d), F32),
                        pltpu.VMEM((d // LANES, tm, LANES), F32)],
        compiler_params=_params("arbitrary", "arbitrary"),
        name="out_proj_conv_ffn",
    )(rn, rn, rn, an, an, an, x, x, x, wo, mod, mod, mod, norm_g, wu, cw, cb, wd, mod, fg)


def _rope_tables(s):
    half = HEAD_DIM // 4
    inv = ROPE_THETA ** (-np.arange(half, dtype=np.float64) / half)

    def tables(npos, lanes_first):
        ang = np.arange(npos, dtype=np.float64)[:, None] * inv[None, :]
        zero = np.zeros((npos, 2 * half))
        cos = np.concatenate([np.cos(ang)] * 2, axis=1)
        sin = np.concatenate([-np.sin(ang), np.sin(ang)], axis=1)
        order = (lambda t: [t, zero]) if lanes_first else (lambda t: [zero, t])
        reps = LANES // HEAD_DIM
        return [jnp.asarray(np.tile(np.concatenate(order(t), axis=1), (1, reps)).astype(np.float32))
                for t in (cos, sin)]

    cosr, sinr = tables(s // GRID_W, True)
    cosc, sinc = tables(GRID_W, False)
    return cosr, sinr, cosc, sinc


def _gate_weights(w_a, w_i):
    per = MXU_DIM // RNN_BLOCK_W
    eye = jnp.eye(per, dtype=F32)

    def dense(w):
        w = w.reshape(RNN_BLOCKS // per, per, RNN_BLOCK_W, RNN_BLOCK_W)
        m = w[:, :, :, None, :] * eye[None, :, None, :, None]
        return m.reshape(RNN_BLOCKS // per, MXU_DIM, MXU_DIM)

    return (0.5 * jnp.concatenate([dense(w_a), dense(w_i)], axis=2)).astype(BF16)


def _interleave_heads(w, axis):
    shape = w.shape
    w = w.reshape(shape[:axis] + (N_KV_HEADS, Q_PER_KV, HEAD_DIM) + shape[axis + 1:])
    return jnp.swapaxes(w, axis, axis + 1).reshape(shape)


def kernel(x, c, ctx, c_ctx, w_mod, b_mod, norm1_g, w_in, rnn_conv_w, rnn_conv_b, lru_w_a, lru_b_a,
           lru_w_i, lru_b_i, lru_lam, attn_sink, gn_rnn, gn_attn, w_out, norm2_g, w_up, ffn_conv_w,
           ffn_conv_b, w_down, final_g):
    assert w_mod.shape[0] == 1, "one layer: the last layer's context outputs are never consumed"
    b, s, d = x.shape
    lx = ctx.shape[1]

    cond = jnp.concatenate([c, c_ctx[None], jnp.zeros((SUBLANES_F32 - b - 1, d), F32)], axis=0)
    mod = _modulation(cond, b + 1, w_mod[0], b_mod[0])[:, None, :]

    wi = w_in[0].astype(BF16)
    q0 = 2 * D_RNN
    wi = jnp.concatenate([wi[:, :q0], _interleave_heads(wi[:, q0:q0 + D_ATTN], 1),
                          wi[:, q0 + D_ATTN:]], axis=1)
    wo = w_out[0].astype(BF16)
    wo = jnp.concatenate([wo[:D_RNN], _interleave_heads(wo[D_RNN:], 0)], axis=0)
    gn_a = _interleave_heads(gn_attn[0], 0)[None]
    gn_r = gn_rnn[0][None]
    cw = rnn_conv_w[0]
    cb = rnn_conv_b[0][None]
    sp = (-0.5 * LRU_C) * jax.nn.softplus(-lru_lam[0])

    tables = _rope_tables(s)
    xc, gx, q, kv = _in_proj(x, mod, None, norm1_g, wi, cw, cb, tables, rope=True, tm=TM_PROJ)
    xcc, _, _, kvx = _in_proj(ctx, mod, b, norm1_g, wi, cw, cb, tables, rope=False, tm=lx)

    h0 = jnp.zeros((b, 1, D_RNN), F32)
    y_dir = None
    for dr in range(2):
        wg = _gate_weights(lru_w_a[0, dr], lru_w_i[0, dr])
        gate_args = (wg, 0.5 * lru_b_a[0, dr][None], 0.5 * lru_b_i[0, dr][None], sp[dr][None])
        rev = dr == 1
        _, h_ctx = _rglru(xcc, *gate_args, h0, reverse=rev, tm=lx, blk=min(lx, SUB_PROJ))
        fin = (y_dir, gx, gn_r) if rev else None
        y_dir, _ = _rglru(xc, *gate_args, h_ctx, reverse=rev, tm=TM_SCAN, blk=SUB_PROJ,
                          final_args=fin)
    rn = y_dir

    an = _attention(q, kv, kvx, _attn_bias(), attn_sink[0] * LOG2E, gn_a)

    return _mixer_out_ffn(rn, an, x, wo, mod, norm2_g, w_up[0].astype(BF16), ffn_conv_w[0],
                          ffn_conv_b[0][None], w_down[0].astype(BF16), final_g[None], tm=TM_FFN)
```

```python
import functools

import jax
import jax.numpy as jnp
import numpy as np
from jax import lax
from jax.experimental import pallas as pl
from jax.experimental.pallas import tpu as pltpu

F32 = jnp.float32
BF16 = jnp.bfloat16

EPS = 1e-6
GRID_W = 64
D_RNN = 512
RNN_BLOCKS = 8
RNN_BLOCK_W = D_RNN // RNN_BLOCKS
RNN_CONV_W = 4
RNN_CONV_LEFT = 2
LRU_C = 8.0
HEAD_DIM = 64
N_Q_HEADS = 8
N_KV_HEADS = 2
Q_PER_KV = N_Q_HEADS // N_KV_HEADS
D_ATTN = N_Q_HEADS * HEAD_DIM
D_KV = N_KV_HEADS * HEAD_DIM
WINDOW = 128
BLOCK_Q = 128
ROPE_THETA = 10000.0
NEG_INF = -1e30
LOG2E = 1.4426950408889634
F32_TINY = 2.0 ** -126
FFN_CONV_W = 3
FFN_CONV_LEFT = 1

LANES = 128
SUBLANES_F32 = 8
SUBLANES_BF16 = 16
MXU_DIM = 256
VMEM_LIMIT = 56 * 1024 * 1024

TM_PROJ = 2048
SUB_PROJ = 512
TM_SCAN = 1024
TM_FFN = 512
FFN_CHUNK = 256
ATTN_QB = 16
ATTN_LOOKAHEAD = 1


def _params(*sem):
    return pltpu.CompilerParams(dimension_semantics=sem, vmem_limit_bytes=VMEM_LIMIT)


def _const_spec(shape):
    zeros = (0,) * len(shape)
    return pl.BlockSpec(shape, lambda *_: zeros, pipeline_mode=pl.Buffered(1))


def _mod_kernel(condt_ref, w_ref, b_ref, o_ref, *, nrows):
    st = condt_ref[...]
    st = st * jax.nn.sigmoid(st)
    w = w_ref[...]
    rows = [jnp.sum(w * st[:, r:r + 1], axis=0, keepdims=True) for r in range(nrows)]
    rows.append(jnp.zeros((o_ref.shape[0] - nrows, w.shape[1]), F32))
    o_ref[...] = jnp.concatenate(rows, axis=0) + b_ref[...]


def _modulation(cond, nrows, w_mod, b_mod):
    rows, d = cond.shape
    n = w_mod.shape[1]
    tn = 768
    return pl.pallas_call(
        functools.partial(_mod_kernel, nrows=nrows),
        grid=(n // tn,),
        in_specs=[pl.BlockSpec((d, rows), lambda j: (0, 0)),
                  pl.BlockSpec((d, tn), lambda j: (0, j)),
                  pl.BlockSpec((1, tn), lambda j: (0, j))],
        out_specs=pl.BlockSpec((rows, tn), lambda j: (0, j)),
        out_shape=jax.ShapeDtypeStruct((rows, n), F32),
        compiler_params=_params("arbitrary"),
        name="modulation",
    )(cond.T, w_mod, b_mod.reshape(1, n))


def _rope_partner(t):
    lane = lax.broadcasted_iota(jnp.int32, t.shape, 1)
    first = (lane % 32) < 16
    return jnp.where(first, pltpu.roll(t, LANES - 16, 1), pltpu.roll(t, 16, 1))


def _inproj_kernel(x_ref, xp_ref, xn_ref, sh_ref, sc_ref, ng_ref, w_ref, cw_ref, cb_ref,
                   cosr_ref, sinr_ref, cosc_ref, sinc_ref,
                   xc_ref, gx_ref, q_ref, kv_ref, hsc, pslab, *, rope, sub, nt):
    i = pl.program_id(1)
    tm = x_ref.shape[0]
    halo = SUBLANES_BF16
    half = sub // 2
    base = halo - RNN_CONV_LEFT
    q0 = D_RNN
    k0 = q0 + D_ATTN
    v0 = k0 + D_KV
    scale = HEAD_DIM ** -0.5 * LOG2E
    nslab = D_RNN // LANES
    gain = ng_ref[...] * (1.0 + sc_ref[...])

    def modulated(x):
        ms = jnp.mean(x * x, axis=-1, keepdims=True)
        return (x * lax.rsqrt(ms + EPS) * gain + sh_ref[...]).astype(BF16)

    zeros = jnp.zeros((halo, hsc.shape[1]), BF16)
    hsc[0:halo] = jnp.where(i > 0, modulated(xp_ref[...]), zeros)
    hsc[halo + tm:] = jnp.where(i < nt - 1, modulated(xn_ref[...]), zeros)

    for n in range(tm // sub):
        hsc[halo + sub * n:halo + sub * (n + 1)] = modulated(x_ref[sub * n:sub * (n + 1), :])

    cw = cw_ref[...]
    cb = cb_ref[...]
    def project(n):
        pr = jnp.dot(hsc[sub * n:sub * (n + 1) + 2 * halo], w_ref[:, :D_RNN],
                     preferred_element_type=F32)
        p = jnp.dot(hsc[halo + sub * n:halo + sub * (n + 1)], w_ref[:, D_RNN:],
                    preferred_element_type=F32)
        return pr, p

    def epilogue(n, pr, p):
        rows = slice(sub * n, sub * (n + 1))
        cols = []
        for s in range(nslab):
            sl = slice(LANES * s, LANES * (s + 1))
            slab = pslab.at[n * nslab + s]
            slab[...] = pr[:, sl]
            taps = [slab[pl.ds(base + j, half, stride=2), :] for j in range(RNN_CONV_W + 1)]
            even = cb[:, sl]
            odd = cb[:, sl]
            for j in range(RNN_CONV_W):
                even = even + taps[j] * cw[j:j + 1, sl]
                odd = odd + taps[j + 1] * cw[j:j + 1, sl]
            cols.append(jnp.concatenate([even, odd], axis=0))
        xc_ref[rows, :] = jnp.concatenate(cols, axis=1)
        gx_ref[rows, :] = jax.nn.gelu(p[:, :D_RNN]).astype(BF16)
        cols = [p[:, q0 + LANES * c:q0 + LANES * (c + 1)] for c in range(D_ATTN // LANES)]
        k = p[:, k0:v0]
        if rope:
            per = sub // GRID_W
            trows = slice(per * n, per * (n + 1))
            expand = lambda r_ref, c_ref: (r_ref[trows, :][:, None, :]
                                           + c_ref[...][None, :, :]).reshape(sub, LANES)
            cos = expand(cosr_ref, cosc_ref)
            sin = expand(sinr_ref, sinc_ref)
            cols = [t * cos + _rope_partner(t) * sin for t in cols]
            k = k * cos + _rope_partner(k) * sin
        for c, t in enumerate(cols):
            q_ref[rows, LANES * c:LANES * (c + 1)] = (t * scale).astype(BF16)
        kv_ref[rows, :D_KV] = k.astype(BF16)
        kv_ref[rows, D_KV:] = p[:, v0:v0 + D_KV].astype(BF16)

    nsub = tm // sub
    ready = project(0)
    for n in range(nsub):
        following = project(n + 1) if n + 1 < nsub else None
        epilogue(n, *ready)
        ready = following


def _mod_spec(d, chunk, row=None):
    return pl.BlockSpec((None, 1, d), lambda bb, i: (bb if row is None else row, 0, chunk))


def _in_proj(x, mod, mod_row, norm_g, w_in, cw, cb, tables, *, rope, tm):
    b, s, d = x.shape
    n = w_in.shape[1]
    nt = s // tm
    sub = min(tm, SUB_PROJ)
    halo = SUBLANES_BF16
    per = tm // halo
    row = lambda bb, i: (bb, i, 0)
    prev = lambda bb, i: (bb, jnp.maximum(i * per - 1, 0), 0)
    nxt = lambda bb, i: (bb, jnp.minimum((i + 1) * per, s // halo - 1), 0)
    cosr, sinr, cosc, sinc = tables
    if rope:
        assert tm % GRID_W == 0
        rtab = pl.BlockSpec((tm // GRID_W, LANES), lambda bb, i: (i, 0))
    else:
        rtab = _const_spec(cosr.shape)
    outs = [(D_RNN, F32), (D_RNN, BF16), (D_ATTN, BF16), (2 * D_KV, BF16)]
    return pl.pallas_call(
        functools.partial(_inproj_kernel, rope=rope, sub=sub, nt=nt),
        grid=(b, nt),
        in_specs=[pl.BlockSpec((None, tm, d), row),
                  pl.BlockSpec((None, halo, d), prev),
                  pl.BlockSpec((None, halo, d), nxt),
                  _mod_spec(d, 0, mod_row), _mod_spec(d, 1, mod_row), _const_spec(norm_g.shape),
                  _const_spec((d, n)), _const_spec(cw.shape), _const_spec(cb.shape),
                  rtab, rtab, _const_spec(cosc.shape), _const_spec(sinc.shape)],
        out_specs=[pl.BlockSpec((None, tm, w), row) for w, _ in outs],
        out_shape=[jax.ShapeDtypeStruct((b, s, w), dt) for w, dt in outs],
        scratch_shapes=[pltpu.VMEM((tm + 2 * halo, d), BF16),
                        pltpu.VMEM((tm // sub * (D_RNN // LANES), sub + 2 * halo, LANES), F32)],
        compiler_params=_params("arbitrary", "arbitrary"),
        name="in_proj_rope" if rope else "in_proj_ctx",
    )(x, x, x, mod, mod, norm_g, w_in, cw, cb, cosr, sinr, cosc, sinc)


def _rglru_kernel(*refs, reverse, final, blk):
    if final:
        (xc_ref, wg_ref, ba_ref, bi_ref, sp_ref, h0_ref, yo_ref, gx_ref, gn_ref,
         y_ref, hfin_ref, a_scr, u_scr, y_scr, h_scr) = refs
    else:
        (xc_ref, wg_ref, ba_ref, bi_ref, sp_ref, h0_ref,
         y_ref, hfin_ref, a_scr, u_scr, y_scr, h_scr) = refs
    i = pl.program_id(0)
    nb, tm, _ = xc_ref.shape

    @pl.when(i == 0)
    def _():
        h_scr[...] = h0_ref[...]

    for bb in range(nb):
        xc = xc_ref[bb]
        xb = xc.astype(BF16)
        for g in range(D_RNN // MXU_DIM):
            sl = slice(MXU_DIM * g, MXU_DIM * (g + 1))
            z = jnp.dot(xb[:, sl], wg_ref[g], preferred_element_type=F32)
            tr = jnp.tanh(z[:, :MXU_DIM] + ba_ref[:, sl])
            ti = jnp.tanh(z[:, MXU_DIM:] + bi_ref[:, sl])
            nl = (tr + 1.0) * sp_ref[:, sl]
            a = jnp.exp2(nl * (-LOG2E))
            a_scr[bb, :, sl] = a
            w = jnp.tanh(nl) * (1.0 + a * a)
            u_scr[bb, :, sl] = w * lax.rsqrt(jnp.maximum(w, F32_TINY)) * ((ti + 1.0) * xc[:, sl])

    rows = SUBLANES_F32
    half = blk // 2
    ngrp = half // rows
    order = range(rows - 1, -1, -1) if reverse else range(rows)
    parity = (1, 0) if reverse else (0, 1)
    spans = range(tm // blk - 1, -1, -1) if reverse else range(tm // blk)

    def group(g, hs, off):
        gg = (ngrp - 1 - g) if reverse else g
        lo = pl.multiple_of(off + gg * rows, rows)
        hi = pl.multiple_of(off + half + gg * rows, rows)
        out = pl.multiple_of(off + gg * 2 * rows, 2 * rows)
        hs = list(hs)
        for j in order:
            for par in parity:
                start = hi if par else lo
                for bb in range(nb):
                    av = a_scr.at[bb, pl.ds(start, rows), :]
                    uv = u_scr.at[bb, pl.ds(start, rows), :]
                    yv = y_scr.at[bb, pl.ds(out, 2 * rows), :]
                    hs[bb] = av[j:j + 1, :] * hs[bb] + uv[j:j + 1, :]
                    yv[2 * j + par:2 * j + par + 1, :] = hs[bb]
        return tuple(hs)

    hs = tuple(h_scr[bb] for bb in range(nb))
    for span in spans:
        hs = lax.fori_loop(0, ngrp, functools.partial(group, off=span * blk), hs)
    for bb in range(nb):
        h_scr[bb] = hs[bb]
        hfin_ref[bb] = hs[bb]

    if final:
        z = gx_ref[...] * (y_scr[...] + yo_ref[...])
        ms = jnp.mean(z * z, axis=-1, keepdims=True)
        y_ref[...] = (z * lax.rsqrt(ms + EPS) * gn_ref[...]).astype(BF16)
    else:
        y_ref[...] = y_scr[...]


def _rglru(xc, wg, ba, bi, sp, h0, *, reverse, tm, blk, final_args=None):
    b, s, d = xc.shape
    nt = s // tm
    final = final_args is not None
    row = (lambda i: (0, nt - 1 - i, 0)) if reverse else (lambda i: (0, i, 0))
    in_specs = [pl.BlockSpec((b, tm, d), row), _const_spec(wg.shape),
                _const_spec(ba.shape), _const_spec(bi.shape), _const_spec(sp.shape),
                _const_spec(h0.shape)]
    args = [xc, wg, ba, bi, sp, h0]
    if final:
        y_other, gx, gn = final_args
        in_specs += [pl.BlockSpec((b, tm, d), row), pl.BlockSpec((b, tm, d), row),
                     _const_spec(gn.shape)]
        args += [y_other, gx, gn]
    return pl.pallas_call(
        functools.partial(_rglru_kernel, reverse=reverse, final=final, blk=blk),
        grid=(nt,),
        in_specs=in_specs,
        out_specs=[pl.BlockSpec((b, tm, d), row), pl.BlockSpec((b, 1, d), lambda i: (0, 0, 0))],
        out_shape=[jax.ShapeDtypeStruct((b, s, d), BF16 if final else F32),
                   jax.ShapeDtypeStruct((b, 1, d), F32)],
        scratch_shapes=[pltpu.VMEM((b, tm, d), F32), pltpu.VMEM((b, tm, d), F32),
                        pltpu.VMEM((b, tm, d), F32), pltpu.VMEM((b, 1, d), F32)],
        compiler_params=_params("arbitrary"),
        name="rglru_" + ("bwd" if reverse else "fwd") + ("_final" if final else ""),
    )(*args)


def _attn_kernel(q_ref, kvp_ref, kvc_ref, kvn_ref, kvx_ref, bias_ref, sink_ref, gn_ref, o_ref,
                 keys, vals, keyx, valx, *, qb, nstep):
    ncol = D_ATTN // LANES
    nwin = 3 * BLOCK_Q
    step = pl.program_id(1)
    band = ((slice(0, BLOCK_Q), kvp_ref), (slice(BLOCK_Q, BLOCK_Q * (qb + 1)), kvc_ref),
            (slice(BLOCK_Q * (qb + 1), BLOCK_Q * (qb + 2)), kvn_ref))
    for rows, kv_ref in band:
        keys[rows, :] = kv_ref[:, :D_KV]
        vals[rows, :D_KV] = kv_ref[:, D_KV:]
    keyx[...] = kvx_ref[:, :D_KV]
    valx[:, :D_KV] = kvx_ref[:, D_KV:]
    vals[:, D_KV:] = jnp.ones((vals.shape[0], LANES), BF16)
    valx[:, D_KV:] = jnp.ones((valx.shape[0], LANES), BF16)

    low = lax.broadcasted_iota(jnp.int32, (BLOCK_Q, LANES), 1) < HEAD_DIM
    first = lax.broadcasted_iota(jnp.int32, (2 * BLOCK_Q, 1), 0) < BLOCK_Q
    zero = jnp.zeros((BLOCK_Q, LANES), BF16)
    nt_dims = (((1,), (1,)), ((), ()))
    def block_bias(x):
        variant = 1
        if x == 0:
            variant = jnp.where(step == 0, 0, variant)
        if x == qb - 1:
            variant = jnp.where(step == nstep - 1, 2, variant)
        return bias_ref[variant]

    def scores(x, c, bias):
        win = slice(BLOCK_Q * x, BLOCK_Q * x + nwin)
        t = q_ref[BLOCK_Q * x:BLOCK_Q * (x + 1), LANES * c:LANES * (c + 1)]
        qs = jnp.concatenate([jnp.where(low, t, zero), jnp.where(low, zero, t)], axis=0)
        s_loc = lax.dot_general(qs, keys[win, :], nt_dims, preferred_element_type=F32)
        s_ctx = lax.dot_general(qs, keyx[...], nt_dims, preferred_element_type=F32)
        s_loc = (s_loc.reshape(2, BLOCK_Q, nwin) + bias[None]).reshape(2 * BLOCK_Q, nwin)
        sink = jnp.where(first, sink_ref[c], sink_ref[ncol + c])
        m = jnp.maximum(jnp.maximum(jnp.max(s_loc, axis=-1, keepdims=True),
                                    jnp.max(s_ctx, axis=-1, keepdims=True)), sink)
        e_loc = jnp.exp2(s_loc - m).astype(BF16)
        e_ctx = jnp.exp2(s_ctx - m).astype(BF16)
        return x, e_loc, e_ctx, jnp.exp2(sink - m)

    def weighted(x, e_loc, e_ctx, e_sink):
        win = slice(BLOCK_Q * x, BLOCK_Q * x + nwin)
        pv = (jnp.dot(e_loc, vals[win, :], preferred_element_type=F32)
              + jnp.dot(e_ctx, valx[...], preferred_element_type=F32))
        o = pv[:, :D_KV] / (pv[:, D_KV:] + e_sink)
        return jnp.where(low, o[:BLOCK_Q], o[BLOCK_Q:])

    def finish(x, outs):
        ms = sum(jnp.sum(t * t, axis=-1, keepdims=True) for t in outs) * (1.0 / D_ATTN)
        inv = lax.rsqrt(ms + EPS)
        for c, t in enumerate(outs):
            sl = slice(LANES * c, LANES * (c + 1))
            o_ref[BLOCK_Q * x:BLOCK_Q * (x + 1), sl] = (t * inv * gn_ref[:, sl]).astype(BF16)

    pending = []
    outs = [[] for _ in range(qb)]

    def retire():
        chain = pending.pop(0)
        x = chain[0]
        outs[x].append(weighted(*chain))
        if len(outs[x]) == ncol:
            finish(x, outs[x])

    for x in range(qb):
        bias = block_bias(x)
        for c in range(ncol):
            pending.append(scores(x, c, bias))
            if len(pending) > ATTN_LOOKAHEAD:
                retire()
    while pending:
        retire()


def _attention(q, kv, kvx, bias, sink, gn):
    b, s, _ = q.shape
    qb = ATTN_QB
    nblk = s // BLOCK_Q
    nstep = nblk // qb
    lx = kvx.shape[1]
    cur = lambda bb, n: (bb, n, 0)
    prev = lambda bb, n: (bb, jnp.maximum(n * qb - 1, 0), 0)
    nxt = lambda bb, n: (bb, jnp.minimum((n + 1) * qb, nblk - 1), 0)
    halo = lambda im: pl.BlockSpec((None, BLOCK_Q, 2 * D_KV), im)
    return pl.pallas_call(
        functools.partial(_attn_kernel, qb=qb, nstep=nstep),
        grid=(b, nstep),
        in_specs=[pl.BlockSpec((None, qb * BLOCK_Q, D_ATTN), cur),
                  halo(prev), pl.BlockSpec((None, qb * BLOCK_Q, 2 * D_KV), cur), halo(nxt),
                  pl.BlockSpec((None, lx, 2 * D_KV), lambda bb, n: (bb, 0, 0)),
                  _const_spec(bias.shape), pl.BlockSpec(memory_space=pltpu.SMEM),
                  _const_spec(gn.shape)],
        out_specs=pl.BlockSpec((None, qb * BLOCK_Q, D_ATTN), cur),
        out_shape=jax.ShapeDtypeStruct((b, s, D_ATTN), BF16),
        scratch_shapes=[pltpu.VMEM(((qb + 2) * BLOCK_Q, D_KV), BF16),
                        pltpu.VMEM(((qb + 2) * BLOCK_Q, D_KV + LANES), BF16),
                        pltpu.VMEM((lx, D_KV), BF16),
                        pltpu.VMEM((lx, D_KV + LANES), BF16)],
        compiler_params=_params("arbitrary", "arbitrary"),
        name="attention",
    )(q, kv, kv, kv, kvx, bias, sink, gn)


def _attn_bias():
    i = np.arange(BLOCK_Q)[:, None]
    j = np.arange(3 * BLOCK_Q)[None, :]
    band = np.abs(i + BLOCK_Q - j) <= WINDOW
    variants = [band & (j >= BLOCK_Q), band, band & (j < 2 * BLOCK_Q)]
    return jnp.asarray(np.stack([np.where(ok, 0.0, NEG_INF) for ok in variants]).astype(np.float32))


def _ffn_kernel(rn_ref, rnp_ref, rnn_ref, an_ref, anp_ref, ann_ref, x_ref, xp_ref, xn_ref,
                wo_ref, g1_ref, sh_ref, sc_ref, ng_ref, wu_ref, cw_ref, cb_ref, wd_ref, g2_ref, fg_ref,
                o_ref, mix, x1s, hext, ug, uv, act, acc, nat, *, nt, tm):
    d_ff = wd_ref.shape[0]
    nc = d_ff // FFN_CHUNK
    span = lambda j, branch: slice(branch * d_ff + j * FFN_CHUNK, branch * d_ff + (j + 1) * FFN_CHUNK)
    i = pl.program_id(1)
    halo = SUBLANES_BF16
    body = slice(halo, halo + tm)
    tail = slice(halo + tm, 2 * halo + tm)
    for rows, r_ref, a_ref in ((slice(0, halo), rnp_ref, anp_ref), (body, rn_ref, an_ref),
                               (tail, rnn_ref, ann_ref)):
        mix[rows, :D_RNN] = r_ref[...]
        mix[rows, D_RNN:] = a_ref[...]
    proj = jnp.dot(mix[...], wo_ref[...], preferred_element_type=F32)

    def residual(xin_ref, rows):
        return xin_ref[...] + g1_ref[...] * proj[rows]

    gain = ng_ref[...] * (1.0 + sc_ref[...])

    def modulated(x1):
        ms = jnp.mean(x1 * x1, axis=-1, keepdims=True)
        return (x1 * lax.rsqrt(ms + EPS) * gain + sh_ref[...]).astype(BF16)

    x1 = residual(x_ref, body)
    x1s[...] = x1
    hext[body] = modulated(x1)
    zeros = jnp.zeros((halo, hext.shape[1]), BF16)
    hext[0:halo] = jnp.where(i > 0, modulated(residual(xp_ref, slice(0, halo))), zeros)
    hext[tail] = jnp.where(i < nt - 1, modulated(residual(xn_ref, tail)), zeros)

    base = halo - FFN_CONV_LEFT

    half = tm // 2
    nslab = FFN_CHUNK // LANES

    def up(j, slot):
        he = hext[...]
        g = jnp.dot(he, wu_ref[:, span(j,0)], preferred_element_type=F32)
        v = jnp.dot(he, wu_ref[:, span(j,1)], preferred_element_type=F32)
        for s in range(nslab):
            ug[slot, s] = g[:, LANES * s:LANES * (s + 1)]
            uv[slot, s] = v[:, LANES * s:LANES * (s + 1)]

    def conv(u, slot, cw, cb):
        cols = []
        for s in range(nslab):
            sl = slice(LANES * s, LANES * (s + 1))
            taps = [u[slot, s, pl.ds(base + t, half, stride=2), :] for t in range(FFN_CONV_W + 1)]
            even = cb[:, sl]
            odd = cb[:, sl]
            for t in range(FFN_CONV_W):
                even = even + taps[t] * cw[t:t + 1, sl]
                odd = odd + taps[t + 1] * cw[t:t + 1, sl]
            cols.append(jnp.concatenate([even, odd], axis=0))
        return jnp.concatenate(cols, axis=1)

    def gate(j, slot):
        hg = 0.5 * conv(ug, slot, cw_ref[:, span(j,0)], cb_ref[:, span(j,0)])
        silu = hg * jnp.tanh(hg) + hg
        val = conv(uv, slot, cw_ref[:, span(j,1)], cb_ref[:, span(j,1)])
        act[slot] = (silu * val).astype(BF16)

    def down(j, slot):
        return jnp.dot(act[slot], wd_ref[span(j,0), :], preferred_element_type=F32)

    up(0, 0)
    for j in range(nc):
        if j + 1 < nc:
            up(j + 1, (j + 1) % 2)
        gate(j, j % 2)
        if j == 1:
            acc[...] = down(0, 0)
        elif j > 1:
            acc[...] += down(j - 1, (j - 1) % 2)
    mixed = acc[...] + down(nc - 1, (nc - 1) % 2)
    nlane = mixed.shape[1] // LANES
    for s in range(nlane):
        sl = slice(LANES * s, LANES * (s + 1))
        nat[s, pl.ds(0, half, stride=2), :] = mixed[:half, sl]
        nat[s, pl.ds(1, half, stride=2), :] = mixed[half:, sl]
    ffn = jnp.concatenate([nat[s] for s in range(nlane)], axis=1)
    y = x1s[...] + g2_ref[...] * ffn
    ms = jnp.mean(y * y, axis=-1, keepdims=True)
    o_ref[...] = y * lax.rsqrt(ms + EPS) * fg_ref[...]


def _mixer_out_ffn(rn, an, x, wo, mod, norm_g, wu, cw, cb, wd, fg, *, tm):
    b, s, d = x.shape
    nt = s // tm
    assert wd.shape[0] % FFN_CHUNK == 0 and (wd.shape[0] // FFN_CHUNK) % 2 == 1
    halo = SUBLANES_BF16
    per = tm // halo
    nblk = s // halo
    row = lambda bb, i: (bb, i, 0)
    prev = lambda bb, i: (bb, jnp.maximum(i * per - 1, 0), 0)
    nxt = lambda bb, i: (bb, jnp.minimum((i + 1) * per, nblk - 1), 0)
    banded = lambda w: [pl.BlockSpec((None, tm, w), row), pl.BlockSpec((None, halo, w), prev),
                        pl.BlockSpec((None, halo, w), nxt)]
    return pl.pallas_call(
        functools.partial(_ffn_kernel, nt=nt, tm=tm),
        grid=(b, nt),
        in_specs=banded(D_RNN) + banded(D_ATTN) + banded(d)
        + [_const_spec(wo.shape), _mod_spec(d, 2), _mod_spec(d, 3), _mod_spec(d, 4),
           _const_spec(norm_g.shape),
           _const_spec(wu.shape), _const_spec(cw.shape), _const_spec(cb.shape),
           _const_spec(wd.shape), _mod_spec(d, 5), _const_spec(fg.shape)],
        out_specs=pl.BlockSpec((None, tm, d), row),
        out_shape=jax.ShapeDtypeStruct((b, s, d), F32),
        scratch_shapes=[pltpu.VMEM((tm + 2 * halo, D_RNN + D_ATTN), BF16),
                        pltpu.VMEM((tm, d), F32),
                        pltpu.VMEM((tm + 2 * halo, d), BF16),
                        pltpu.VMEM((2, FFN_CHUNK // LANES, tm + 2 * halo, LANES), F32),
                        pltpu.VMEM((2, FFN_CHUNK // LANES, tm + 2 * halo, LANES), F32),
                        pltpu.VMEM((2, tm, FFN_CHUNK), BF16),
                        pltpu.VMEM((tm, d), F32),
                        pltpu.VMEM((d // LANES, tm, LANES), F32)],
        compiler_params=_params("arbitrary", "arbitrary"),
        name="out_proj_conv_ffn",
    )(rn, rn, rn, an, an, an, x, x, x, wo, mod, mod, mod, norm_g, wu, cw, cb, wd, mod, fg)


def _rope_tables(s):
    half = HEAD_DIM // 4
    inv = ROPE_THETA ** (-np.arange(half, dtype=np.float64) / half)

    def tables(npos, lanes_first):
        ang = np.arange(npos, dtype=np.float64)[:, None] * inv[None, :]
        zero = np.zeros((npos, 2 * half))
        cos = np.concatenate([np.cos(ang)] * 2, axis=1)
        sin = np.concatenate([-np.sin(ang), np.sin(ang)], axis=1)
        order = (lambda t: [t, zero]) if lanes_first else (lambda t: [zero, t])
        reps = LANES // HEAD_DIM
        return [jnp.asarray(np.tile(np.concatenate(order(t), axis=1), (1, reps)).astype(np.float32))
                for t in (cos, sin)]

    cosr, sinr = tables(s // GRID_W, True)
    cosc, sinc = tables(GRID_W, False)
    return cosr, sinr, cosc, sinc


def _gate_weights(w_a, w_i):
    per = MXU_DIM // RNN_BLOCK_W
    eye = jnp.eye(per, dtype=F32)

    def dense(w):
        w = w.reshape(RNN_BLOCKS // per, per, RNN_BLOCK_W, RNN_BLOCK_W)
        m = w[:, :, :, None, :] * eye[None, :, None, :, None]
        return m.reshape(RNN_BLOCKS // per, MXU_DIM, MXU_DIM)

    return jnp.concatenate([dense(w_a), dense(w_i)], axis=2).astype(BF16)


def _interleave_heads(w, axis):
    shape = w.shape
    w = w.reshape(shape[:axis] + (N_KV_HEADS, Q_PER_KV, HEAD_DIM) + shape[axis + 1:])
    return jnp.swapaxes(w, axis, axis + 1).reshape(shape)


def kernel(x, c, ctx, c_ctx, w_mod, b_mod, norm1_g, w_in, rnn_conv_w, rnn_conv_b, lru_w_a, lru_b_a,
           lru_w_i, lru_b_i, lru_lam, attn_sink, gn_rnn, gn_attn, w_out, norm2_g, w_up, ffn_conv_w,
           ffn_conv_b, w_down, final_g):
    assert w_mod.shape[0] == 1, "one layer: the last layer's context outputs are never consumed"
    b, s, d = x.shape
    lx = ctx.shape[1]

    cond = jnp.concatenate([c, c_ctx[None], jnp.zeros((SUBLANES_F32 - b - 1, d), F32)], axis=0)
    mod = _modulation(cond, b + 1, w_mod[0], b_mod[0])[:, None, :]

    wi = w_in[0].astype(BF16)
    q0 = 2 * D_RNN
    wi = jnp.concatenate([wi[:, :q0], _interleave_heads(wi[:, q0:q0 + D_ATTN], 1),
                          wi[:, q0 + D_ATTN:]], axis=1)
    wo = w_out[0].astype(BF16)
    wo = jnp.concatenate([wo[:D_RNN], _interleave_heads(wo[D_RNN:], 0)], axis=0)
    gn_a = _interleave_heads(gn_attn[0], 0)[None]
    gn_r = gn_rnn[0][None]
    cw = 0.5 * rnn_conv_w[0]
    cb = 0.5 * rnn_conv_b[0][None]
    sp = (0.5 * LRU_C) * jax.nn.softplus(-lru_lam[0])

    tables = _rope_tables(s)
    xc, gx, q, kv = _in_proj(x, mod, None, norm1_g, wi, cw, cb, tables, rope=True, tm=TM_PROJ)
    xcc, _, _, kvx = _in_proj(ctx, mod, b, norm1_g, wi, cw, cb, tables, rope=False, tm=lx)

    h0 = jnp.zeros((b, 1, D_RNN), F32)
    y_dir = None
    for dr in range(2):
        wg = _gate_weights(lru_w_a[0, dr], lru_w_i[0, dr])
        gate_args = (wg, 0.5 * lru_b_a[0, dr][None], 0.5 * lru_b_i[0, dr][None], sp[dr][None])
        rev = dr == 1
        _, h_ctx = _rglru(xcc, *gate_args, h0, reverse=rev, tm=lx, blk=min(lx, SUB_PROJ))
        fin = (y_dir, gx, gn_r) if rev else None
        y_dir, _ = _rglru(xc, *gate_args, h_ctx, reverse=rev, tm=TM_SCAN, blk=SUB_PROJ,
                          final_args=fin)
    rn = y_dir

    an = _attention(q, kv, kvx, _attn_bias(), attn_sink[0] * LOG2E, gn_a)

    return _mixer_out_ffn(rn, an, x, wo, mod, norm2_g, w_up[0].astype(BF16), ffn_conv_w[0],
                          ffn_conv_b[0][None], w_down[0].astype(BF16), final_g[None], tm=TM_FFN)
```

```python
import functools

import jax
import jax.numpy as jnp
import numpy as np
from jax import lax
from jax.experimental import pallas as pl
from jax.experimental.pallas import tpu as pltpu

F32 = jnp.float32
BF16 = jnp.bfloat16

EPS = 1e-6
GRID_W = 64
D_RNN = 512
RNN_BLOCKS = 8
RNN_BLOCK_W = D_RNN // RNN_BLOCKS
RNN_CONV_W = 4
RNN_CONV_LEFT = 2
LRU_C = 8.0
HEAD_DIM = 64
N_Q_HEADS = 8
N_KV_HEADS = 2
Q_PER_KV = N_Q_HEADS // N_KV_HEADS
D_ATTN = N_Q_HEADS * HEAD_DIM
D_KV = N_KV_HEADS * HEAD_DIM
WINDOW = 128
BLOCK_Q = 128
ROPE_THETA = 10000.0
NEG_INF = -1e30
LOG2E = 1.4426950408889634
F32_TINY = 2.0 ** -126
FFN_CONV_W = 3
FFN_CONV_LEFT = 1

LANES = 128
SUBLANES_F32 = 8
SUBLANES_BF16 = 16
MXU_DIM = 256
VMEM_LIMIT = 56 * 1024 * 1024

TM_PROJ = 2048
SUB_PROJ = 512
TM_SCAN = 1024
TM_FFN = 512
FFN_CHUNK = 256
ATTN_QB = 8
ATTN_LOOKAHEAD = 1


def _params(*sem):
    return pltpu.CompilerParams(dimension_semantics=sem, vmem_limit_bytes=VMEM_LIMIT)


def _const_spec(shape):
    zeros = (0,) * len(shape)
    return pl.BlockSpec(shape, lambda *_: zeros, pipeline_mode=pl.Buffered(1))


def _mod_kernel(condt_ref, w_ref, b_ref, o_ref, *, nrows):
    st = condt_ref[...]
    st = st * jax.nn.sigmoid(st)
    w = w_ref[...]
    rows = [jnp.sum(w * st[:, r:r + 1], axis=0, keepdims=True) for r in range(nrows)]
    rows.append(jnp.zeros((o_ref.shape[0] - nrows, w.shape[1]), F32))
    o_ref[...] = jnp.concatenate(rows, axis=0) + b_ref[...]


def _modulation(cond, nrows, w_mod, b_mod):
    rows, d = cond.shape
    n = w_mod.shape[1]
    tn = 768
    return pl.pallas_call(
        functools.partial(_mod_kernel, nrows=nrows),
        grid=(n // tn,),
        in_specs=[pl.BlockSpec((d, rows), lambda j: (0, 0)),
                  pl.BlockSpec((d, tn), lambda j: (0, j)),
                  pl.BlockSpec((1, tn), lambda j: (0, j))],
        out_specs=pl.BlockSpec((rows, tn), lambda j: (0, j)),
        out_shape=jax.ShapeDtypeStruct((rows, n), F32),
        compiler_params=_params("arbitrary"),
        name="modulation",
    )(cond.T, w_mod, b_mod.reshape(1, n))


def _rope_partner(t):
    lane = lax.broadcasted_iota(jnp.int32, t.shape, 1)
    first = (lane % 32) < 16
    return jnp.where(first, pltpu.roll(t, LANES - 16, 1), pltpu.roll(t, 16, 1))


def _inproj_kernel(x_ref, xp_ref, xn_ref, sh_ref, sc_ref, ng_ref, w_ref, cw_ref, cb_ref,
                   cosr_ref, sinr_ref, cosc_ref, sinc_ref,
                   xc_ref, gx_ref, q_ref, kv_ref, hsc, pslab, *, rope, sub, nt):
    i = pl.program_id(1)
    tm = x_ref.shape[0]
    halo = SUBLANES_BF16
    half = sub // 2
    base = halo - RNN_CONV_LEFT
    q0 = D_RNN
    k0 = q0 + D_ATTN
    v0 = k0 + D_KV
    scale = HEAD_DIM ** -0.5 * LOG2E
    nslab = D_RNN // LANES
    gain = ng_ref[...] * (1.0 + sc_ref[...])

    def modulated(x):
        ms = jnp.mean(x * x, axis=-1, keepdims=True)
        return (x * lax.rsqrt(ms + EPS) * gain + sh_ref[...]).astype(BF16)

    zeros = jnp.zeros((halo, hsc.shape[1]), BF16)
    hsc[0:halo] = jnp.where(i > 0, modulated(xp_ref[...]), zeros)
    hsc[halo + tm:] = jnp.where(i < nt - 1, modulated(xn_ref[...]), zeros)

    for n in range(tm // sub):
        hsc[halo + sub * n:halo + sub * (n + 1)] = modulated(x_ref[sub * n:sub * (n + 1), :])

    cw = cw_ref[...]
    cb = cb_ref[...]
    def project(n):
        pr = jnp.dot(hsc[sub * n:sub * (n + 1) + 2 * halo], w_ref[:, :D_RNN],
                     preferred_element_type=F32)
        p = jnp.dot(hsc[halo + sub * n:halo + sub * (n + 1)], w_ref[:, D_RNN:],
                    preferred_element_type=F32)
        return pr, p

    def epilogue(n, pr, p):
        rows = slice(sub * n, sub * (n + 1))
        cols = []
        for s in range(nslab):
            sl = slice(LANES * s, LANES * (s + 1))
            slab = pslab.at[n * nslab + s]
            slab[...] = pr[:, sl]
            taps = [slab[pl.ds(base + j, half, stride=2), :] for j in range(RNN_CONV_W + 1)]
            even = cb[:, sl]
            odd = cb[:, sl]
            for j in range(RNN_CONV_W):
                even = even + taps[j] * cw[j:j + 1, sl]
                odd = odd + taps[j + 1] * cw[j:j + 1, sl]
            cols.append(jnp.concatenate([even, odd], axis=0))
        xc_ref[rows, :] = jnp.concatenate(cols, axis=1).astype(BF16)
        gx_ref[rows, :] = jax.nn.gelu(p[:, :D_RNN]).astype(BF16)
        cols = [p[:, q0 + LANES * c:q0 + LANES * (c + 1)] for c in range(D_ATTN // LANES)]
        k = p[:, k0:v0]
        if rope:
            per = sub // GRID_W
            trows = slice(per * n, per * (n + 1))
            expand = lambda r_ref, c_ref: (r_ref[trows, :][:, None, :]
                                           + c_ref[...][None, :, :]).reshape(sub, LANES)
            cos = expand(cosr_ref, cosc_ref)
            sin = expand(sinr_ref, sinc_ref)
            cols = [t * cos + _rope_partner(t) * sin for t in cols]
            k = k * cos + _rope_partner(k) * sin
        for c, t in enumerate(cols):
            q_ref[rows, LANES * c:LANES * (c + 1)] = (t * scale).astype(BF16)
        kv_ref[rows, :D_KV] = k.astype(BF16)
        kv_ref[rows, D_KV:] = p[:, v0:v0 + D_KV].astype(BF16)

    nsub = tm // sub
    ready = project(0)
    for n in range(nsub):
        following = project(n + 1) if n + 1 < nsub else None
        epilogue(n, *ready)
        ready = following


def _mod_spec(d, chunk, row=None):
    return pl.BlockSpec((None, 1, d), lambda bb, i: (bb if row is None else row, 0, chunk))


def _in_proj(x, mod, mod_row, norm_g, w_in, cw, cb, tables, *, rope, tm):
    b, s, d = x.shape
    n = w_in.shape[1]
    nt = s // tm
    sub = min(tm, SUB_PROJ)
    halo = SUBLANES_BF16
    per = tm // halo
    row = lambda bb, i: (bb, i, 0)
    prev = lambda bb, i: (bb, jnp.maximum(i * per - 1, 0), 0)
    nxt = lambda bb, i: (bb, jnp.minimum((i + 1) * per, s // halo - 1), 0)
    cosr, sinr, cosc, sinc = tables
    if rope:
        assert tm % GRID_W == 0
        rtab = pl.BlockSpec((tm // GRID_W, LANES), lambda bb, i: (i, 0))
    else:
        rtab = _const_spec(cosr.shape)
    outs = [(D_RNN, BF16), (D_RNN, BF16), (D_ATTN, BF16), (2 * D_KV, BF16)]
    return pl.pallas_call(
        functools.partial(_inproj_kernel, rope=rope, sub=sub, nt=nt),
        grid=(b, nt),
        in_specs=[pl.BlockSpec((None, tm, d), row),
                  pl.BlockSpec((None, halo, d), prev),
                  pl.BlockSpec((None, halo, d), nxt),
                  _mod_spec(d, 0, mod_row), _mod_spec(d, 1, mod_row), _const_spec(norm_g.shape),
                  _const_spec((d, n)), _const_spec(cw.shape), _const_spec(cb.shape),
                  rtab, rtab, _const_spec(cosc.shape), _const_spec(sinc.shape)],
        out_specs=[pl.BlockSpec((None, tm, w), row) for w, _ in outs],
        out_shape=[jax.ShapeDtypeStruct((b, s, w), dt) for w, dt in outs],
        scratch_shapes=[pltpu.VMEM((tm + 2 * halo, d), BF16),
                        pltpu.VMEM((tm // sub * (D_RNN // LANES), sub + 2 * halo, LANES), F32)],
        compiler_params=_params("arbitrary", "arbitrary"),
        name="in_proj_rope" if rope else "in_proj_ctx",
    )(x, x, x, mod, mod, norm_g, w_in, cw, cb, cosr, sinr, cosc, sinc)


def _rglru_kernel(*refs, reverse, final, blk):
    if final:
        (xc_ref, wg_ref, ba_ref, bi_ref, sp_ref, h0_ref, yo_ref, gx_ref, gn_ref,
         y_ref, hfin_ref, a_scr, u_scr, y_scr, h_scr) = refs
    else:
        (xc_ref, wg_ref, ba_ref, bi_ref, sp_ref, h0_ref,
         y_ref, hfin_ref, a_scr, u_scr, y_scr, h_scr) = refs
    i = pl.program_id(0)
    nb, tm, _ = xc_ref.shape

    @pl.when(i == 0)
    def _():
        h_scr[...] = h0_ref[...]

    for bb in range(nb):
        xb = xc_ref[bb]
        xc = xb.astype(F32)
        for g in range(D_RNN // MXU_DIM):
            sl = slice(MXU_DIM * g, MXU_DIM * (g + 1))
            z = jnp.dot(xb[:, sl], wg_ref[g], preferred_element_type=F32)
            tr = jnp.tanh(z[:, :MXU_DIM] + ba_ref[:, sl])
            ti = jnp.tanh(z[:, MXU_DIM:] + bi_ref[:, sl])
            nl = (tr + 1.0) * sp_ref[:, sl]
            a = jnp.exp2(nl * (-LOG2E))
            a_scr[bb, :, sl] = a
            w = jnp.tanh(nl) * (1.0 + a * a)
            u_scr[bb, :, sl] = w * lax.rsqrt(jnp.maximum(w, F32_TINY)) * ((ti + 1.0) * xc[:, sl])

    rows = SUBLANES_F32
    half = blk // 2
    ngrp = half // rows
    order = range(rows - 1, -1, -1) if reverse else range(rows)
    parity = (1, 0) if reverse else (0, 1)
    spans = range(tm // blk - 1, -1, -1) if reverse else range(tm // blk)

    def group(g, hs, off):
        gg = (ngrp - 1 - g) if reverse else g
        lo = pl.multiple_of(off + gg * rows, rows)
        hi = pl.multiple_of(off + half + gg * rows, rows)
        out = pl.multiple_of(off + gg * 2 * rows, 2 * rows)
        hs = list(hs)
        for j in order:
            for par in parity:
                start = hi if par else lo
                for bb in range(nb):
                    av = a_scr.at[bb, pl.ds(start, rows), :]
                    uv = u_scr.at[bb, pl.ds(start, rows), :]
                    yv = y_scr.at[bb, pl.ds(out, 2 * rows), :]
                    hs[bb] = av[j:j + 1, :] * hs[bb] + uv[j:j + 1, :]
                    yv[2 * j + par:2 * j + par + 1, :] = hs[bb]
        return tuple(hs)

    hs = tuple(h_scr[bb] for bb in range(nb))
    for span in spans:
        hs = lax.fori_loop(0, ngrp, functools.partial(group, off=span * blk), hs)
    for bb in range(nb):
        h_scr[bb] = hs[bb]
        hfin_ref[bb] = hs[bb]

    if final:
        z = gx_ref[...] * (y_scr[...] + yo_ref[...])
        ms = jnp.mean(z * z, axis=-1, keepdims=True)
        y_ref[...] = (z * lax.rsqrt(ms + EPS) * gn_ref[...]).astype(BF16)
    else:
        y_ref[...] = y_scr[...]


def _rglru(xc, wg, ba, bi, sp, h0, *, reverse, tm, blk, final_args=None):
    b, s, d = xc.shape
    nt = s // tm
    final = final_args is not None
    row = (lambda i: (0, nt - 1 - i, 0)) if reverse else (lambda i: (0, i, 0))
    in_specs = [pl.BlockSpec((b, tm, d), row), _const_spec(wg.shape),
                _const_spec(ba.shape), _const_spec(bi.shape), _const_spec(sp.shape),
                _const_spec(h0.shape)]
    args = [xc, wg, ba, bi, sp, h0]
    if final:
        y_other, gx, gn = final_args
        in_specs += [pl.BlockSpec((b, tm, d), row), pl.BlockSpec((b, tm, d), row),
                     _const_spec(gn.shape)]
        args += [y_other, gx, gn]
    return pl.pallas_call(
        functools.partial(_rglru_kernel, reverse=reverse, final=final, blk=blk),
        grid=(nt,),
        in_specs=in_specs,
        out_specs=[pl.BlockSpec((b, tm, d), row), pl.BlockSpec((b, 1, d), lambda i: (0, 0, 0))],
        out_shape=[jax.ShapeDtypeStruct((b, s, d), BF16 if final else F32),
                   jax.ShapeDtypeStruct((b, 1, d), F32)],
        scratch_shapes=[pltpu.VMEM((b, tm, d), F32), pltpu.VMEM((b, tm, d), F32),
                        pltpu.VMEM((b, tm, d), F32), pltpu.VMEM((b, 1, d), F32)],
        compiler_params=_params("arbitrary"),
        name="rglru_" + ("bwd" if reverse else "fwd") + ("_final" if final else ""),
    )(*args)


def _attn_kernel(q_ref, kvp_ref, kvc_ref, kvn_ref, kvx_ref, bias_ref, sink_ref, gn_ref, o_ref,
                 keys, vals, keyx, valx, *, qb, nstep):
    ncol = D_ATTN // LANES
    nwin = 3 * BLOCK_Q
    step = pl.program_id(1)
    band = ((slice(0, BLOCK_Q), kvp_ref), (slice(BLOCK_Q, BLOCK_Q * (qb + 1)), kvc_ref),
            (slice(BLOCK_Q * (qb + 1), BLOCK_Q * (qb + 2)), kvn_ref))
    for rows, kv_ref in band:
        keys[rows, :] = kv_ref[:, :D_KV]
        vals[rows, :D_KV] = kv_ref[:, D_KV:]
    keyx[...] = kvx_ref[:, :D_KV]
    valx[:, :D_KV] = kvx_ref[:, D_KV:]
    vals[:, D_KV:] = jnp.ones((vals.shape[0], LANES), BF16)
    valx[:, D_KV:] = jnp.ones((valx.shape[0], LANES), BF16)

    low = lax.broadcasted_iota(jnp.int32, (BLOCK_Q, LANES), 1) < HEAD_DIM
    first = lax.broadcasted_iota(jnp.int32, (2 * BLOCK_Q, 1), 0) < BLOCK_Q
    zero = jnp.zeros((BLOCK_Q, LANES), BF16)
    nt_dims = (((1,), (1,)), ((), ()))
    def block_bias(x):
        variant = 1
        if x == 0:
            variant = jnp.where(step == 0, 0, variant)
        if x == qb - 1:
            variant = jnp.where(step == nstep - 1, 2, variant)
        return bias_ref[variant]

    def scores(x, c, bias):
        win = slice(BLOCK_Q * x, BLOCK_Q * x + nwin)
        t = q_ref[BLOCK_Q * x:BLOCK_Q * (x + 1), LANES * c:LANES * (c + 1)]
        qs = jnp.concatenate([jnp.where(low, t, zero), jnp.where(low, zero, t)], axis=0)
        s_loc = lax.dot_general(qs, keys[win, :], nt_dims, preferred_element_type=F32)
        s_ctx = lax.dot_general(qs, keyx[...], nt_dims, preferred_element_type=F32)
        s_loc = (s_loc.reshape(2, BLOCK_Q, nwin) + bias[None]).reshape(2 * BLOCK_Q, nwin)
        sink = jnp.where(first, sink_ref[c], sink_ref[ncol + c])
        m = jnp.maximum(jnp.maximum(jnp.max(s_loc, axis=-1, keepdims=True),
                                    jnp.max(s_ctx, axis=-1, keepdims=True)), sink)
        e_loc = jnp.exp2(s_loc - m).astype(BF16)
        e_ctx = jnp.exp2(s_ctx - m).astype(BF16)
        return x, e_loc, e_ctx, jnp.exp2(sink - m)

    def weighted(x, e_loc, e_ctx, e_sink):
        win = slice(BLOCK_Q * x, BLOCK_Q * x + nwin)
        pv = (jnp.dot(e_loc, vals[win, :], preferred_element_type=F32)
              + jnp.dot(e_ctx, valx[...], preferred_element_type=F32))
        o = pv[:, :D_KV] / (pv[:, D_KV:] + e_sink)
        return jnp.where(low, o[:BLOCK_Q], o[BLOCK_Q:])

    def finish(x, outs):
        ms = sum(jnp.sum(t * t, axis=-1, keepdims=True) for t in outs) * (1.0 / D_ATTN)
        inv = lax.rsqrt(ms + EPS)
        for c, t in enumerate(outs):
            sl = slice(LANES * c, LANES * (c + 1))
            o_ref[BLOCK_Q * x:BLOCK_Q * (x + 1), sl] = (t * inv * gn_ref[:, sl]).astype(BF16)

    pending = []
    outs = [[] for _ in range(qb)]

    def retire():
        chain = pending.pop(0)
        x = chain[0]
        outs[x].append(weighted(*chain))
        if len(outs[x]) == ncol:
            finish(x, outs[x])

    for x in range(qb):
        bias = block_bias(x)
        for c in range(ncol):
            pending.append(scores(x, c, bias))
            if len(pending) > ATTN_LOOKAHEAD:
                retire()
    while pending:
        retire()


def _attention(q, kv, kvx, bias, sink, gn):
    b, s, _ = q.shape
    qb = ATTN_QB
    nblk = s // BLOCK_Q
    nstep = nblk // qb
    lx = kvx.shape[1]
    cur = lambda bb, n: (bb, n, 0)
    prev = lambda bb, n: (bb, jnp.maximum(n * qb - 1, 0), 0)
    nxt = lambda bb, n: (bb, jnp.minimum((n + 1) * qb, nblk - 1), 0)
    halo = lambda im: pl.BlockSpec((None, BLOCK_Q, 2 * D_KV), im)
    return pl.pallas_call(
        functools.partial(_attn_kernel, qb=qb, nstep=nstep),
        grid=(b, nstep),
        in_specs=[pl.BlockSpec((None, qb * BLOCK_Q, D_ATTN), cur),
                  halo(prev), pl.BlockSpec((None, qb * BLOCK_Q, 2 * D_KV), cur), halo(nxt),
                  pl.BlockSpec((None, lx, 2 * D_KV), lambda bb, n: (bb, 0, 0)),
                  _const_spec(bias.shape), pl.BlockSpec(memory_space=pltpu.SMEM),
                  _const_spec(gn.shape)],
        out_specs=pl.BlockSpec((None, qb * BLOCK_Q, D_ATTN), cur),
        out_shape=jax.ShapeDtypeStruct((b, s, D_ATTN), BF16),
        scratch_shapes=[pltpu.VMEM(((qb + 2) * BLOCK_Q, D_KV), BF16),
                        pltpu.VMEM(((qb + 2) * BLOCK_Q, D_KV + LANES), BF16),
                        pltpu.VMEM((lx, D_KV), BF16),
                        pltpu.VMEM((lx, D_KV + LANES), BF16)],
        compiler_params=_params("arbitrary", "arbitrary"),
        name="attention",
    )(q, kv, kv, kv, kvx, bias, sink, gn)


def _attn_bias():
    i = np.arange(BLOCK_Q)[:, None]
    j = np.arange(3 * BLOCK_Q)[None, :]
    band = np.abs(i + BLOCK_Q - j) <= WINDOW
    variants = [band & (j >= BLOCK_Q), band, band & (j < 2 * BLOCK_Q)]
    return jnp.asarray(np.stack([np.where(ok, 0.0, NEG_INF) for ok in variants]).astype(np.float32))


def _ffn_kernel(rn_ref, rnp_ref, rnn_ref, an_ref, anp_ref, ann_ref, x_ref, xp_ref, xn_ref,
                wo_ref, g1_ref, sh_ref, sc_ref, ng_ref, wu_ref, cw_ref, cb_ref, wd_ref, g2_ref, fg_ref,
                o_ref, mix, x1s, hext, ug, uv, act, acc, nat, *, nt, tm):
    d_ff = wd_ref.shape[0]
    nc = d_ff // FFN_CHUNK
    span = lambda j, branch: slice(branch * d_ff + j * FFN_CHUNK, branch * d_ff + (j + 1) * FFN_CHUNK)
    i = pl.program_id(1)
    halo = SUBLANES_BF16
    body = slice(halo, halo + tm)
    tail = slice(halo + tm, 2 * halo + tm)
    for rows, r_ref, a_ref in ((slice(0, halo), rnp_ref, anp_ref), (body, rn_ref, an_ref),
                               (tail, rnn_ref, ann_ref)):
        mix[rows, :D_RNN] = r_ref[...]
        mix[rows, D_RNN:] = a_ref[...]
    proj = jnp.dot(mix[...], wo_ref[...], preferred_element_type=F32)

    def residual(xin_ref, rows):
        return xin_ref[...] + g1_ref[...] * proj[rows]

    gain = ng_ref[...] * (1.0 + sc_ref[...])

    def modulated(x1):
        ms = jnp.mean(x1 * x1, axis=-1, keepdims=True)
        return (x1 * lax.rsqrt(ms + EPS) * gain + sh_ref[...]).astype(BF16)

    x1 = residual(x_ref, body)
    x1s[...] = x1
    hext[body] = modulated(x1)
    zeros = jnp.zeros((halo, hext.shape[1]), BF16)
    hext[0:halo] = jnp.where(i > 0, modulated(residual(xp_ref, slice(0, halo))), zeros)
    hext[tail] = jnp.where(i < nt - 1, modulated(residual(xn_ref, tail)), zeros)

    base = halo - FFN_CONV_LEFT

    half = tm // 2
    nslab = FFN_CHUNK // LANES

    def up(j, slot):
        he = hext[...]
        g = jnp.dot(he, wu_ref[:, span(j,0)], preferred_element_type=F32)
        v = jnp.dot(he, wu_ref[:, span(j,1)], preferred_element_type=F32)
        for s in range(nslab):
            ug[slot, s] = g[:, LANES * s:LANES * (s + 1)]
            uv[slot, s] = v[:, LANES * s:LANES * (s + 1)]

    def conv(u, slot, cw, cb):
        cols = []
        for s in range(nslab):
            sl = slice(LANES * s, LANES * (s + 1))
            taps = [u[slot, s, pl.ds(base + t, half, stride=2), :] for t in range(FFN_CONV_W + 1)]
            even = cb[:, sl]
            odd = cb[:, sl]
            for t in range(FFN_CONV_W):
                even = even + taps[t] * cw[t:t + 1, sl]
                odd = odd + taps[t + 1] * cw[t:t + 1, sl]
            cols.append(jnp.concatenate([even, odd], axis=0))
        return jnp.concatenate(cols, axis=1)

    def gate(j, slot):
        hg = 0.5 * conv(ug, slot, cw_ref[:, span(j,0)], cb_ref[:, span(j,0)])
        silu = hg * jnp.tanh(hg) + hg
        val = conv(uv, slot, cw_ref[:, span(j,1)], cb_ref[:, span(j,1)])
        act[slot] = (silu * val).astype(BF16)

    def down(j, slot):
        return jnp.dot(act[slot], wd_ref[span(j,0), :], preferred_element_type=F32)

    up(0, 0)
    for j in range(nc):
        if j + 1 < nc:
            up(j + 1, (j + 1) % 2)
        gate(j, j % 2)
        if j == 1:
            acc[...] = down(0, 0)
        elif j > 1:
            acc[...] += down(j - 1, (j - 1) % 2)
    mixed = acc[...] + down(nc - 1, (nc - 1) % 2)
    nlane = mixed.shape[1] // LANES
    for s in range(nlane):
        sl = slice(LANES * s, LANES * (s + 1))
        nat[s, pl.ds(0, half, stride=2), :] = mixed[:half, sl]
        nat[s, pl.ds(1, half, stride=2), :] = mixed[half:, sl]
    ffn = jnp.concatenate([nat[s] for s in range(nlane)], axis=1)
    y = x1s[...] + g2_ref[...] * ffn
    ms = jnp.mean(y * y, axis=-1, keepdims=True)
    o_ref[...] = y * lax.rsqrt(ms + EPS) * fg_ref[...]


def _mixer_out_ffn(rn, an, x, wo, mod, norm_g, wu, cw, cb, wd, fg, *, tm):
    b, s, d = x.shape
    nt = s // tm
    assert wd.shape[0] % FFN_CHUNK == 0 and (wd.shape[0] // FFN_CHUNK) % 2 == 1
    halo = SUBLANES_BF16
    per = tm // halo
    nblk = s // halo
    row = lambda bb, i: (bb, i, 0)
    prev = lambda bb, i: (bb, jnp.maximum(i * per - 1, 0), 0)
    nxt = lambda bb, i: (bb, jnp.minimum((i + 1) * per, nblk - 1), 0)
    banded = lambda w: [pl.BlockSpec((None, tm, w), row), pl.BlockSpec((None, halo, w), prev),
                        pl.BlockSpec((None, halo, w), nxt)]
    return pl.pallas_call(
        functools.partial(_ffn_kernel, nt=nt, tm=tm),
        grid=(b, nt),
        in_specs=banded(D_RNN) + banded(D_ATTN) + banded(d)
        + [_const_spec(wo.shape), _mod_spec(d, 2), _mod_spec(d, 3), _mod_spec(d, 4),
           _const_spec(norm_g.shape),
           _const_spec(wu.shape), _const_spec(cw.shape), _const_spec(cb.shape),
           _const_spec(wd.shape), _mod_spec(d, 5), _const_spec(fg.shape)],
        out_specs=pl.BlockSpec((None, tm, d), row),
        out_shape=jax.ShapeDtypeStruct((b, s, d), F32),
        scratch_shapes=[pltpu.VMEM((tm + 2 * halo, D_RNN + D_ATTN), BF16),
                        pltpu.VMEM((tm, d), F32),
                        pltpu.VMEM((tm + 2 * halo, d), BF16),
                        pltpu.VMEM((2, FFN_CHUNK // LANES, tm + 2 * halo, LANES), F32),
                        pltpu.VMEM((2, FFN_CHUNK // LANES, tm + 2 * halo, LANES), F32),
                        pltpu.VMEM((2, tm, FFN_CHUNK), BF16),
                        pltpu.VMEM((tm, d), F32),
                        pltpu.VMEM((d // LANES, tm, LANES), F32)],
        compiler_params=_params("arbitrary", "arbitrary"),
        name="out_proj_conv_ffn",
    )(rn, rn, rn, an, an, an, x, x, x, wo, mod, mod, mod, norm_g, wu, cw, cb, wd, mod, fg)


def _rope_tables(s):
    half = HEAD_DIM // 4
    inv = ROPE_THETA ** (-np.arange(half, dtype=np.float64) / half)

    def tables(npos, lanes_first):
        ang = np.arange(npos, dtype=np.float64)[:, None] * inv[None, :]
        zero = np.zeros((npos, 2 * half))
        cos = np.concatenate([np.cos(ang)] * 2, axis=1)
        sin = np.concatenate([-np.sin(ang), np.sin(ang)], axis=1)
        order = (lambda t: [t, zero]) if lanes_first else (lambda t: [zero, t])
        reps = LANES // HEAD_DIM
        return [jnp.asarray(np.tile(np.concatenate(order(t), axis=1), (1, reps)).astype(np.float32))
                for t in (cos, sin)]

    cosr, sinr = tables(s // GRID_W, True)
    cosc, sinc = tables(GRID_W, False)
    return cosr, sinr, cosc, sinc


def _gate_weights(w_a, w_i):
    per = MXU_DIM // RNN_BLOCK_W
    eye = jnp.eye(per, dtype=F32)

    def dense(w):
        w = w.reshape(RNN_BLOCKS // per, per, RNN_BLOCK_W, RNN_BLOCK_W)
        m = w[:, :, :, None, :] * eye[None, :, None, :, None]
        return m.reshape(RNN_BLOCKS // per, MXU_DIM, MXU_DIM)

    return jnp.concatenate([dense(w_a), dense(w_i)], axis=2).astype(BF16)


def _interleave_heads(w, axis):
    shape = w.shape
    w = w.reshape(shape[:axis] + (N_KV_HEADS, Q_PER_KV, HEAD_DIM) + shape[axis + 1:])
    return jnp.swapaxes(w, axis, axis + 1).reshape(shape)


def kernel(x, c, ctx, c_ctx, w_mod, b_mod, norm1_g, w_in, rnn_conv_w, rnn_conv_b, lru_w_a, lru_b_a,
           lru_w_i, lru_b_i, lru_lam, attn_sink, gn_rnn, gn_attn, w_out, norm2_g, w_up, ffn_conv_w,
           ffn_conv_b, w_down, final_g):
    assert w_mod.shape[0] == 1, "one layer: the last layer's context outputs are never consumed"
    b, s, d = x.shape
    lx = ctx.shape[1]

    cond = jnp.concatenate([c, c_ctx[None], jnp.zeros((SUBLANES_F32 - b - 1, d), F32)], axis=0)
    mod = _modulation(cond, b + 1, w_mod[0], b_mod[0])[:, None, :]

    wi = w_in[0].astype(BF16)
    q0 = 2 * D_RNN
    wi = jnp.concatenate([wi[:, :q0], _interleave_heads(wi[:, q0:q0 + D_ATTN], 1),
                          wi[:, q0 + D_ATTN:]], axis=1)
    wo = w_out[0].astype(BF16)
    wo = jnp.concatenate([wo[:D_RNN], _interleave_heads(wo[D_RNN:], 0)], axis=0)
    gn_a = _interleave_heads(gn_attn[0], 0)[None]
    gn_r = gn_rnn[0][None]
    cw = 0.5 * rnn_conv_w[0]
    cb = 0.5 * rnn_conv_b[0][None]
    sp = (0.5 * LRU_C) * jax.nn.softplus(-lru_lam[0])

    tables = _rope_tables(s)
    xc, gx, q, kv = _in_proj(x, mod, None, norm1_g, wi, cw, cb, tables, rope=True, tm=TM_PROJ)
    xcc, _, _, kvx = _in_proj(ctx, mod, b, norm1_g, wi, cw, cb, tables, rope=False, tm=lx)

    h0 = jnp.zeros((b, 1, D_RNN), F32)
    y_dir = None
    for dr in range(2):
        wg = _gate_weights(lru_w_a[0, dr], lru_w_i[0, dr])
        gate_args = (wg, 0.5 * lru_b_a[0, dr][None], 0.5 * lru_b_i[0, dr][None], sp[dr][None])
        rev = dr == 1
        _, h_ctx = _rglru(xcc, *gate_args, h0, reverse=rev, tm=lx, blk=min(lx, SUB_PROJ))
        fin = (y_dir, gx, gn_r) if rev else None
        y_dir, _ = _rglru(xc, *gate_args, h_ctx, reverse=rev, tm=TM_SCAN, blk=SUB_PROJ,
                          final_args=fin)
    rn = y_dir

    an = _attention(q, kv, kvx, _attn_bias(), attn_sink[0] * LOG2E, gn_a)

    return _mixer_out_ffn(rn, an, x, wo, mod, norm2_g, w_up[0].astype(BF16), ffn_conv_w[0],
                          ffn_conv_b[0][None], w_down[0].astype(BF16), final_g[None], tm=TM_FFN)
```

```python
import functools

import jax
import jax.numpy as jnp
import numpy as np
from jax import lax
from jax.experimental import pallas as pl
from jax.experimental.pallas import tpu as pltpu

F32 = jnp.float32
BF16 = jnp.bfloat16

EPS = 1e-6
GRID_W = 64
D_RNN = 512
RNN_BLOCKS = 8
RNN_BLOCK_W = D_RNN // RNN_BLOCKS
RNN_CONV_W = 4
RNN_CONV_LEFT = 2
LRU_C = 8.0
HEAD_DIM = 64
N_Q_HEADS = 8
N_KV_HEADS = 2
Q_PER_KV = N_Q_HEADS // N_KV_HEADS
D_ATTN = N_Q_HEADS * HEAD_DIM
D_KV = N_KV_HEADS * HEAD_DIM
WINDOW = 128
BLOCK_Q = 128
ROPE_THETA = 10000.0
NEG_INF = -1e30
LOG2E = 1.4426950408889634
F32_TINY = 2.0 ** -126
FFN_CONV_W = 3
FFN_CONV_LEFT = 1

LANES = 128
SUBLANES_F32 = 8
SUBLANES_BF16 = 16
MXU_DIM = 256
VMEM_LIMIT = 56 * 1024 * 1024

TM_PROJ = 2048
SUB_PROJ = 512
TM_SCAN = 1024
TM_FFN = 512
FFN_CHUNK = 256
ATTN_QB = 8
ATTN_LOOKAHEAD = 1


def _params(*sem):
    return pltpu.CompilerParams(dimension_semantics=sem, vmem_limit_bytes=VMEM_LIMIT)


def _const_spec(shape):
    zeros = (0,) * len(shape)
    return pl.BlockSpec(shape, lambda *_: zeros, pipeline_mode=pl.Buffered(1))


def _mod_kernel(condt_ref, w_ref, b_ref, o_ref, *, nrows):
    st = condt_ref[...]
    st = st * jax.nn.sigmoid(st)
    w = w_ref[...]
    rows = [jnp.sum(w * st[:, r:r + 1], axis=0, keepdims=True) for r in range(nrows)]
    rows.append(jnp.zeros((o_ref.shape[0] - nrows, w.shape[1]), F32))
    o_ref[...] = jnp.concatenate(rows, axis=0) + b_ref[...]


def _modulation(cond, nrows, w_mod, b_mod):
    rows, d = cond.shape
    n = w_mod.shape[1]
    tn = 768
    return pl.pallas_call(
        functools.partial(_mod_kernel, nrows=nrows),
        grid=(n // tn,),
        in_specs=[pl.BlockSpec((d, rows), lambda j: (0, 0)),
                  pl.BlockSpec((d, tn), lambda j: (0, j)),
                  pl.BlockSpec((1, tn), lambda j: (0, j))],
        out_specs=pl.BlockSpec((rows, tn), lambda j: (0, j)),
        out_shape=jax.ShapeDtypeStruct((rows, n), F32),
        compiler_params=_params("arbitrary"),
        name="modulation",
    )(cond.T, w_mod, b_mod.reshape(1, n))


def _rope_partner(t):
    lane = lax.broadcasted_iota(jnp.int32, t.shape, 1)
    first = (lane % 32) < 16
    return jnp.where(first, pltpu.roll(t, LANES - 16, 1), pltpu.roll(t, 16, 1))


def _inproj_kernel(x_ref, xp_ref, xn_ref, sh_ref, sc_ref, ng_ref, w_ref, cw_ref, cb_ref,
                   cosr_ref, sinr_ref, cosc_ref, sinc_ref,
                   xc_ref, gx_ref, q_ref, kv_ref, hsc, pslab, *, rope, sub, nt):
    i = pl.program_id(1)
    tm = x_ref.shape[0]
    halo = SUBLANES_BF16
    half = sub // 2
    base = halo - RNN_CONV_LEFT
    q0 = D_RNN
    k0 = q0 + D_ATTN
    v0 = k0 + D_KV
    scale = HEAD_DIM ** -0.5 * LOG2E
    nslab = D_RNN // LANES
    gain = ng_ref[...] * (1.0 + sc_ref[...])

    def modulated(x):
        ms = jnp.mean(x * x, axis=-1, keepdims=True)
        return (x * lax.rsqrt(ms + EPS) * gain + sh_ref[...]).astype(BF16)

    zeros = jnp.zeros((halo, hsc.shape[1]), BF16)
    hsc[0:halo] = jnp.where(i > 0, modulated(xp_ref[...]), zeros)
    hsc[halo + tm:] = jnp.where(i < nt - 1, modulated(xn_ref[...]), zeros)

    for n in range(tm // sub):
        hsc[halo + sub * n:halo + sub * (n + 1)] = modulated(x_ref[sub * n:sub * (n + 1), :])

    cw = cw_ref[...]
    cb = cb_ref[...]
    def project(n):
        pr = jnp.dot(hsc[sub * n:sub * (n + 1) + 2 * halo], w_ref[:, :D_RNN],
                     preferred_element_type=F32)
        p = jnp.dot(hsc[halo + sub * n:halo + sub * (n + 1)], w_ref[:, D_RNN:],
                    preferred_element_type=F32)
        return pr, p

    def epilogue(n, pr, p):
        rows = slice(sub * n, sub * (n + 1))
        cols = []
        for s in range(nslab):
            sl = slice(LANES * s, LANES * (s + 1))
            slab = pslab.at[n * nslab + s]
            slab[...] = pr[:, sl]
            taps = [slab[pl.ds(base + j, half, stride=2), :] for j in range(RNN_CONV_W + 1)]
            even = cb[:, sl]
            odd = cb[:, sl]
            for j in range(RNN_CONV_W):
                even = even + taps[j] * cw[j:j + 1, sl]
                odd = odd + taps[j + 1] * cw[j:j + 1, sl]
            cols.append(jnp.concatenate([even, odd], axis=0))
        xc_ref[rows, :] = jnp.concatenate(cols, axis=1)
        gx_ref[rows, :] = jax.nn.gelu(p[:, :D_RNN]).astype(BF16)
        cols = [p[:, q0 + LANES * c:q0 + LANES * (c + 1)] for c in range(D_ATTN // LANES)]
        k = p[:, k0:v0]
        if rope:
            per = sub // GRID_W
            trows = slice(per * n, per * (n + 1))
            expand = lambda r_ref, c_ref: (r_ref[trows, :][:, None, :]
                                           + c_ref[...][None, :, :]).reshape(sub, LANES)
            cos = expand(cosr_ref, cosc_ref)
            sin = expand(sinr_ref, sinc_ref)
            cols = [t * cos + _rope_partner(t) * sin for t in cols]
            k = k * cos + _rope_partner(k) * sin
        for c, t in enumerate(cols):
            q_ref[rows, LANES * c:LANES * (c + 1)] = (t * scale).astype(BF16)
        kv_ref[rows, :D_KV] = k.astype(BF16)
        kv_ref[rows, D_KV:] = p[:, v0:v0 + D_KV].astype(BF16)

    nsub = tm // sub
    ready = project(0)
    for n in range(nsub):
        following = project(n + 1) if n + 1 < nsub else None
        epilogue(n, *ready)
        ready = following


def _mod_spec(d, chunk, row=None):
    return pl.BlockSpec((None, 1, d), lambda bb, i: (bb if row is None else row, 0, chunk))


def _in_proj(x, mod, mod_row, norm_g, w_in, cw, cb, tables, *, rope, tm):
    b, s, d = x.shape
    n = w_in.shape[1]
    nt = s // tm
    sub = min(tm, SUB_PROJ)
    halo = SUBLANES_BF16
    per = tm // halo
    row = lambda bb, i: (bb, i, 0)
    prev = lambda bb, i: (bb, jnp.maximum(i * per - 1, 0), 0)
    nxt = lambda bb, i: (bb, jnp.minimum((i + 1) * per, s // halo - 1), 0)
    cosr, sinr, cosc, sinc = tables
    if rope:
        assert tm % GRID_W == 0
        rtab = pl.BlockSpec((tm // GRID_W, LANES), lambda bb, i: (i, 0))
    else:
        rtab = _const_spec(cosr.shape)
    outs = [(D_RNN, F32), (D_RNN, BF16), (D_ATTN, BF16), (2 * D_KV, BF16)]
    return pl.pallas_call(
        functools.partial(_inproj_kernel, rope=rope, sub=sub, nt=nt),
        grid=(b, nt),
        in_specs=[pl.BlockSpec((None, tm, d), row),
                  pl.BlockSpec((None, halo, d), prev),
                  pl.BlockSpec((None, halo, d), nxt),
                  _mod_spec(d, 0, mod_row), _mod_spec(d, 1, mod_row), _const_spec(norm_g.shape),
                  _const_spec((d, n)), _const_spec(cw.shape), _const_spec(cb.shape),
                  rtab, rtab, _const_spec(cosc.shape), _const_spec(sinc.shape)],
        out_specs=[pl.BlockSpec((None, tm, w), row) for w, _ in outs],
        out_shape=[jax.ShapeDtypeStruct((b, s, w), dt) for w, dt in outs],
        scratch_shapes=[pltpu.VMEM((tm + 2 * halo, d), BF16),
                        pltpu.VMEM((tm // sub * (D_RNN // LANES), sub + 2 * halo, LANES), F32)],
        compiler_params=_params("arbitrary", "arbitrary"),
        name="in_proj_rope" if rope else "in_proj_ctx",
    )(x, x, x, mod, mod, norm_g, w_in, cw, cb, cosr, sinr, cosc, sinc)


def _rglru_kernel(*refs, reverse, final, blk):
    if final:
        (xc_ref, wg_ref, ba_ref, bi_ref, sp_ref, h0_ref, yo_ref, gx_ref, gn_ref,
         y_ref, hfin_ref, a_scr, u_scr, y_scr, h_scr) = refs
    else:
        (xc_ref, wg_ref, ba_ref, bi_ref, sp_ref, h0_ref,
         y_ref, hfin_ref, a_scr, u_scr, y_scr, h_scr) = refs
    i = pl.program_id(0)
    nb, tm, _ = xc_ref.shape

    @pl.when(i == 0)
    def _():
        h_scr[...] = h0_ref[...]

    for bb in range(nb):
        xc = xc_ref[bb]
        xb = xc.astype(BF16)
        for g in range(D_RNN // MXU_DIM):
            sl = slice(MXU_DIM * g, MXU_DIM * (g + 1))
            z = jnp.dot(xb[:, sl], wg_ref[g], preferred_element_type=F32)
            tr = jnp.tanh(z[:, :MXU_DIM] + ba_ref[:, sl])
            ti = jnp.tanh(z[:, MXU_DIM:] + bi_ref[:, sl])
            nl = (tr + 1.0) * sp_ref[:, sl]
            a = jnp.exp2(nl * (-LOG2E))
            a_scr[bb, :, sl] = a
            w = jnp.tanh(nl) * (1.0 + a * a)
            u_scr[bb, :, sl] = w * lax.rsqrt(jnp.maximum(w, F32_TINY)) * ((ti + 1.0) * xc[:, sl])

    rows = SUBLANES_F32
    half = blk // 2
    ngrp = half // rows
    order = range(rows - 1, -1, -1) if reverse else range(rows)
    parity = (1, 0) if reverse else (0, 1)
    spans = range(tm // blk - 1, -1, -1) if reverse else range(tm // blk)

    def group(g, hs, off):
        gg = (ngrp - 1 - g) if reverse else g
        lo = pl.multiple_of(off + gg * rows, rows)
        hi = pl.multiple_of(off + half + gg * rows, rows)
        out = pl.multiple_of(off + gg * 2 * rows, 2 * rows)
        hs = list(hs)
        for j in order:
            for par in parity:
                start = hi if par else lo
                for bb in range(nb):
                    av = a_scr.at[bb, pl.ds(start, rows), :]
                    uv = u_scr.at[bb, pl.ds(start, rows), :]
                    yv = y_scr.at[bb, pl.ds(out, 2 * rows), :]
                    hs[bb] = av[j:j + 1, :] * hs[bb] + uv[j:j + 1, :]
                    yv[2 * j + par:2 * j + par + 1, :] = hs[bb]
        return tuple(hs)

    hs = tuple(h_scr[bb] for bb in range(nb))
    for span in spans:
        hs = lax.fori_loop(0, ngrp, functools.partial(group, off=span * blk), hs)
    for bb in range(nb):
        h_scr[bb] = hs[bb]
        hfin_ref[bb] = hs[bb]

    if final:
        z = gx_ref[...] * (y_scr[...] + yo_ref[...])
        ms = jnp.mean(z * z, axis=-1, keepdims=True)
        y_ref[...] = (z * lax.rsqrt(ms + EPS) * gn_ref[...]).astype(BF16)
    else:
        y_ref[...] = y_scr[...]


def _rglru(xc, wg, ba, bi, sp, h0, *, reverse, tm, blk, final_args=None):
    b, s, d = xc.shape
    nt = s // tm
    final = final_args is not None
    row = (lambda i: (0, nt - 1 - i, 0)) if reverse else (lambda i: (0, i, 0))
    in_specs = [pl.BlockSpec((b, tm, d), row), _const_spec(wg.shape),
                _const_spec(ba.shape), _const_spec(bi.shape), _const_spec(sp.shape),
                _const_spec(h0.shape)]
    args = [xc, wg, ba, bi, sp, h0]
    if final:
        y_other, gx, gn = final_args
        in_specs += [pl.BlockSpec((b, tm, d), row), pl.BlockSpec((b, tm, d), row),
                     _const_spec(gn.shape)]
        args += [y_other, gx, gn]
    return pl.pallas_call(
        functools.partial(_rglru_kernel, reverse=reverse, final=final, blk=blk),
        grid=(nt,),
        in_specs=in_specs,
        out_specs=[pl.BlockSpec((b, tm, d), row), pl.BlockSpec((b, 1, d), lambda i: (0, 0, 0))],
        out_shape=[jax.ShapeDtypeStruct((b, s, d), BF16 if final else F32),
                   jax.ShapeDtypeStruct((b, 1, d), F32)],
        scratch_shapes=[pltpu.VMEM((b, tm, d), F32), pltpu.VMEM((b, tm, d), F32),
                        pltpu.VMEM((b, tm, d), F32), pltpu.VMEM((b, 1, d), F32)],
        compiler_params=_params("arbitrary"),
        name="rglru_" + ("bwd" if reverse else "fwd") + ("_final" if final else ""),
    )(*args)


def _attn_kernel(q_ref, kvp_ref, kvc_ref, kvn_ref, kvx_ref, bias_ref, sink_ref, gn_ref, o_ref,
                 keys, vals, keyx, valx, *, qb, nstep):
    ncol = D_ATTN // LANES
    nwin = 3 * BLOCK_Q
    step = pl.program_id(1)
    band = ((slice(0, BLOCK_Q), kvp_ref), (slice(BLOCK_Q, BLOCK_Q * (qb + 1)), kvc_ref),
            (slice(BLOCK_Q * (qb + 1), BLOCK_Q * (qb + 2)), kvn_ref))
    for rows, kv_ref in band:
        keys[rows, :] = kv_ref[:, :D_KV]
        vals[rows, :D_KV] = kv_ref[:, D_KV:]
    keyx[...] = kvx_ref[:, :D_KV]
    valx[:, :D_KV] = kvx_ref[:, D_KV:]
    vals[:, D_KV:] = jnp.ones((vals.shape[0], LANES), BF16)
    valx[:, D_KV:] = jnp.ones((valx.shape[0], LANES), BF16)

    low = lax.broadcasted_iota(jnp.int32, (BLOCK_Q, LANES), 1) < HEAD_DIM
    first = lax.broadcasted_iota(jnp.int32, (2 * BLOCK_Q, 1), 0) < BLOCK_Q
    zero = jnp.zeros((BLOCK_Q, LANES), BF16)
    nt_dims = (((1,), (1,)), ((), ()))
    def block_bias(x):
        variant = 1
        if x == 0:
            variant = jnp.where(step == 0, 0, variant)
        if x == qb - 1:
            variant = jnp.where(step == nstep - 1, 2, variant)
        return bias_ref[variant]

    def scores(x, c, bias):
        win = slice(BLOCK_Q * x, BLOCK_Q * x + nwin)
        t = q_ref[BLOCK_Q * x:BLOCK_Q * (x + 1), LANES * c:LANES * (c + 1)]
        qs = jnp.concatenate([jnp.where(low, t, zero), jnp.where(low, zero, t)], axis=0)
        s_loc = lax.dot_general(qs, keys[win, :], nt_dims, preferred_element_type=F32)
        s_ctx = lax.dot_general(qs, keyx[...], nt_dims, preferred_element_type=F32)
        s_loc = (s_loc.reshape(2, BLOCK_Q, nwin) + bias[None]).reshape(2 * BLOCK_Q, nwin)
        sink = jnp.where(first, sink_ref[c], sink_ref[ncol + c])
        m = jnp.maximum(jnp.maximum(jnp.max(s_loc, axis=-1, keepdims=True),
                                    jnp.max(s_ctx, axis=-1, keepdims=True)), sink)
        e_loc = jnp.exp2(s_loc - m).astype(BF16)
        e_ctx = jnp.exp2(s_ctx - m).astype(BF16)
        return x, e_loc, e_ctx, jnp.exp2(sink - m)

    def weighted(x, e_loc, e_ctx, e_sink):
        win = slice(BLOCK_Q * x, BLOCK_Q * x + nwin)
        pv = (jnp.dot(e_loc, vals[win, :], preferred_element_type=F32)
              + jnp.dot(e_ctx, valx[...], preferred_element_type=F32))
        o = pv[:, :D_KV] / (pv[:, D_KV:] + e_sink)
        return jnp.where(low, o[:BLOCK_Q], o[BLOCK_Q:])

    def finish(x, outs):
        ms = sum(jnp.sum(t * t, axis=-1, keepdims=True) for t in outs) * (1.0 / D_ATTN)
        inv = lax.rsqrt(ms + EPS)
        for c, t in enumerate(outs):
            sl = slice(LANES * c, LANES * (c + 1))
            o_ref[BLOCK_Q * x:BLOCK_Q * (x + 1), sl] = (t * inv * gn_ref[:, sl]).astype(BF16)

    pending = []
    outs = [[] for _ in range(qb)]

    def retire():
        chain = pending.pop(0)
        x = chain[0]
        outs[x].append(weighted(*chain))
        if len(outs[x]) == ncol:
            finish(x, outs[x])

    for x in range(qb):
        bias = block_bias(x)
        for c in range(ncol):
            pending.append(scores(x, c, bias))
            if len(pending) > ATTN_LOOKAHEAD:
                retire()
    while pending:
        retire()


def _attention(q, kv, kvx, bias, sink, gn):
    b, s, _ = q.shape
    qb = ATTN_QB
    nblk = s // BLOCK_Q
    nstep = nblk // qb
    lx = kvx.shape[1]
    cur = lambda bb, n: (bb, n, 0)
    prev = lambda bb, n: (bb, jnp.maximum(n * qb - 1, 0), 0)
    nxt = lambda bb, n: (bb, jnp.minimum((n + 1) * qb, nblk - 1), 0)
    halo = lambda im: pl.BlockSpec((None, BLOCK_Q, 2 * D_KV), im)
    return pl.pallas_call(
        functools.partial(_attn_kernel, qb=qb, nstep=nstep),
        grid=(b, nstep),
        in_specs=[pl.BlockSpec((None, qb * BLOCK_Q, D_ATTN), cur),
                  halo(prev), pl.BlockSpec((None, qb * BLOCK_Q, 2 * D_KV), cur), halo(nxt),
                  pl.BlockSpec((None, lx, 2 * D_KV), lambda bb, n: (bb, 0, 0)),
                  _const_spec(bias.shape), pl.BlockSpec(memory_space=pltpu.SMEM),
                  _const_spec(gn.shape)],
        out_specs=pl.BlockSpec((None, qb * BLOCK_Q, D_ATTN), cur),
        out_shape=jax.ShapeDtypeStruct((b, s, D_ATTN), BF16),
        scratch_shapes=[pltpu.VMEM(((qb + 2) * BLOCK_Q, D_KV), BF16),
                        pltpu.VMEM(((qb + 2) * BLOCK_Q, D_KV + LANES), BF16),
                        pltpu.VMEM((lx, D_KV), BF16),
                        pltpu.VMEM((lx, D_KV + LANES), BF16)],
        compiler_params=_params("arbitrary", "arbitrary"),
        name="attention",
    )(q, kv, kv, kv, kvx, bias, sink, gn)


def _attn_bias():
    i = np.arange(BLOCK_Q)[:, None]
    j = np.arange(3 * BLOCK_Q)[None, :]
    band = np.abs(i + BLOCK_Q - j) <= WINDOW
    variants = [band & (j >= BLOCK_Q), band, band & (j < 2 * BLOCK_Q)]
    return jnp.asarray(np.stack([np.where(ok, 0.0, NEG_INF) for ok in variants]).astype(np.float32))


def _ffn_kernel(rn_ref, rnp_ref, rnn_ref, an_ref, anp_ref, ann_ref, x_ref, xp_ref, xn_ref,
                wo_ref, g1_ref, sh_ref, sc_ref, ng_ref, wu_ref, cw_ref, cb_ref, wd_ref, g2_ref, fg_ref,
                o_ref, mix, x1s, hext, ug, uv, act, acc, nat, *, nt, tm):
    d_ff = wd_ref.shape[0]
    nc = d_ff // FFN_CHUNK
    span = lambda j, branch: slice(branch * d_ff + j * FFN_CHUNK, branch * d_ff + (j + 1) * FFN_CHUNK)
    i = pl.program_id(1)
    halo = SUBLANES_BF16
    body = slice(halo, halo + tm)
    tail = slice(halo + tm, 2 * halo + tm)
    for rows, r_ref, a_ref in ((slice(0, halo), rnp_ref, anp_ref), (body, rn_ref, an_ref),
                               (tail, rnn_ref, ann_ref)):
        mix[rows, :D_RNN] = r_ref[...]
        mix[rows, D_RNN:] = a_ref[...]
    proj = jnp.dot(mix[...], wo_ref[...], preferred_element_type=F32)

    def residual(xin_ref, rows):
        return xin_ref[...] + g1_ref[...] * proj[rows]

    gain = ng_ref[...] * (1.0 + sc_ref[...])

    def modulated(x1):
        ms = jnp.mean(x1 * x1, axis=-1, keepdims=True)
        return (x1 * lax.rsqrt(ms + EPS) * gain + sh_ref[...]).astype(BF16)

    x1 = residual(x_ref, body)
    x1s[...] = x1
    hext[body] = modulated(x1)
    zeros = jnp.zeros((halo, hext.shape[1]), BF16)
    hext[0:halo] = jnp.where(i > 0, modulated(residual(xp_ref, slice(0, halo))), zeros)
    hext[tail] = jnp.where(i < nt - 1, modulated(residual(xn_ref, tail)), zeros)

    base = halo - FFN_CONV_LEFT

    half = tm // 2
    nslab = FFN_CHUNK // LANES

    def up(j, slot):
        he = hext[...]
        g = jnp.dot(he, wu_ref[:, span(j,0)], preferred_element_type=F32)
        v = jnp.dot(he, wu_ref[:, span(j,1)], preferred_element_type=F32)
        for s in range(nslab):
            ug[slot, s] = g[:, LANES * s:LANES * (s + 1)]
            uv[slot, s] = v[:, LANES * s:LANES * (s + 1)]

    def conv(u, slot, cw, cb):
        cols = []
        for s in range(nslab):
            sl = slice(LANES * s, LANES * (s + 1))
            taps = [u[slot, s, pl.ds(base + t, half, stride=2), :] for t in range(FFN_CONV_W + 1)]
            even = cb[:, sl]
            odd = cb[:, sl]
            for t in range(FFN_CONV_W):
                even = even + taps[t] * cw[t:t + 1, sl]
                odd = odd + taps[t + 1] * cw[t:t + 1, sl]
            cols.append(jnp.concatenate([even, odd], axis=0))
        return jnp.concatenate(cols, axis=1)

    def gate(j, slot):
        hg = 0.5 * conv(ug, slot, cw_ref[:, span(j,0)], cb_ref[:, span(j,0)])
        silu = hg * jnp.tanh(hg) + hg
        val = conv(uv, slot, cw_ref[:, span(j,1)], cb_ref[:, span(j,1)])
        act[slot] = (silu * val).astype(BF16)

    def down(j, slot):
        return jnp.dot(act[slot], wd_ref[span(j,0), :], preferred_element_type=F32)

    up(0, 0)
    for j in range(nc):
        if j + 1 < nc:
            up(j + 1, (j + 1) % 2)
        gate(j, j % 2)
        if j == 1:
            acc[...] = down(0, 0)
        elif j > 1:
            acc[...] += down(j - 1, (j - 1) % 2)
    mixed = acc[...] + down(nc - 1, (nc - 1) % 2)
    nlane = mixed.shape[1] // LANES
    for s in range(nlane):
        sl = slice(LANES * s, LANES * (s + 1))
        nat[s, pl.ds(0, half, stride=2), :] = mixed[:half, sl]
        nat[s, pl.ds(1, half, stride=2), :] = mixed[half:, sl]
    ffn = jnp.concatenate([nat[s] for s in range(nlane)], axis=1)
    y = x1s[...] + g2_ref[...] * ffn
    ms = jnp.mean(y * y, axis=-1, keepdims=True)
    o_ref[...] = y * lax.rsqrt(ms + EPS) * fg_ref[...]


def _mixer_out_ffn(rn, an, x, wo, mod, norm_g, wu, cw, cb, wd, fg, *, tm):
    b, s, d = x.shape
    nt = s // tm
    assert wd.shape[0] % FFN_CHUNK == 0 and (wd.shape[0] // FFN_CHUNK) % 2 == 1
    halo = SUBLANES_BF16
    per = tm // halo
    nblk = s // halo
    row = lambda bb, i: (bb, i, 0)
    prev = lambda bb, i: (bb, jnp.maximum(i * per - 1, 0), 0)
    nxt = lambda bb, i: (bb, jnp.minimum((i + 1) * per, nblk - 1), 0)
    banded = lambda w: [pl.BlockSpec((None, tm, w), row), pl.BlockSpec((None, halo, w), prev),
                        pl.BlockSpec((None, halo, w), nxt)]
    return pl.pallas_call(
        functools.partial(_ffn_kernel, nt=nt, tm=tm),
        grid=(b, nt),
        in_specs=banded(D_RNN) + banded(D_ATTN) + banded(d)
        + [_const_spec(wo.shape), _mod_spec(d, 2), _mod_spec(d, 3), _mod_spec(d, 4),
           _const_spec(norm_g.shape),
           _const_spec(wu.shape), _const_spec(cw.shape), _const_spec(cb.shape),
           _const_spec(wd.shape), _mod_spec(d, 5), _const_spec(fg.shape)],
        out_specs=pl.BlockSpec((None, tm, d), row),
        out_shape=jax.ShapeDtypeStruct((b, s, d), F32),
        scratch_shapes=[pltpu.VMEM((tm + 2 * halo, D_RNN + D_ATTN), BF16),
                        pltpu.VMEM((tm, d), F32),
                        pltpu.VMEM((tm + 2 * halo, d), BF16),
                        pltpu.VMEM((2, FFN_CHUNK // LANES, tm + 2 * halo, LANES), F32),
                        pltpu.VMEM((2, FFN_CHUNK // LANES, tm + 2 * halo, LANES), F32),
                        pltpu.VMEM((2, tm, FFN_CHUNK), BF16),
                        pltpu.VMEM((tm, d), F32),
                        pltpu.VMEM((d // LANES, tm, LANES), F32)],
        compiler_params=_params("arbitrary", "arbitrary"),
        name="out_proj_conv_ffn",
    )(rn, rn, rn, an, an, an, x, x, x, wo, mod, mod, mod, norm_g, wu, cw, cb, wd, mod, fg)


def _rope_tables(s):
    half = HEAD_DIM // 4
    inv = ROPE_THETA ** (-np.arange(half, dtype=np.float64) / half)

    def tables(npos, lanes_first):
        ang = np.arange(npos, dtype=np.float64)[:, None] * inv[None, :]
        zero = np.zeros((npos, 2 * half))
        cos = np.concatenate([np.cos(ang)] * 2, axis=1)
        sin = np.concatenate([-np.sin(ang), np.sin(ang)], axis=1)
        order = (lambda t: [t, zero]) if lanes_first else (lambda t: [zero, t])
        reps = LANES // HEAD_DIM
        return [jnp.asarray(np.tile(np.concatenate(order(t), axis=1), (1, reps)).astype(np.float32))
                for t in (cos, sin)]

    cosr, sinr = tables(s // GRID_W, True)
    cosc, sinc = tables(GRID_W, False)
    return cosr, sinr, cosc, sinc


def _gate_weights(w_a, w_i):
    per = MXU_DIM // RNN_BLOCK_W
    eye = jnp.eye(per, dtype=F32)

    def dense(w):
        w = w.reshape(RNN_BLOCKS // per, per, RNN_BLOCK_W, RNN_BLOCK_W)
        m = w[:, :, :, None, :] * eye[None, :, None, :, None]
        return m.reshape(RNN_BLOCKS // per, MXU_DIM, MXU_DIM)

    return jnp.concatenate([dense(w_a), dense(w_i)], axis=2).astype(BF16)


def _interleave_heads(w, axis):
    shape = w.shape
    w = w.reshape(shape[:axis] + (N_KV_HEADS, Q_PER_KV, HEAD_DIM) + shape[axis + 1:])
    return jnp.swapaxes(w, axis, axis + 1).reshape(shape)


def kernel(x, c, ctx, c_ctx, w_mod, b_mod, norm1_g, w_in, rnn_conv_w, rnn_conv_b, lru_w_a, lru_b_a,
           lru_w_i, lru_b_i, lru_lam, attn_sink, gn_rnn, gn_attn, w_out, norm2_g, w_up, ffn_conv_w,
           ffn_conv_b, w_down, final_g):
    assert w_mod.shape[0] == 1, "one layer: the last layer's context outputs are never consumed"
    b, s, d = x.shape
    lx = ctx.shape[1]

    cond = jnp.concatenate([c, c_ctx[None], jnp.zeros((SUBLANES_F32 - b - 1, d), F32)], axis=0)
    mod = _modulation(cond, b + 1, w_mod[0], b_mod[0])[:, None, :]

    wi = w_in[0].astype(BF16)
    q0 = 2 * D_RNN
    wi = jnp.concatenate([wi[:, :q0], _interleave_heads(wi[:, q0:q0 + D_ATTN], 1),
                          wi[:, q0 + D_ATTN:]], axis=1)
    wo = w_out[0].astype(BF16)
    wo = jnp.concatenate([wo[:D_RNN], _interleave_heads(wo[D_RNN:], 0)], axis=0)
    gn_a = _interleave_heads(gn_attn[0], 0)[None]
    gn_r = gn_rnn[0][None]
    cw = 0.5 * rnn_conv_w[0]
    cb = 0.5 * rnn_conv_b[0][None]
    sp = (0.5 * LRU_C) * jax.nn.softplus(-lru_lam[0])

    tables = _rope_tables(s)
    xc, gx, q, kv = _in_proj(x, mod, None, norm1_g, wi, cw, cb, tables, rope=True, tm=TM_PROJ)
    xcc, _, _, kvx = _in_proj(ctx, mod, b, norm1_g, wi, cw, cb, tables, rope=False, tm=lx)

    h0 = jnp.zeros((b, 1, D_RNN), F32)
    y_dir = None
    for dr in range(2):
        wg = _gate_weights(lru_w_a[0, dr], lru_w_i[0, dr])
        gate_args = (wg, 0.5 * lru_b_a[0, dr][None], 0.5 * lru_b_i[0, dr][None], sp[dr][None])
        rev = dr == 1
        _, h_ctx = _rglru(xcc, *gate_args, h0, reverse=rev, tm=lx, blk=min(lx, SUB_PROJ))
        fin = (y_dir, gx, gn_r) if rev else None
        y_dir, _ = _rglru(xc, *gate_args, h_ctx, reverse=rev, tm=TM_SCAN, blk=SUB_PROJ,
                          final_args=fin)
    rn = y_dir

    an = _attention(q, kv, kvx, _attn_bias(), attn_sink[0] * LOG2E, gn_a)

    return _mixer_out_ffn(rn, an, x, wo, mod, norm2_g, w_up[0].astype(BF16), ffn_conv_w[0],
                          ffn_conv_b[0][None], w_down[0].astype(BF16), final_g[None], tm=TM_FFN)
```

```python
import functools

import jax
import jax.numpy as jnp
import numpy as np
from jax import lax
from jax.experimental import pallas as pl
from jax.experimental.pallas import tpu as pltpu

F32 = jnp.float32
BF16 = jnp.bfloat16

EPS = 1e-6
GRID_W = 64
D_RNN = 512
RNN_BLOCKS = 8
RNN_BLOCK_W = D_RNN // RNN_BLOCKS
RNN_CONV_W = 4
RNN_CONV_LEFT = 2
LRU_C = 8.0
HEAD_DIM = 64
N_Q_HEADS = 8
N_KV_HEADS = 2
Q_PER_KV = N_Q_HEADS // N_KV_HEADS
D_ATTN = N_Q_HEADS * HEAD_DIM
D_KV = N_KV_HEADS * HEAD_DIM
WINDOW = 128
BLOCK_Q = 128
ROPE_THETA = 10000.0
NEG_INF = -1e30
LOG2E = 1.4426950408889634
F32_TINY = 2.0 ** -126
FFN_CONV_W = 3
FFN_CONV_LEFT = 1

LANES = 128
SUBLANES_F32 = 8
SUBLANES_BF16 = 16
MXU_DIM = 256
VMEM_LIMIT = 56 * 1024 * 1024

TM_PROJ = 2048
SUB_PROJ = 512
TM_SCAN = 1024
TM_FFN = 512
FFN_CHUNK = 256
ATTN_QB = 16
ATTN_LOOKAHEAD = 1


def _params(*sem):
    return pltpu.CompilerParams(dimension_semantics=sem, vmem_limit_bytes=VMEM_LIMIT)


def _const_spec(shape):
    zeros = (0,) * len(shape)
    return pl.BlockSpec(shape, lambda *_: zeros, pipeline_mode=pl.Buffered(1))


def _mod_kernel(condt_ref, w_ref, b_ref, o_ref, *, nrows):
    st = condt_ref[...]
    st = st * jax.nn.sigmoid(st)
    w = w_ref[...]
    rows = [jnp.sum(w * st[:, r:r + 1], axis=0, keepdims=True) for r in range(nrows)]
    rows.append(jnp.zeros((o_ref.shape[0] - nrows, w.shape[1]), F32))
    o_ref[...] = jnp.concatenate(rows, axis=0) + b_ref[...]


def _modulation(cond, nrows, w_mod, b_mod):
    rows, d = cond.shape
    n = w_mod.shape[1]
    tn = 768
    return pl.pallas_call(
        functools.partial(_mod_kernel, nrows=nrows),
        grid=(n // tn,),
        in_specs=[pl.BlockSpec((d, rows), lambda j: (0, 0)),
                  pl.BlockSpec((d, tn), lambda j: (0, j)),
                  pl.BlockSpec((1, tn), lambda j: (0, j))],
        out_specs=pl.BlockSpec((rows, tn), lambda j: (0, j)),
        out_shape=jax.ShapeDtypeStruct((rows, n), F32),
        compiler_params=_params("arbitrary"),
        name="modulation",
    )(cond.T, w_mod, b_mod.reshape(1, n))


def _rope_partner(t):
    lane = lax.broadcasted_iota(jnp.int32, t.shape, 1)
    first = (lane % 32) < 16
    return jnp.where(first, pltpu.roll(t, LANES - 16, 1), pltpu.roll(t, 16, 1))


def _inproj_kernel(x_ref, xp_ref, xn_ref, sh_ref, sc_ref, ng_ref, w_ref, cw_ref, cb_ref,
                   cosr_ref, sinr_ref, cosc_ref, sinc_ref,
                   xc_ref, gx_ref, q_ref, kv_ref, hsc, pslab, *, rope, sub, nt):
    i = pl.program_id(1)
    tm = x_ref.shape[0]
    halo = SUBLANES_BF16
    half = sub // 2
    base = halo - RNN_CONV_LEFT
    q0 = D_RNN
    k0 = q0 + D_ATTN
    v0 = k0 + D_KV
    scale = HEAD_DIM ** -0.5 * LOG2E
    nslab = D_RNN // LANES
    gain = ng_ref[...] * (1.0 + sc_ref[...])

    def modulated(x):
        ms = jnp.mean(x * x, axis=-1, keepdims=True)
        return (x * lax.rsqrt(ms + EPS) * gain + sh_ref[...]).astype(BF16)

    zeros = jnp.zeros((halo, hsc.shape[1]), BF16)
    hsc[0:halo] = jnp.where(i > 0, modulated(xp_ref[...]), zeros)
    hsc[halo + tm:] = jnp.where(i < nt - 1, modulated(xn_ref[...]), zeros)

    for n in range(tm // sub):
        hsc[halo + sub * n:halo + sub * (n + 1)] = modulated(x_ref[sub * n:sub * (n + 1), :])

    cw = cw_ref[...]
    cb = cb_ref[...]
    def project(n):
        pr = jnp.dot(hsc[sub * n:sub * (n + 1) + 2 * halo], w_ref[:, :D_RNN],
                     preferred_element_type=F32)
        p = jnp.dot(hsc[halo + sub * n:halo + sub * (n + 1)], w_ref[:, D_RNN:],
                    preferred_element_type=F32)
        return pr, p

    def epilogue(n, pr, p):
        rows = slice(sub * n, sub * (n + 1))
        cols = []
        for s in range(nslab):
            sl = slice(LANES * s, LANES * (s + 1))
            slab = pslab.at[n * nslab + s]
            slab[...] = pr[:, sl]
            taps = [slab[pl.ds(base + j, half, stride=2), :] for j in range(RNN_CONV_W + 1)]
            even = cb[:, sl]
            odd = cb[:, sl]
            for j in range(RNN_CONV_W):
                even = even + taps[j] * cw[j:j + 1, sl]
                odd = odd + taps[j + 1] * cw[j:j + 1, sl]
            cols.append(jnp.concatenate([even, odd], axis=0))
        xc_ref[rows, :] = jnp.concatenate(cols, axis=1)
        gx_ref[rows, :] = jax.nn.gelu(p[:, :D_RNN]).astype(BF16)
        cols = [p[:, q0 + LANES * c:q0 + LANES * (c + 1)] for c in range(D_ATTN // LANES)]
        k = p[:, k0:v0]
        if rope:
            per = sub // GRID_W
            trows = slice(per * n, per * (n + 1))
            expand = lambda r_ref, c_ref: (r_ref[trows, :][:, None, :]
                                           + c_ref[...][None, :, :]).reshape(sub, LANES)
            cos = expand(cosr_ref, cosc_ref)
            sin = expand(sinr_ref, sinc_ref)
            cols = [t * cos + _rope_partner(t) * sin for t in cols]
            k = k * cos + _rope_partner(k) * sin
        for c, t in enumerate(cols):
            q_ref[rows, LANES * c:LANES * (c + 1)] = (t * scale).astype(BF16)
        kv_ref[rows, :D_KV] = k.astype(BF16)
        kv_ref[rows, D_KV:] = p[:, v0:v0 + D_KV].astype(BF16)

    nsub = tm // sub
    ready = project(0)
    for n in range(nsub):
        following = project(n + 1) if n + 1 < nsub else None
        epilogue(n, *ready)
        ready = following


def _mod_spec(d, chunk, row=None):
    return pl.BlockSpec((None, 1, d), lambda bb, i: (bb if row is None else row, 0, chunk))


def _in_proj(x, mod, mod_row, norm_g, w_in, cw, cb, tables, *, rope, tm):
    b, s, d = x.shape
    n = w_in.shape[1]
    nt = s // tm
    sub = min(tm, SUB_PROJ)
    halo = SUBLANES_BF16
    per = tm // halo
    row = lambda bb, i: (bb, i, 0)
    prev = lambda bb, i: (bb, jnp.maximum(i * per - 1, 0), 0)
    nxt = lambda bb, i: (bb, jnp.minimum((i + 1) * per, s // halo - 1), 0)
    cosr, sinr, cosc, sinc = tables
    if rope:
        assert tm % GRID_W == 0
        rtab = pl.BlockSpec((tm // GRID_W, LANES), lambda bb, i: (i, 0))
    else:
        rtab = _const_spec(cosr.shape)
    outs = [(D_RNN, F32), (D_RNN, BF16), (D_ATTN, BF16), (2 * D_KV, BF16)]
    return pl.pallas_call(
        functools.partial(_inproj_kernel, rope=rope, sub=sub, nt=nt),
        grid=(b, nt),
        in_specs=[pl.BlockSpec((None, tm, d), row),
                  pl.BlockSpec((None, halo, d), prev),
                  pl.BlockSpec((None, halo, d), nxt),
                  _mod_spec(d, 0, mod_row), _mod_spec(d, 1, mod_row), _const_spec(norm_g.shape),
                  _const_spec((d, n)), _const_spec(cw.shape), _const_spec(cb.shape),
                  rtab, rtab, _const_spec(cosc.shape), _const_spec(sinc.shape)],
        out_specs=[pl.BlockSpec((None, tm, w), row) for w, _ in outs],
        out_shape=[jax.ShapeDtypeStruct((b, s, w), dt) for w, dt in outs],
        scratch_shapes=[pltpu.VMEM((tm + 2 * halo, d), BF16),
                        pltpu.VMEM((tm // sub * (D_RNN // LANES), sub + 2 * halo, LANES), F32)],
        compiler_params=_params("arbitrary", "arbitrary"),
        name="in_proj_rope" if rope else "in_proj_ctx",
    )(x, x, x, mod, mod, norm_g, w_in, cw, cb, cosr, sinr, cosc, sinc)


def _rglru_kernel(*refs, reverse, final, blk):
    if final:
        (xc_ref, wg_ref, ba_ref, bi_ref, sp_ref, h0_ref, yo_ref, gx_ref, gn_ref,
         y_ref, hfin_ref, a_scr, u_scr, y_scr, h_scr) = refs
    else:
        (xc_ref, wg_ref, ba_ref, bi_ref, sp_ref, h0_ref,
         y_ref, hfin_ref, a_scr, u_scr, y_scr, h_scr) = refs
    i = pl.program_id(0)
    nb, tm, _ = xc_ref.shape

    @pl.when(i == 0)
    def _():
        h_scr[...] = h0_ref[...]

    for bb in range(nb):
        xc = xc_ref[bb]
        xb = xc.astype(BF16)
        for g in range(D_RNN // MXU_DIM):
            sl = slice(MXU_DIM * g, MXU_DIM * (g + 1))
            z = jnp.dot(xb[:, sl], wg_ref[g], preferred_element_type=F32)
            tr = jnp.tanh(z[:, :MXU_DIM] + ba_ref[:, sl])
            ti = jnp.tanh(z[:, MXU_DIM:] + bi_ref[:, sl])
            nl = (tr + 1.0) * sp_ref[:, sl]
            a = jnp.exp2(nl * (-LOG2E))
            a_scr[bb, :, sl] = a
            w = jnp.tanh(nl) * (1.0 + a * a)
            u_scr[bb, :, sl] = w * lax.rsqrt(jnp.maximum(w, F32_TINY)) * ((ti + 1.0) * xc[:, sl])

    rows = SUBLANES_F32
    half = blk // 2
    ngrp = half // rows
    order = range(rows - 1, -1, -1) if reverse else range(rows)
    parity = (1, 0) if reverse else (0, 1)
    spans = range(tm // blk - 1, -1, -1) if reverse else range(tm // blk)

    def group(g, hs, off):
        gg = (ngrp - 1 - g) if reverse else g
        lo = pl.multiple_of(off + gg * rows, rows)
        hi = pl.multiple_of(off + half + gg * rows, rows)
        out = pl.multiple_of(off + gg * 2 * rows, 2 * rows)
        hs = list(hs)
        for j in order:
            for par in parity:
                start = hi if par else lo
                for bb in range(nb):
                    av = a_scr.at[bb, pl.ds(start, rows), :]
                    uv = u_scr.at[bb, pl.ds(start, rows), :]
                    yv = y_scr.at[bb, pl.ds(out, 2 * rows), :]
                    hs[bb] = av[j:j + 1, :] * hs[bb] + uv[j:j + 1, :]
                    yv[2 * j + par:2 * j + par + 1, :] = hs[bb]
        return tuple(hs)

    hs = tuple(h_scr[bb] for bb in range(nb))
    for span in spans:
        hs = lax.fori_loop(0, ngrp, functools.partial(group, off=span * blk), hs)
    for bb in range(nb):
        h_scr[bb] = hs[bb]
        hfin_ref[bb] = hs[bb]

    if final:
        z = gx_ref[...] * (y_scr[...] + yo_ref[...])
        ms = jnp.mean(z * z, axis=-1, keepdims=True)
        y_ref[...] = (z * lax.rsqrt(ms + EPS) * gn_ref[...]).astype(BF16)
    else:
        y_ref[...] = y_scr[...]


def _rglru(xc, wg, ba, bi, sp, h0, *, reverse, tm, blk, final_args=None):
    b, s, d = xc.shape
    nt = s // tm
    final = final_args is not None
    row = (lambda i: (0, nt - 1 - i, 0)) if reverse else (lambda i: (0, i, 0))
    in_specs = [pl.BlockSpec((b, tm, d), row), _const_spec(wg.shape),
                _const_spec(ba.shape), _const_spec(bi.shape), _const_spec(sp.shape),
                _const_spec(h0.shape)]
    args = [xc, wg, ba, bi, sp, h0]
    if final:
        y_other, gx, gn = final_args
        in_specs += [pl.BlockSpec((b, tm, d), row), pl.BlockSpec((b, tm, d), row),
                     _const_spec(gn.shape)]
        args += [y_other, gx, gn]
    return pl.pallas_call(
        functools.partial(_rglru_kernel, reverse=reverse, final=final, blk=blk),
        grid=(nt,),
        in_specs=in_specs,
        out_specs=[pl.BlockSpec((b, tm, d), row), pl.BlockSpec((b, 1, d), lambda i: (0, 0, 0))],
        out_shape=[jax.ShapeDtypeStruct((b, s, d), BF16 if final else F32),
                   jax.ShapeDtypeStruct((b, 1, d), F32)],
        scratch_shapes=[pltpu.VMEM((b, tm, d), F32), pltpu.VMEM((b, tm, d), F32),
                        pltpu.VMEM((b, tm, d), F32), pltpu.VMEM((b, 1, d), F32)],
        compiler_params=_params("arbitrary"),
        name="rglru_" + ("bwd" if reverse else "fwd") + ("_final" if final else ""),
    )(*args)


def _attn_kernel(q_ref, kvp_ref, kvc_ref, kvn_ref, kvx_ref, bias_ref, sink_ref, gn_ref, o_ref,
                 keys, vals, keyx, valx, *, qb, nstep):
    ncol = D_ATTN // LANES
    nwin = 3 * BLOCK_Q
    step = pl.program_id(1)
    band = ((slice(0, BLOCK_Q), kvp_ref), (slice(BLOCK_Q, BLOCK_Q * (qb + 1)), kvc_ref),
            (slice(BLOCK_Q * (qb + 1), BLOCK_Q * (qb + 2)), kvn_ref))
    for rows, kv_ref in band:
        keys[rows, :] = kv_ref[:, :D_KV]
        vals[rows, :D_KV] = kv_ref[:, D_KV:]
    keyx[...] = kvx_ref[:, :D_KV]
    valx[:, :D_KV] = kvx_ref[:, D_KV:]
    vals[:, D_KV:] = jnp.ones((vals.shape[0], LANES), BF16)
    valx[:, D_KV:] = jnp.ones((valx.shape[0], LANES), BF16)

    low = lax.broadcasted_iota(jnp.int32, (BLOCK_Q, LANES), 1) < HEAD_DIM
    first = lax.broadcasted_iota(jnp.int32, (2 * BLOCK_Q, 1), 0) < BLOCK_Q
    zero = jnp.zeros((BLOCK_Q, LANES), BF16)
    nt_dims = (((1,), (1,)), ((), ()))
    def block_bias(x):
        variant = 1
        if x == 0:
            variant = jnp.where(step == 0, 0, variant)
        if x == qb - 1:
            variant = jnp.where(step == nstep - 1, 2, variant)
        return bias_ref[variant]

    def scores(x, c, bias):
        win = slice(BLOCK_Q * x, BLOCK_Q * x + nwin)
        t = q_ref[BLOCK_Q * x:BLOCK_Q * (x + 1), LANES * c:LANES * (c + 1)]
        qs = jnp.concatenate([jnp.where(low, t, zero), jnp.where(low, zero, t)], axis=0)
        s_loc = lax.dot_general(qs, keys[win, :], nt_dims, preferred_element_type=F32)
        s_ctx = lax.dot_general(qs, keyx[...], nt_dims, preferred_element_type=F32)
        s_loc = (s_loc.reshape(2, BLOCK_Q, nwin) + bias[None]).reshape(2 * BLOCK_Q, nwin)
        sink = jnp.where(first, sink_ref[c], sink_ref[ncol + c])
        m = jnp.maximum(jnp.maximum(jnp.max(s_loc, axis=-1, keepdims=True),
                                    jnp.max(s_ctx, axis=-1, keepdims=True)), sink)
        e_loc = jnp.exp2(s_loc - m).astype(BF16)
        e_ctx = jnp.exp2(s_ctx - m).astype(BF16)
        return x, e_loc, e_ctx, jnp.exp2(sink - m)

    def weighted(x, e_loc, e_ctx, e_sink):
        win = slice(BLOCK_Q * x, BLOCK_Q * x + nwin)
        pv = (jnp.dot(e_loc, vals[win, :], preferred_element_type=F32)
              + jnp.dot(e_ctx, valx[...], preferred_element_type=F32))
        o = pv[:, :D_KV] / (pv[:, D_KV:] + e_sink)
        return jnp.where(low, o[:BLOCK_Q], o[BLOCK_Q:])

    def finish(x, outs):
        ms = sum(jnp.sum(t * t, axis=-1, keepdims=True) for t in outs) * (1.0 / D_ATTN)
        inv = lax.rsqrt(ms + EPS)
        for c, t in enumerate(outs):
            sl = slice(LANES * c, LANES * (c + 1))
            o_ref[BLOCK_Q * x:BLOCK_Q * (x + 1), sl] = (t * inv * gn_ref[:, sl]).astype(BF16)

    pending = []
    outs = [[] for _ in range(qb)]

    def retire():
        chain = pending.pop(0)
        x = chain[0]
        outs[x].append(weighted(*chain))
        if len(outs[x]) == ncol:
            finish(x, outs[x])

    for x in range(qb):
        bias = block_bias(x)
        for c in range(ncol):
            pending.append(scores(x, c, bias))
            if len(pending) > ATTN_LOOKAHEAD:
                retire()
    while pending:
        retire()


def _attention(q, kv, kvx, bias, sink, gn):
    b, s, _ = q.shape
    qb = ATTN_QB
    nblk = s // BLOCK_Q
    nstep = nblk // qb
    lx = kvx.shape[1]
    cur = lambda bb, n: (bb, n, 0)
    prev = lambda bb, n: (bb, jnp.maximum(n * qb - 1, 0), 0)
    nxt = lambda bb, n: (bb, jnp.minimum((n + 1) * qb, nblk - 1), 0)
    halo = lambda im: pl.BlockSpec((None, BLOCK_Q, 2 * D_KV), im)
    return pl.pallas_call(
        functools.partial(_attn_kernel, qb=qb, nstep=nstep),
        grid=(b, nstep),
        in_specs=[pl.BlockSpec((None, qb * BLOCK_Q, D_ATTN), cur),
                  halo(prev), pl.BlockSpec((None, qb * BLOCK_Q, 2 * D_KV), cur), halo(nxt),
                  pl.BlockSpec((None, lx, 2 * D_KV), lambda bb, n: (bb, 0, 0)),
                  _const_spec(bias.shape), pl.BlockSpec(memory_space=pltpu.SMEM),
                  _const_spec(gn.shape)],
        out_specs=pl.BlockSpec((None, qb * BLOCK_Q, D_ATTN), cur),
        out_shape=jax.ShapeDtypeStruct((b, s, D_ATTN), BF16),
        scratch_shapes=[pltpu.VMEM(((qb + 2) * BLOCK_Q, D_KV), BF16),
                        pltpu.VMEM(((qb + 2) * BLOCK_Q, D_KV + LANES), BF16),
                        pltpu.VMEM((lx, D_KV), BF16),
                        pltpu.VMEM((lx, D_KV + LANES), BF16)],
        compiler_params=_params("arbitrary", "arbitrary"),
        name="attention",
    )(q, kv, kv, kv, kvx, bias, sink, gn)


def _attn_bias():
    i = np.arange(BLOCK_Q)[:, None]
    j = np.arange(3 * BLOCK_Q)[None, :]
    band = np.abs(i + BLOCK_Q - j) <= WINDOW
    variants = [band & (j >= BLOCK_Q), band, band & (j < 2 * BLOCK_Q)]
    return jnp.asarray(np.stack([np.where(ok, 0.0, NEG_INF) for ok in variants]).astype(np.float32))


def _ffn_kernel(rn_ref, rnp_ref, rnn_ref, an_ref, anp_ref, ann_ref, x_ref, xp_ref, xn_ref,
                wo_ref, g1_ref, sh_ref, sc_ref, ng_ref, wu_ref, cw_ref, cb_ref, wd_ref, g2_ref, fg_ref,
                o_ref, mix, x1s, hext, ug, uv, act, acc, nat, *, nt, tm):
    d_ff = wd_ref.shape[0]
    nc = d_ff // FFN_CHUNK
    span = lambda j, branch: slice(branch * d_ff + j * FFN_CHUNK, branch * d_ff + (j + 1) * FFN_CHUNK)
    i = pl.program_id(1)
    halo = SUBLANES_BF16
    body = slice(halo, halo + tm)
    tail = slice(halo + tm, 2 * halo + tm)
    for rows, r_ref, a_ref in ((slice(0, halo), rnp_ref, anp_ref), (body, rn_ref, an_ref),
                               (tail, rnn_ref, ann_ref)):
        mix[rows, :D_RNN] = r_ref[...]
        mix[rows, D_RNN:] = a_ref[...]
    proj = jnp.dot(mix[...], wo_ref[...], preferred_element_type=F32)

    def residual(xin_ref, rows):
        return xin_ref[...] + g1_ref[...] * proj[rows]

    gain = ng_ref[...] * (1.0 + sc_ref[...])

    def modulated(x1):
        ms = jnp.mean(x1 * x1, axis=-1, keepdims=True)
        return (x1 * lax.rsqrt(ms + EPS) * gain + sh_ref[...]).astype(BF16)

    x1 = residual(x_ref, body)
    x1s[...] = x1
    hext[body] = modulated(x1)
    zeros = jnp.zeros((halo, hext.shape[1]), BF16)
    hext[0:halo] = jnp.where(i > 0, modulated(residual(xp_ref, slice(0, halo))), zeros)
    hext[tail] = jnp.where(i < nt - 1, modulated(residual(xn_ref, tail)), zeros)

    base = halo - FFN_CONV_LEFT

    half = tm // 2
    nslab = FFN_CHUNK // LANES

    def up(j, slot):
        he = hext[...]
        g = jnp.dot(he, wu_ref[:, span(j,0)], preferred_element_type=F32)
        v = jnp.dot(he, wu_ref[:, span(j,1)], preferred_element_type=F32)
        for s in range(nslab):
            ug[slot, s] = g[:, LANES * s:LANES * (s + 1)]
            uv[slot, s] = v[:, LANES * s:LANES * (s + 1)]

    def conv(u, slot, cw, cb):
        cols = []
        for s in range(nslab):
            sl = slice(LANES * s, LANES * (s + 1))
            taps = [u[slot, s, pl.ds(base + t, half, stride=2), :] for t in range(FFN_CONV_W + 1)]
            even = cb[:, sl]
            odd = cb[:, sl]
            for t in range(FFN_CONV_W):
                even = even + taps[t] * cw[t:t + 1, sl]
                odd = odd + taps[t + 1] * cw[t:t + 1, sl]
            cols.append(jnp.concatenate([even, odd], axis=0))
        return jnp.concatenate(cols, axis=1)

    def gate(j, slot):
        hg = 0.5 * conv(ug, slot, cw_ref[:, span(j,0)], cb_ref[:, span(j,0)])
        silu = hg * jnp.tanh(hg) + hg
        val = conv(uv, slot, cw_ref[:, span(j,1)], cb_ref[:, span(j,1)])
        act[slot] = (silu * val).astype(BF16)

    def down(j, slot):
        return jnp.dot(act[slot], wd_ref[span(j,0), :], preferred_element_type=F32)

    up(0, 0)
    for j in range(nc):
        if j + 1 < nc:
            up(j + 1, (j + 1) % 2)
        gate(j, j % 2)
        if j == 1:
            acc[...] = down(0, 0)
        elif j > 1:
            acc[...] += down(j - 1, (j - 1) % 2)
    mixed = acc[...] + down(nc - 1, (nc - 1) % 2)
    nlane = mixed.shape[1] // LANES
    for s in range(nlane):
        sl = slice(LANES * s, LANES * (s + 1))
        nat[s, pl.ds(0, half, stride=2), :] = mixed[:half, sl]
        nat[s, pl.ds(1, half, stride=2), :] = mixed[half:, sl]
    ffn = jnp.concatenate([nat[s] for s in range(nlane)], axis=1)
    y = x1s[...] + g2_ref[...] * ffn
    ms = jnp.mean(y * y, axis=-1, keepdims=True)
    o_ref[...] = y * lax.rsqrt(ms + EPS) * fg_ref[...]


def _mixer_out_ffn(rn, an, x, wo, mod, norm_g, wu, cw, cb, wd, fg, *, tm):
    b, s, d = x.shape
    nt = s // tm
    assert wd.shape[0] % FFN_CHUNK == 0 and (wd.shape[0] // FFN_CHUNK) % 2 == 1
    halo = SUBLANES_BF16
    per = tm // halo
    nblk = s // halo
    row = lambda bb, i: (bb, i, 0)
    prev = lambda bb, i: (bb, jnp.maximum(i * per - 1, 0), 0)
    nxt = lambda bb, i: (bb, jnp.minimum((i + 1) * per, nblk - 1), 0)
    banded = lambda w: [pl.BlockSpec((None, tm, w), row), pl.BlockSpec((None, halo, w), prev),
                        pl.BlockSpec((None, halo, w), nxt)]
    return pl.pallas_call(
        functools.partial(_ffn_kernel, nt=nt, tm=tm),
        grid=(b, nt),
        in_specs=banded(D_RNN) + banded(D_ATTN) + banded(d)
        + [_const_spec(wo.shape), _mod_spec(d, 2), _mod_spec(d, 3), _mod_spec(d, 4),
           _const_spec(norm_g.shape),
           _const_spec(wu.shape), _const_spec(cw.shape), _const_spec(cb.shape),
           _const_spec(wd.shape), _mod_spec(d, 5), _const_spec(fg.shape)],
        out_specs=pl.BlockSpec((None, tm, d), row),
        out_shape=jax.ShapeDtypeStruct((b, s, d), F32),
        scratch_shapes=[pltpu.VMEM((tm + 2 * halo, D_RNN + D_ATTN), BF16),
                        pltpu.VMEM((tm, d), F32),
                        pltpu.VMEM((tm + 2 * halo, d), BF16),
                        pltpu.VMEM((2, FFN_CHUNK // LANES, tm + 2 * halo, LANES), F32),
                        pltpu.VMEM((2, FFN_CHUNK // LANES, tm + 2 * halo, LANES), F32),
                        pltpu.VMEM((2, tm, FFN_CHUNK), BF16),
                        pltpu.VMEM((tm, d), F32),
                        pltpu.VMEM((d // LANES, tm, LANES), F32)],
        compiler_params=_params("arbitrary", "arbitrary"),
        name="out_proj_conv_ffn",
    )(rn, rn, rn, an, an, an, x, x, x, wo, mod, mod, mod, norm_g, wu, cw, cb, wd, mod, fg)


def _rope_tables(s):
    half = HEAD_DIM // 4
    inv = ROPE_THETA ** (-np.arange(half, dtype=np.float64) / half)

    def tables(npos, lanes_first):
        ang = np.arange(npos, dtype=np.float64)[:, None] * inv[None, :]
        zero = np.zeros((npos, 2 * half))
        cos = np.concatenate([np.cos(ang)] * 2, axis=1)
        sin = np.concatenate([-np.sin(ang), np.sin(ang)], axis=1)
        order = (lambda t: [t, zero]) if lanes_first else (lambda t: [zero, t])
        reps = LANES // HEAD_DIM
        return [jnp.asarray(np.tile(np.concatenate(order(t), axis=1), (1, reps)).astype(np.float32))
                for t in (cos, sin)]

    cosr, sinr = tables(s // GRID_W, True)
    cosc, sinc = tables(GRID_W, False)
    return cosr, sinr, cosc, sinc


def _gate_weights(w_a, w_i):
    per = MXU_DIM // RNN_BLOCK_W
    eye = jnp.eye(per, dtype=F32)

    def dense(w):
        w = w.reshape(RNN_BLOCKS // per, per, RNN_BLOCK_W, RNN_BLOCK_W)
        m = w[:, :, :, None, :] * eye[None, :, None, :, None]
        return m.reshape(RNN_BLOCKS // per, MXU_DIM, MXU_DIM)

    return jnp.concatenate([dense(w_a), dense(w_i)], axis=2).astype(BF16)


def _interleave_heads(w, axis):
    shape = w.shape
    w = w.reshape(shape[:axis] + (N_KV_HEADS, Q_PER_KV, HEAD_DIM) + shape[axis + 1:])
    return jnp.swapaxes(w, axis, axis + 1).reshape(shape)


def kernel(x, c, ctx, c_ctx, w_mod, b_mod, norm1_g, w_in, rnn_conv_w, rnn_conv_b, lru_w_a, lru_b_a,
           lru_w_i, lru_b_i, lru_lam, attn_sink, gn_rnn, gn_attn, w_out, norm2_g, w_up, ffn_conv_w,
           ffn_conv_b, w_down, final_g):
    assert w_mod.shape[0] == 1, "one layer: the last layer's context outputs are never consumed"
    b, s, d = x.shape
    lx = ctx.shape[1]

    cond = jnp.concatenate([c, c_ctx[None], jnp.zeros((SUBLANES_F32 - b - 1, d), F32)], axis=0)
    mod = _modulation(cond, b + 1, w_mod[0], b_mod[0])[:, None, :]

    wi = w_in[0].astype(BF16)
    q0 = 2 * D_RNN
    wi = jnp.concatenate([wi[:, :q0], _interleave_heads(wi[:, q0:q0 + D_ATTN], 1),
                          wi[:, q0 + D_ATTN:]], axis=1)
    wo = w_out[0].astype(BF16)
    wo = jnp.concatenate([wo[:D_RNN], _interleave_heads(wo[D_RNN:], 0)], axis=0)
    gn_a = _interleave_heads(gn_attn[0], 0)[None]
    gn_r = gn_rnn[0][None]
    cw = 0.5 * rnn_conv_w[0]
    cb = 0.5 * rnn_conv_b[0][None]
    sp = (0.5 * LRU_C) * jax.nn.softplus(-lru_lam[0])

    tables = _rope_tables(s)
    xc, gx, q, kv = _in_proj(x, mod, None, norm1_g, wi, cw, cb, tables, rope=True, tm=TM_PROJ)
    xcc, _, _, kvx = _in_proj(ctx, mod, b, norm1_g, wi, cw, cb, tables, rope=False, tm=lx)

    h0 = jnp.zeros((b, 1, D_RNN), F32)
    y_dir = None
    for dr in range(2):
        wg = _gate_weights(lru_w_a[0, dr], lru_w_i[0, dr])
        gate_args = (wg, 0.5 * lru_b_a[0, dr][None], 0.5 * lru_b_i[0, dr][None], sp[dr][None])
        rev = dr == 1
        _, h_ctx = _rglru(xcc, *gate_args, h0, reverse=rev, tm=lx, blk=min(lx, SUB_PROJ))
        fin = (y_dir, gx, gn_r) if rev else None
        y_dir, _ = _rglru(xc, *gate_args, h_ctx, reverse=rev, tm=TM_SCAN, blk=SUB_PROJ,
                          final_args=fin)
    rn = y_dir

    an = _attention(q, kv, kvx, _attn_bias(), attn_sink[0] * LOG2E, gn_a)

    return _mixer_out_ffn(rn, an, x, wo, mod, norm2_g, w_up[0].astype(BF16), ffn_conv_w[0],
                          ffn_conv_b[0][None], w_down[0].astype(BF16), final_g[None], tm=TM_FFN)
```

```python
import functools

import jax
import jax.numpy as jnp
import numpy as np
from jax import lax
from jax.experimental import pallas as pl
from jax.experimental.pallas import tpu as pltpu

F32 = jnp.float32
BF16 = jnp.bfloat16

EPS = 1e-6
GRID_W = 64
D_RNN = 512
RNN_BLOCKS = 8
RNN_BLOCK_W = D_RNN // RNN_BLOCKS
RNN_CONV_W = 4
RNN_CONV_LEFT = 2
LRU_C = 8.0
HEAD_DIM = 64
N_Q_HEADS = 8
N_KV_HEADS = 2
Q_PER_KV = N_Q_HEADS // N_KV_HEADS
D_ATTN = N_Q_HEADS * HEAD_DIM
D_KV = N_KV_HEADS * HEAD_DIM
WINDOW = 128
BLOCK_Q = 128
ROPE_THETA = 10000.0
NEG_INF = -1e30
LOG2E = 1.4426950408889634
F32_TINY = 2.0 ** -126
FFN_CONV_W = 3
FFN_CONV_LEFT = 1

LANES = 128
SUBLANES_F32 = 8
SUBLANES_BF16 = 16
MXU_DIM = 256
VMEM_LIMIT = 56 * 1024 * 1024

TM_PROJ = 2048
SUB_PROJ = 512
TM_SCAN = 1024
TM_FFN = 512
FFN_CHUNK = 256
ATTN_QB = 8
ATTN_LOOKAHEAD = 1


def _params(*sem):
    return pltpu.CompilerParams(dimension_semantics=sem, vmem_limit_bytes=VMEM_LIMIT)


def _const_spec(shape):
    zeros = (0,) * len(shape)
    return pl.BlockSpec(shape, lambda *_: zeros, pipeline_mode=pl.Buffered(1))


def _mod_kernel(condt_ref, w_ref, b_ref, o_ref, *, nrows):
    st = condt_ref[...]
    st = st * jax.nn.sigmoid(st)
    w = w_ref[...]
    rows = [jnp.sum(w * st[:, r:r + 1], axis=0, keepdims=True) for r in range(nrows)]
    rows.append(jnp.zeros((o_ref.shape[0] - nrows, w.shape[1]), F32))
    o_ref[...] = jnp.concatenate(rows, axis=0) + b_ref[...]


def _modulation(cond, nrows, w_mod, b_mod):
    rows, d = cond.shape
    n = w_mod.shape[1]
    tn = 768
    return pl.pallas_call(
        functools.partial(_mod_kernel, nrows=nrows),
        grid=(n // tn,),
        in_specs=[pl.BlockSpec((d, rows), lambda j: (0, 0)),
                  pl.BlockSpec((d, tn), lambda j: (0, j)),
                  pl.BlockSpec((1, tn), lambda j: (0, j))],
        out_specs=pl.BlockSpec((rows, tn), lambda j: (0, j)),
        out_shape=jax.ShapeDtypeStruct((rows, n), F32),
        compiler_params=_params("arbitrary"),
        name="modulation",
    )(cond.T, w_mod, b_mod.reshape(1, n))


def _rope_partner(t):
    lane = lax.broadcasted_iota(jnp.int32, t.shape, 1)
    first = (lane % 32) < 16
    return jnp.where(first, pltpu.roll(t, LANES - 16, 1), pltpu.roll(t, 16, 1))


def _inproj_kernel(x_ref, xp_ref, xn_ref, sh_ref, sc_ref, ng_ref, w_ref, cw_ref, cb_ref,
                   cosr_ref, sinr_ref, cosc_ref, sinc_ref,
                   xc_ref, gx_ref, q_ref, kv_ref, hsc, pslab, *, rope, sub, nt):
    i = pl.program_id(1)
    tm = x_ref.shape[0]
    halo = SUBLANES_BF16
    half = sub // 2
    base = halo - RNN_CONV_LEFT
    q0 = D_RNN
    k0 = q0 + D_ATTN
    v0 = k0 + D_KV
    scale = HEAD_DIM ** -0.5 * LOG2E
    nslab = D_RNN // LANES
    gain = ng_ref[...] * (1.0 + sc_ref[...])

    def modulated(x):
        ms = jnp.mean(x * x, axis=-1, keepdims=True)
        return (x * lax.rsqrt(ms + EPS) * gain + sh_ref[...]).astype(BF16)

    zeros = jnp.zeros((halo, hsc.shape[1]), BF16)
    hsc[0:halo] = jnp.where(i > 0, modulated(xp_ref[...]), zeros)
    hsc[halo + tm:] = jnp.where(i < nt - 1, modulated(xn_ref[...]), zeros)

    for n in range(tm // sub):
        hsc[halo + sub * n:halo + sub * (n + 1)] = modulated(x_ref[sub * n:sub * (n + 1), :])

    cw = cw_ref[...]
    cb = cb_ref[...]
    def project(n):
        pr = jnp.dot(hsc[sub * n:sub * (n + 1) + 2 * halo], w_ref[:, :D_RNN],
                     preferred_element_type=F32)
        p = jnp.dot(hsc[halo + sub * n:halo + sub * (n + 1)], w_ref[:, D_RNN:],
                    preferred_element_type=F32)
        return pr, p

    def epilogue(n, pr, p):
        rows = slice(sub * n, sub * (n + 1))
        cols = []
        for s in range(nslab):
            sl = slice(LANES * s, LANES * (s + 1))
            slab = pslab.at[n * nslab + s]
            slab[...] = pr[:, sl]
            taps = [slab[pl.ds(base + j, half, stride=2), :] for j in range(RNN_CONV_W + 1)]
            even = cb[:, sl]
            odd = cb[:, sl]
            for j in range(RNN_CONV_W):
                even = even + taps[j] * cw[j:j + 1, sl]
                odd = odd + taps[j + 1] * cw[j:j + 1, sl]
            cols.append(jnp.concatenate([even, odd], axis=0))
        xc_ref[rows, :] = jnp.concatenate(cols, axis=1)
        gx_ref[rows, :] = jax.nn.gelu(p[:, :D_RNN]).astype(BF16)
        cols = [p[:, q0 + LANES * c:q0 + LANES * (c + 1)] for c in range(D_ATTN // LANES)]
        k = p[:, k0:v0]
        if rope:
            per = sub // GRID_W
            trows = slice(per * n, per * (n + 1))
            expand = lambda r_ref, c_ref: (r_ref[trows, :][:, None, :]
                                           + c_ref[...][None, :, :]).reshape(sub, LANES)
            cos = expand(cosr_ref, cosc_ref)
            sin = expand(sinr_ref, sinc_ref)
            cols = [t * cos + _rope_partner(t) * sin for t in cols]
            k = k * cos + _rope_partner(k) * sin
        for c, t in enumerate(cols):
            q_ref[rows, LANES * c:LANES * (c + 1)] = (t * scale).astype(BF16)
        kv_ref[rows, :D_KV] = k.astype(BF16)
        kv_ref[rows, D_KV:] = p[:, v0:v0 + D_KV].astype(BF16)

    nsub = tm // sub
    ready = project(0)
    for n in range(nsub):
        following = project(n + 1) if n + 1 < nsub else None
        epilogue(n, *ready)
        ready = following


def _mod_spec(d, chunk, row=None):
    return pl.BlockSpec((None, 1, d), lambda bb, i: (bb if row is None else row, 0, chunk))


def _in_proj(x, mod, mod_row, norm_g, w_in, cw, cb, tables, *, rope, tm):
    b, s, d = x.shape
    n = w_in.shape[1]
    nt = s // tm
    sub = min(tm, SUB_PROJ)
    halo = SUBLANES_BF16
    per = tm // halo
    row = lambda bb, i: (bb, i, 0)
    prev = lambda bb, i: (bb, jnp.maximum(i * per - 1, 0), 0)
    nxt = lambda bb, i: (bb, jnp.minimum((i + 1) * per, s // halo - 1), 0)
    cosr, sinr, cosc, sinc = tables
    if rope:
        assert tm % GRID_W == 0
        rtab = pl.BlockSpec((tm // GRID_W, LANES), lambda bb, i: (i, 0))
    else:
        rtab = _const_spec(cosr.shape)
    outs = [(D_RNN, F32), (D_RNN, BF16), (D_ATTN, BF16), (2 * D_KV, BF16)]
    return pl.pallas_call(
        functools.partial(_inproj_kernel, rope=rope, sub=sub, nt=nt),
        grid=(b, nt),
        in_specs=[pl.BlockSpec((None, tm, d), row),
                  pl.BlockSpec((None, halo, d), prev),
                  pl.BlockSpec((None, halo, d), nxt),
                  _mod_spec(d, 0, mod_row), _mod_spec(d, 1, mod_row), _const_spec(norm_g.shape),
                  _const_spec((d, n)), _const_spec(cw.shape), _const_spec(cb.shape),
                  rtab, rtab, _const_spec(cosc.shape), _const_spec(sinc.shape)],
        out_specs=[pl.BlockSpec((None, tm, w), row) for w, _ in outs],
        out_shape=[jax.ShapeDtypeStruct((b, s, w), dt) for w, dt in outs],
        scratch_shapes=[pltpu.VMEM((tm + 2 * halo, d), BF16),
                        pltpu.VMEM((tm // sub * (D_RNN // LANES), sub + 2 * halo, LANES), F32)],
        compiler_params=_params("arbitrary", "arbitrary"),
        name="in_proj_rope" if rope else "in_proj_ctx",
    )(x, x, x, mod, mod, norm_g, w_in, cw, cb, cosr, sinr, cosc, sinc)


def _rglru_kernel(*refs, reverse, final, blk, blk_ctx):
    if final:
        (xc_ref, xcc_ref, wg_ref, ba_ref, bi_ref, sp_ref, yo_ref, gx_ref, gn_ref,
         y_ref, hfin_ref, a_scr, u_scr, y_scr, h_scr) = refs
    else:
        (xc_ref, xcc_ref, wg_ref, ba_ref, bi_ref, sp_ref,
         y_ref, hfin_ref, a_scr, u_scr, y_scr, h_scr) = refs
    i = pl.program_id(0)
    nb, tm, _ = xc_ref.shape

    def coefficients(src_ref):
        n = src_ref.shape[1]
        for bb in range(nb):
            xc = src_ref[bb]
            xb = xc.astype(BF16)
            for g in range(D_RNN // MXU_DIM):
                sl = slice(MXU_DIM * g, MXU_DIM * (g + 1))
                z = jnp.dot(xb[:, sl], wg_ref[g], preferred_element_type=F32)
                tr = jnp.tanh(z[:, :MXU_DIM] + ba_ref[:, sl])
                ti = jnp.tanh(z[:, MXU_DIM:] + bi_ref[:, sl])
                nl = (tr + 1.0) * sp_ref[:, sl]
                a = jnp.exp2(nl * (-LOG2E))
                a_scr[bb, 0:n, sl] = a
                w = jnp.tanh(nl) * (1.0 + a * a)
                u_scr[bb, 0:n, sl] = (w * lax.rsqrt(jnp.maximum(w, F32_TINY))
                                      * ((ti + 1.0) * xc[:, sl]))

    def scan(n, span_rows, hs):
        rows = SUBLANES_F32
        half = span_rows // 2
        ngrp = half // rows
        order = range(rows - 1, -1, -1) if reverse else range(rows)
        parity = (1, 0) if reverse else (0, 1)
        spans = range(n // span_rows - 1, -1, -1) if reverse else range(n // span_rows)

        def group(g, hs, off):
            gg = (ngrp - 1 - g) if reverse else g
            lo = pl.multiple_of(off + gg * rows, rows)
            hi = pl.multiple_of(off + half + gg * rows, rows)
            out = pl.multiple_of(off + gg * 2 * rows, 2 * rows)
            hs = list(hs)
            for j in order:
                for par in parity:
                    start = hi if par else lo
                    for bb in range(nb):
                        av = a_scr.at[bb, pl.ds(start, rows), :]
                        uv = u_scr.at[bb, pl.ds(start, rows), :]
                        yv = y_scr.at[bb, pl.ds(out, 2 * rows), :]
                        hs[bb] = av[j:j + 1, :] * hs[bb] + uv[j:j + 1, :]
                        yv[2 * j + par:2 * j + par + 1, :] = hs[bb]
            return tuple(hs)

        for span in spans:
            hs = lax.fori_loop(0, ngrp, functools.partial(group, off=span * span_rows), hs)
        return hs

    @pl.when(i == 0)
    def _():
        coefficients(xcc_ref)
        zero = jnp.zeros((1, D_RNN), F32)
        hs = scan(xcc_ref.shape[1], blk_ctx, tuple(zero for _ in range(nb)))
        for bb in range(nb):
            h_scr[bb] = hs[bb]

    coefficients(xc_ref)
    hs = scan(tm, blk, tuple(h_scr[bb] for bb in range(nb)))
    for bb in range(nb):
        h_scr[bb] = hs[bb]
        hfin_ref[bb] = hs[bb]

    if final:
        z = gx_ref[...] * (y_scr[...] + yo_ref[...])
        ms = jnp.mean(z * z, axis=-1, keepdims=True)
        y_ref[...] = (z * lax.rsqrt(ms + EPS) * gn_ref[...]).astype(BF16)
    else:
        y_ref[...] = y_scr[...]


def _rglru(xc, xcc, wg, ba, bi, sp, *, reverse, tm, blk, blk_ctx, final_args=None):
    b, s, d = xc.shape
    nt = s // tm
    final = final_args is not None
    assert xcc.shape[1] <= tm and xcc.shape[1] % blk_ctx == 0
    row = (lambda i: (0, nt - 1 - i, 0)) if reverse else (lambda i: (0, i, 0))
    in_specs = [pl.BlockSpec((b, tm, d), row), _const_spec(xcc.shape), _const_spec(wg.shape),
                _const_spec(ba.shape), _const_spec(bi.shape), _const_spec(sp.shape)]
    args = [xc, xcc, wg, ba, bi, sp]
    if final:
        y_other, gx, gn = final_args
        in_specs += [pl.BlockSpec((b, tm, d), row), pl.BlockSpec((b, tm, d), row),
                     _const_spec(gn.shape)]
        args += [y_other, gx, gn]
    return pl.pallas_call(
        functools.partial(_rglru_kernel, reverse=reverse, final=final, blk=blk, blk_ctx=blk_ctx),
        grid=(nt,),
        in_specs=in_specs,
        out_specs=[pl.BlockSpec((b, tm, d), row), pl.BlockSpec((b, 1, d), lambda i: (0, 0, 0))],
        out_shape=[jax.ShapeDtypeStruct((b, s, d), BF16 if final else F32),
                   jax.ShapeDtypeStruct((b, 1, d), F32)],
        scratch_shapes=[pltpu.VMEM((b, tm, d), F32), pltpu.VMEM((b, tm, d), F32),
                        pltpu.VMEM((b, tm, d), F32), pltpu.VMEM((b, 1, d), F32)],
        compiler_params=_params("arbitrary"),
        name="rglru_" + ("bwd" if reverse else "fwd") + ("_final" if final else ""),
    )(*args)


def _attn_kernel(q_ref, kvp_ref, kvc_ref, kvn_ref, kvx_ref, bias_ref, sink_ref, gn_ref, o_ref,
                 keys, vals, keyx, valx, *, qb, nstep):
    ncol = D_ATTN // LANES
    nwin = 3 * BLOCK_Q
    step = pl.program_id(1)
    band = ((slice(0, BLOCK_Q), kvp_ref), (slice(BLOCK_Q, BLOCK_Q * (qb + 1)), kvc_ref),
            (slice(BLOCK_Q * (qb + 1), BLOCK_Q * (qb + 2)), kvn_ref))
    for rows, kv_ref in band:
        keys[rows, :] = kv_ref[:, :D_KV]
        vals[rows, :D_KV] = kv_ref[:, D_KV:]
    keyx[...] = kvx_ref[:, :D_KV]
    valx[:, :D_KV] = kvx_ref[:, D_KV:]
    @pl.when((pl.program_id(0) == 0) & (step == 0))
    def _():
        vals[:, D_KV:] = jnp.ones((vals.shape[0], LANES), BF16)
        valx[:, D_KV:] = jnp.ones((valx.shape[0], LANES), BF16)

    low = lax.broadcasted_iota(jnp.int32, (BLOCK_Q, LANES), 1) < HEAD_DIM
    first = lax.broadcasted_iota(jnp.int32, (2 * BLOCK_Q, 1), 0) < BLOCK_Q
    zero = jnp.zeros((BLOCK_Q, LANES), BF16)
    nt_dims = (((1,), (1,)), ((), ()))
    def block_bias(x):
        variant = 1
        if x == 0:
            variant = jnp.where(step == 0, 0, variant)
        if x == qb - 1:
            variant = jnp.where(step == nstep - 1, 2, variant)
        return bias_ref[variant]

    def scores(x, c, bias):
        win = slice(BLOCK_Q * x, BLOCK_Q * x + nwin)
        t = q_ref[BLOCK_Q * x:BLOCK_Q * (x + 1), LANES * c:LANES * (c + 1)]
        qs = jnp.concatenate([jnp.where(low, t, zero), jnp.where(low, zero, t)], axis=0)
        s_loc = lax.dot_general(qs, keys[win, :], nt_dims, preferred_element_type=F32)
        s_ctx = lax.dot_general(qs, keyx[...], nt_dims, preferred_element_type=F32)
        edge = lambda t, cols: (t.reshape(2, BLOCK_Q, BLOCK_Q) + bias[None, :, cols]).reshape(
            2 * BLOCK_Q, BLOCK_Q)
        first_cols, last_cols = slice(0, BLOCK_Q), slice(2 * BLOCK_Q, nwin)
        s_loc = jnp.concatenate([edge(s_loc[:, first_cols], first_cols),
                                 s_loc[:, BLOCK_Q:2 * BLOCK_Q],
                                 edge(s_loc[:, last_cols], last_cols)], axis=1)
        sink = jnp.where(first, sink_ref[c], sink_ref[ncol + c])
        m = jnp.maximum(jnp.maximum(jnp.max(s_loc, axis=-1, keepdims=True),
                                    jnp.max(s_ctx, axis=-1, keepdims=True)), sink)
        e_loc = jnp.exp2(s_loc - m).astype(BF16)
        e_ctx = jnp.exp2(s_ctx - m).astype(BF16)
        return x, e_loc, e_ctx, jnp.exp2(sink - m)

    def weighted(x, e_loc, e_ctx, e_sink):
        win = slice(BLOCK_Q * x, BLOCK_Q * x + nwin)
        pv = (jnp.dot(e_loc, vals[win, :], preferred_element_type=F32)
              + jnp.dot(e_ctx, valx[...], preferred_element_type=F32))
        o = pv[:, :D_KV] / (pv[:, D_KV:] + e_sink)
        return jnp.where(low, o[:BLOCK_Q], o[BLOCK_Q:])

    def finish(x, outs):
        ms = sum(jnp.sum(t * t, axis=-1, keepdims=True) for t in outs) * (1.0 / D_ATTN)
        inv = lax.rsqrt(ms + EPS)
        for c, t in enumerate(outs):
            sl = slice(LANES * c, LANES * (c + 1))
            o_ref[BLOCK_Q * x:BLOCK_Q * (x + 1), sl] = (t * inv * gn_ref[:, sl]).astype(BF16)

    pending = []
    outs = [[] for _ in range(qb)]

    def retire():
        chain = pending.pop(0)
        x = chain[0]
        outs[x].append(weighted(*chain))
        if len(outs[x]) == ncol:
            finish(x, outs[x])

    for x in range(qb):
        bias = block_bias(x)
        for c in range(ncol):
            pending.append(scores(x, c, bias))
            if len(pending) > ATTN_LOOKAHEAD:
                retire()
    while pending:
        retire()


def _attention(q, kv, kvx, bias, sink, gn):
    b, s, _ = q.shape
    qb = ATTN_QB
    nblk = s // BLOCK_Q
    nstep = nblk // qb
    lx = kvx.shape[1]
    cur = lambda bb, n: (bb, n, 0)
    prev = lambda bb, n: (bb, jnp.maximum(n * qb - 1, 0), 0)
    nxt = lambda bb, n: (bb, jnp.minimum((n + 1) * qb, nblk - 1), 0)
    halo = lambda im: pl.BlockSpec((None, BLOCK_Q, 2 * D_KV), im)
    return pl.pallas_call(
        functools.partial(_attn_kernel, qb=qb, nstep=nstep),
        grid=(b, nstep),
        in_specs=[pl.BlockSpec((None, qb * BLOCK_Q, D_ATTN), cur),
                  halo(prev), pl.BlockSpec((None, qb * BLOCK_Q, 2 * D_KV), cur), halo(nxt),
                  pl.BlockSpec((None, lx, 2 * D_KV), lambda bb, n: (bb, 0, 0)),
                  _const_spec(bias.shape), pl.BlockSpec(memory_space=pltpu.SMEM),
                  _const_spec(gn.shape)],
        out_specs=pl.BlockSpec((None, qb * BLOCK_Q, D_ATTN), cur),
        out_shape=jax.ShapeDtypeStruct((b, s, D_ATTN), BF16),
        scratch_shapes=[pltpu.VMEM(((qb + 2) * BLOCK_Q, D_KV), BF16),
                        pltpu.VMEM(((qb + 2) * BLOCK_Q, D_KV + LANES), BF16),
                        pltpu.VMEM((lx, D_KV), BF16),
                        pltpu.VMEM((lx, D_KV + LANES), BF16)],
        compiler_params=_params("arbitrary", "arbitrary"),
        name="attention",
    )(q, kv, kv, kv, kvx, bias, sink, gn)


def _attn_bias():
    i = np.arange(BLOCK_Q)[:, None]
    j = np.arange(3 * BLOCK_Q)[None, :]
    band = np.abs(i + BLOCK_Q - j) <= WINDOW
    variants = [band & (j >= BLOCK_Q), band, band & (j < 2 * BLOCK_Q)]
    return jnp.asarray(np.stack([np.where(ok, 0.0, NEG_INF) for ok in variants]).astype(np.float32))


def _ffn_kernel(rn_ref, rnp_ref, rnn_ref, an_ref, anp_ref, ann_ref, x_ref, xp_ref, xn_ref,
                wo_ref, g1_ref, sh_ref, sc_ref, ng_ref, wu_ref, cw_ref, cb_ref, wd_ref, g2_ref, fg_ref,
                o_ref, mix, x1s, hext, ug, uv, act, acc, nat, *, nt, tm):
    d_ff = wd_ref.shape[0]
    nc = d_ff // FFN_CHUNK
    span = lambda j, branch: slice(branch * d_ff + j * FFN_CHUNK, branch * d_ff + (j + 1) * FFN_CHUNK)
    i = pl.program_id(1)
    halo = SUBLANES_BF16
    body = slice(halo, halo + tm)
    tail = slice(halo + tm, 2 * halo + tm)
    for rows, r_ref, a_ref in ((slice(0, halo), rnp_ref, anp_ref), (body, rn_ref, an_ref),
                               (tail, rnn_ref, ann_ref)):
        mix[rows, :D_RNN] = r_ref[...]
        mix[rows, D_RNN:] = a_ref[...]
    proj = jnp.dot(mix[...], wo_ref[...], preferred_element_type=F32)

    def residual(xin_ref, rows):
        return xin_ref[...] + g1_ref[...] * proj[rows]

    gain = ng_ref[...] * (1.0 + sc_ref[...])

    def modulated(x1):
        ms = jnp.mean(x1 * x1, axis=-1, keepdims=True)
        return (x1 * lax.rsqrt(ms + EPS) * gain + sh_ref[...]).astype(BF16)

    x1 = residual(x_ref, body)
    x1s[...] = x1
    hext[body] = modulated(x1)
    zeros = jnp.zeros((halo, hext.shape[1]), BF16)
    hext[0:halo] = jnp.where(i > 0, modulated(residual(xp_ref, slice(0, halo))), zeros)
    hext[tail] = jnp.where(i < nt - 1, modulated(residual(xn_ref, tail)), zeros)

    base = halo - FFN_CONV_LEFT

    half = tm // 2
    nslab = FFN_CHUNK // LANES

    def up(j, slot):
        he = hext[...]
        g = jnp.dot(he, wu_ref[:, span(j,0)], preferred_element_type=F32)
        v = jnp.dot(he, wu_ref[:, span(j,1)], preferred_element_type=F32)
        for s in range(nslab):
            ug[slot, s] = g[:, LANES * s:LANES * (s + 1)]
            uv[slot, s] = v[:, LANES * s:LANES * (s + 1)]

    def conv(u, slot, cw, cb):
        cols = []
        for s in range(nslab):
            sl = slice(LANES * s, LANES * (s + 1))
            taps = [u[slot, s, pl.ds(base + t, half, stride=2), :] for t in range(FFN_CONV_W + 1)]
            even = cb[:, sl]
            odd = cb[:, sl]
            for t in range(FFN_CONV_W):
                even = even + taps[t] * cw[t:t + 1, sl]
                odd = odd + taps[t + 1] * cw[t:t + 1, sl]
            cols.append(jnp.concatenate([even, odd], axis=0))
        return jnp.concatenate(cols, axis=1)

    def gate(j, slot):
        hg = 0.5 * conv(ug, slot, cw_ref[:, span(j,0)], cb_ref[:, span(j,0)])
        silu = hg * jnp.tanh(hg) + hg
        val = conv(uv, slot, cw_ref[:, span(j,1)], cb_ref[:, span(j,1)])
        act[slot] = (silu * val).astype(BF16)

    def down(j, slot):
        return jnp.dot(act[slot], wd_ref[span(j,0), :], preferred_element_type=F32)

    up(0, 0)
    for j in range(nc):
        if j + 1 < nc:
            up(j + 1, (j + 1) % 2)
        gate(j, j % 2)
        if j == 1:
            acc[...] = down(0, 0)
        elif j > 1:
            acc[...] += down(j - 1, (j - 1) % 2)
    mixed = acc[...] + down(nc - 1, (nc - 1) % 2)
    nlane = mixed.shape[1] // LANES
    for s in range(nlane):
        sl = slice(LANES * s, LANES * (s + 1))
        nat[s, pl.ds(0, half, stride=2), :] = mixed[:half, sl]
        nat[s, pl.ds(1, half, stride=2), :] = mixed[half:, sl]
    ffn = jnp.concatenate([nat[s] for s in range(nlane)], axis=1)
    y = x1s[...] + g2_ref[...] * ffn
    ms = jnp.mean(y * y, axis=-1, keepdims=True)
    o_ref[...] = y * lax.rsqrt(ms + EPS) * fg_ref[...]


def _mixer_out_ffn(rn, an, x, wo, mod, norm_g, wu, cw, cb, wd, fg, *, tm):
    b, s, d = x.shape
    nt = s // tm
    assert wd.shape[0] % FFN_CHUNK == 0 and (wd.shape[0] // FFN_CHUNK) % 2 == 1
    halo = SUBLANES_BF16
    per = tm // halo
    nblk = s // halo
    row = lambda bb, i: (bb, i, 0)
    prev = lambda bb, i: (bb, jnp.maximum(i * per - 1, 0), 0)
    nxt = lambda bb, i: (bb, jnp.minimum((i + 1) * per, nblk - 1), 0)
    banded = lambda w: [pl.BlockSpec((None, tm, w), row), pl.BlockSpec((None, halo, w), prev),
                        pl.BlockSpec((None, halo, w), nxt)]
    return pl.pallas_call(
        functools.partial(_ffn_kernel, nt=nt, tm=tm),
        grid=(b, nt),
        in_specs=banded(D_RNN) + banded(D_ATTN) + banded(d)
        + [_const_spec(wo.shape), _mod_spec(d, 2), _mod_spec(d, 3), _mod_spec(d, 4),
           _const_spec(norm_g.shape),
           _const_spec(wu.shape), _const_spec(cw.shape), _const_spec(cb.shape),
           _const_spec(wd.shape), _mod_spec(d, 5), _const_spec(fg.shape)],
        out_specs=pl.BlockSpec((None, tm, d), row),
        out_shape=jax.ShapeDtypeStruct((b, s, d), F32),
        scratch_shapes=[pltpu.VMEM((tm + 2 * halo, D_RNN + D_ATTN), BF16),
                        pltpu.VMEM((tm, d), F32),
                        pltpu.VMEM((tm + 2 * halo, d), BF16),
                        pltpu.VMEM((2, FFN_CHUNK // LANES, tm + 2 * halo, LANES), F32),
                        pltpu.VMEM((2, FFN_CHUNK // LANES, tm + 2 * halo, LANES), F32),
                        pltpu.VMEM((2, tm, FFN_CHUNK), BF16),
                        pltpu.VMEM((tm, d), F32),
                        pltpu.VMEM((d // LANES, tm, LANES), F32)],
        compiler_params=_params("arbitrary", "arbitrary"),
        name="out_proj_conv_ffn",
    )(rn, rn, rn, an, an, an, x, x, x, wo, mod, mod, mod, norm_g, wu, cw, cb, wd, mod, fg)


def _rope_tables(s):
    half = HEAD_DIM // 4
    inv = ROPE_THETA ** (-np.arange(half, dtype=np.float64) / half)

    def tables(npos, lanes_first):
        ang = np.arange(npos, dtype=np.float64)[:, None] * inv[None, :]
        zero = np.zeros((npos, 2 * half))
        cos = np.concatenate([np.cos(ang)] * 2, axis=1)
        sin = np.concatenate([-np.sin(ang), np.sin(ang)], axis=1)
        order = (lambda t: [t, zero]) if lanes_first else (lambda t: [zero, t])
        reps = LANES // HEAD_DIM
        return [jnp.asarray(np.tile(np.concatenate(order(t), axis=1), (1, reps)).astype(np.float32))
                for t in (cos, sin)]

    cosr, sinr = tables(s // GRID_W, True)
    cosc, sinc = tables(GRID_W, False)
    return cosr, sinr, cosc, sinc


def _gate_weights(w_a, w_i):
    per = MXU_DIM // RNN_BLOCK_W
    eye = jnp.eye(per, dtype=F32)

    def dense(w):
        w = w.reshape(RNN_BLOCKS // per, per, RNN_BLOCK_W, RNN_BLOCK_W)
        m = w[:, :, :, None, :] * eye[None, :, None, :, None]
        return m.reshape(RNN_BLOCKS // per, MXU_DIM, MXU_DIM)

    return jnp.concatenate([dense(w_a), dense(w_i)], axis=2).astype(BF16)


def _interleave_heads(w, axis):
    shape = w.shape
    w = w.reshape(shape[:axis] + (N_KV_HEADS, Q_PER_KV, HEAD_DIM) + shape[axis + 1:])
    return jnp.swapaxes(w, axis, axis + 1).reshape(shape)


def kernel(x, c, ctx, c_ctx, w_mod, b_mod, norm1_g, w_in, rnn_conv_w, rnn_conv_b, lru_w_a, lru_b_a,
           lru_w_i, lru_b_i, lru_lam, attn_sink, gn_rnn, gn_attn, w_out, norm2_g, w_up, ffn_conv_w,
           ffn_conv_b, w_down, final_g):
    assert w_mod.shape[0] == 1, "one layer: the last layer's context outputs are never consumed"
    b, s, d = x.shape
    lx = ctx.shape[1]

    cond = jnp.concatenate([c, c_ctx[None], jnp.zeros((SUBLANES_F32 - b - 1, d), F32)], axis=0)
    mod = _modulation(cond, b + 1, w_mod[0], b_mod[0])[:, None, :]

    wi = w_in[0].astype(BF16)
    q0 = 2 * D_RNN
    wi = jnp.concatenate([wi[:, :q0], _interleave_heads(wi[:, q0:q0 + D_ATTN], 1),
                          wi[:, q0 + D_ATTN:]], axis=1)
    wo = w_out[0].astype(BF16)
    wo = jnp.concatenate([wo[:D_RNN], _interleave_heads(wo[D_RNN:], 0)], axis=0)
    gn_a = _interleave_heads(gn_attn[0], 0)[None]
    gn_r = gn_rnn[0][None]
    cw = 0.5 * rnn_conv_w[0]
    cb = 0.5 * rnn_conv_b[0][None]
    sp = (0.5 * LRU_C) * jax.nn.softplus(-lru_lam[0])

    tables = _rope_tables(s)
    xc, gx, q, kv = _in_proj(x, mod, None, norm1_g, wi, cw, cb, tables, rope=True, tm=TM_PROJ)
    xcc, _, _, kvx = _in_proj(ctx, mod, b, norm1_g, wi, cw, cb, tables, rope=False, tm=lx)

    y_dir = None
    for dr in range(2):
        wg = _gate_weights(lru_w_a[0, dr], lru_w_i[0, dr])
        gate_args = (wg, 0.5 * lru_b_a[0, dr][None], 0.5 * lru_b_i[0, dr][None], sp[dr][None])
        rev = dr == 1
        fin = (y_dir, gx, gn_r) if rev else None
        y_dir, _ = _rglru(xc, xcc, *gate_args, reverse=rev, tm=TM_SCAN, blk=SUB_PROJ,
                          blk_ctx=min(lx, SUB_PROJ), final_args=fin)
    rn = y_dir

    an = _attention(q, kv, kvx, _attn_bias(), attn_sink[0] * LOG2E, gn_a)

    return _mixer_out_ffn(rn, an, x, wo, mod, norm2_g, w_up[0].astype(BF16), ffn_conv_w[0],
                          ffn_conv_b[0][None], w_down[0].astype(BF16), final_g[None], tm=TM_FFN)
```
